```python
import math, functools
import jax, jax.numpy as jnp
from jax import lax
import numpy as np

D_MODEL = 1024
BATCH = 8
SEQ = 4096
DEPTH = 1
DEC_BATCH = 32
DEC_SEQ = 4
PAST_LEN = 16384
PAGE_SIZE = 128

HEAD_DIM = 128
DN_HEADS = 4
MOBA_HEADS = 4
DN_WIDTH = DN_HEADS * HEAD_DIM
ATT_WIDTH = MOBA_HEADS * HEAD_DIM
MIX_WIDTH = DN_WIDTH + ATT_WIDTH
CONV_W = 4
CONV_DIM = 3 * DN_WIDTH
DN_CHUNK = 64
MOBA_BLOCK = 256
MOBA_TOPK = 3
Q_CHUNK = 64
N_EXPERTS = 32
TOP_K = 4
D_FF = D_MODEL
SWIGLU_LIMIT = 7.0
SWIGLU_ALPHA = 1.702
MOE_BLOCK = 128
EPS = 1e-6
OFF_Z = CONV_DIM
OFF_A = OFF_Z + DN_WIDTH
OFF_B = OFF_A + DN_HEADS
OFF_MQ = OFF_B + DN_HEADS
OFF_MK = OFF_MQ + ATT_WIDTH
OFF_MV = OFF_MK + ATT_WIDTH
PROJ = OFF_MV + ATT_WIDTH

kernel_name = "hybrid_gdn_moba_moe_step"


def _rmsnorm(x, w):
    xf = x.astype(jnp.float32)
    y = xf * lax.rsqrt(jnp.mean(xf * xf, axis=-1, keepdims=True) + EPS)
    return (y * w.astype(jnp.float32)).astype(x.dtype)


def _l2norm(x):
    return x * lax.rsqrt(jnp.sum(x * x, axis=-1, keepdims=True) + EPS)


def _short_conv(u, buf, w_conv):
    L = u.shape[1]
    up = jnp.concatenate([buf.astype(u.dtype), u], axis=1)
    y = up[:, 0:L] * w_conv[0]
    for i in range(1, CONV_W):
        y = y + up[:, i:i + L] * w_conv[i]
    return jax.nn.silu(y), up[:, L:]


def _gated_delta(q, k, v, g, beta, s0, chunk):
    B, L, H, dk = q.shape
    dv = v.shape[-1]
    n = L // chunk

    def blocks(t):
        return t.reshape(B, n, chunk, H, -1).transpose(1, 0, 3, 2, 4)

    qc, kc, vc = blocks(q), blocks(k), blocks(v)
    gc = blocks(g)[..., 0]
    bc = blocks(beta)[..., 0]
    gcum = jnp.cumsum(gc, axis=-1)
    causal = jnp.tril(jnp.ones((chunk, chunk), bool))
    strict = jnp.tril(jnp.ones((chunk, chunk), bool), -1)
    decay = jnp.exp(jnp.where(causal, gcum[..., :, None] - gcum[..., None, :], -jnp.inf))
    kb = kc * bc[..., None]
    lower = jnp.where(strict, jnp.einsum('nbhid,nbhjd->nbhij', kb, kc) * decay, 0.0)
    tmat = lower + jnp.eye(chunk, dtype=jnp.float32)
    u = lax.linalg.triangular_solve(tmat, vc * bc[..., None], left_side=True, lower=True, unit_diagonal=True)
    w = lax.linalg.triangular_solve(tmat, kb * jnp.exp(gcum)[..., None], left_side=True, lower=True, unit_diagonal=True)
    qk = jnp.einsum('nbhid,nbhjd->nbhij', qc, kc) * decay

    def step(s, xs):
        qi, ki, ui, wi, gi, ai = xs
        vnew = ui - jnp.einsum('bhck,bhkv->bhcv', wi, s)
        o = (jnp.einsum('bhck,bhkv->bhcv', qi * jnp.exp(gi)[..., None], s)
             + jnp.einsum('bhij,bhjv->bhiv', ai, vnew))
        glast = gi[..., -1:]
        s = (s * jnp.exp(glast)[..., None]
             + jnp.einsum('bhck,bhcv->bhkv', ki * jnp.exp(glast - gi)[..., None], vnew))
        return s, o

    s, o = lax.scan(step, s0, (qc, kc, u, w, gcum, qk))
    return o.transpose(1, 0, 3, 2, 4).reshape(B, L, H, dv), s


def _moba_prompt(q, k, v):
    B, S, H, dh = q.shape
    n_full = S // MOBA_BLOCK
    n_blk = -(-S // MOBA_BLOCK)
    pad = n_blk * MOBA_BLOCK - S

    def to_blocks(t):
        t = jnp.pad(t, ((0, 0), (0, pad), (0, 0), (0, 0)))
        return t.reshape(B, n_blk, MOBA_BLOCK, H, dh).transpose(0, 3, 1, 2, 4)

    kb, vb = to_blocks(k), to_blocks(v)
    n_sel = min(MOBA_TOPK, n_full)
    kmean = kb[:, :, :n_full].astype(jnp.float32).mean(axis=3)
    b_idx = jnp.arange(B)[:, None, None, None]
    h_idx = jnp.arange(H)[None, :, None, None]
    scale = HEAD_DIM ** -0.5

    def one_chunk(c):
        q0 = c * Q_CHUNK
        qc = lax.dynamic_slice_in_dim(q, q0, Q_CHUNK, axis=1)
        blk = q0 // MOBA_BLOCK
        qpos = q0 + jnp.arange(Q_CHUNK)
        kpos = blk * MOBA_BLOCK + jnp.arange(MOBA_BLOCK)
        k_own = lax.dynamic_index_in_dim(kb, blk, axis=2, keepdims=False)
        v_own = lax.dynamic_index_in_dim(vb, blk, axis=2, keepdims=False)
        s_own = jnp.einsum('bqhd,bhkd->bhqk', qc, k_own).astype(jnp.float32) * scale
        s_own = jnp.where(kpos[None, :] <= qpos[:, None], s_own, -jnp.inf)
        if n_sel == 0:
            p = jax.nn.softmax(s_own, axis=-1).astype(v.dtype)
            return jnp.einsum('bhqk,bhkd->bqhd', p, v_own)
        gate = jnp.einsum('bqhd,bhjd->bhqj', qc.astype(jnp.float32), kmean)
        gate = jnp.where(jnp.arange(n_full) < blk, gate, -jnp.inf)
        _, sel = lax.top_k(gate, n_sel)
        kg = kb[b_idx, h_idx, sel]
        vg = vb[b_idx, h_idx, sel]
        s_past = jnp.einsum('bqhd,bhqsnd->bhqsn', qc, kg).astype(jnp.float32) * scale
        s_past = jnp.where((jnp.arange(n_sel) < blk)[:, None], s_past, -jnp.inf)
        s = jnp.concatenate([s_past.reshape(B, H, Q_CHUNK, n_sel * MOBA_BLOCK), s_own], axis=-1)
        p = jax.nn.softmax(s, axis=-1).astype(v.dtype)
        p_past = p[..., :n_sel * MOBA_BLOCK].reshape(B, H, Q_CHUNK, n_sel, MOBA_BLOCK)
        return (jnp.einsum('bhqsn,bhqsnd->bqhd', p_past, vg)
                + jnp.einsum('bhqk,bhkd->bqhd', p[..., n_sel * MOBA_BLOCK:], v_own))

    o = lax.map(one_chunk, jnp.arange(S // Q_CHUNK))
    return o.transpose(1, 0, 2, 3, 4).reshape(B, S, H, dh)


def _moba_sample(q, k, v, cache_k, cache_v, page_table):
    DB, T, H, dh = q.shape
    n_pages = page_table.shape[1]
    past = n_pages * PAGE_SIZE
    ppb = MOBA_BLOCK // PAGE_SIZE
    cur = past // MOBA_BLOCK
    n_sel = min(MOBA_TOPK, cur)
    scale = HEAD_DIM ** -0.5
    qh, kh, vh = (t.transpose(0, 2, 1, 3) for t in (q, k, v))

    def paged_rows(pool, pages):
        r = pool[pages]
        return r.transpose(0, 2, 1, 3, 4).reshape(DB, H, -1, dh)

    k_own = jnp.concatenate([paged_rows(cache_k, page_table[:, cur * ppb:]).astype(kh.dtype), kh], axis=2)
    v_own = jnp.concatenate([paged_rows(cache_v, page_table[:, cur * ppb:]).astype(vh.dtype), vh], axis=2)
    n_own = k_own.shape[2]
    kpos = jnp.arange(n_own)
    qpos = n_own - T + jnp.arange(T)
    s_own = jnp.einsum('bhtd,bhkd->bhtk', qh, k_own).astype(jnp.float32) * scale
    s_own = jnp.where(kpos[None, :] <= qpos[:, None], s_own, -jnp.inf)
    if n_sel == 0:
        p = jax.nn.softmax(s_own, axis=-1).astype(v_own.dtype)
        return jnp.einsum('bhtk,bhkd->bhtd', p, v_own).transpose(0, 2, 1, 3)
    k_past = paged_rows(cache_k, page_table[:, :cur * ppb])
    kmean = k_past.astype(jnp.float32).reshape(DB, H, cur, MOBA_BLOCK, dh).mean(axis=3)
    gate = jnp.einsum('bhtd,bhjd->bhtj', qh.astype(jnp.float32), kmean)
    _, sel = lax.top_k(gate, n_sel)
    logical = sel[..., None] * ppb + jnp.arange(ppb)
    phys = page_table[jnp.arange(DB)[:, None, None, None, None], logical]
    h_idx = jnp.arange(H)[None, :, None, None, None]
    kg = cache_k[phys, h_idx].reshape(DB, H, T, n_sel * MOBA_BLOCK, dh)
    vg = cache_v[phys, h_idx].reshape(DB, H, T, n_sel * MOBA_BLOCK, dh)
    s_past = jnp.einsum('bhtd,bhtnd->bhtn', qh, kg).astype(jnp.float32) * scale
    s = jnp.concatenate([s_past, s_own], axis=-1)
    p = jax.nn.softmax(s, axis=-1).astype(v_own.dtype)
    o = (jnp.einsum('bhtn,bhtnd->bhtd', p[..., :n_sel * MOBA_BLOCK], vg.astype(v_own.dtype))
         + jnp.einsum('bhtk,bhkd->bhtd', p[..., n_sel * MOBA_BLOCK:], v_own))
    return o.transpose(0, 2, 1, 3)


def _moe(x2d, w_router, b_router, w_gate, b_gate, w_up, b_up, w_down, b_down):
    T = x2d.shape[0]
    logits = (x2d @ w_router + b_router).astype(jnp.float32)
    top_val, top_idx = lax.top_k(logits, TOP_K)
    gates = jax.nn.softmax(top_val, axis=-1)
    A = T * TOP_K
    flat_e = top_idx.reshape(-1)
    order = jnp.argsort(flat_e)
    sorted_e = flat_e[order]
    tok = (order // TOP_K).astype(jnp.int32)
    counts = jnp.zeros((N_EXPERTS,), jnp.int32).at[flat_e].add(1)
    starts = jnp.cumsum(counts) - counts
    padded = (counts + MOE_BLOCK - 1) // MOE_BLOCK * MOE_BLOCK
    pends = jnp.cumsum(padded)
    pstarts = pends - padded
    dest = pstarts[sorted_e] + (jnp.arange(A, dtype=jnp.int32) - starts[sorted_e])
    n_blocks = -(-A // MOE_BLOCK) + N_EXPERTS
    slot_tok = jnp.full((n_blocks * MOE_BLOCK,), T, jnp.int32).at[dest].set(tok)
    block_e = jnp.minimum(jnp.searchsorted(pends, jnp.arange(n_blocks, dtype=jnp.int32) * MOE_BLOCK, side='right'), N_EXPERTS - 1)
    xpad = jnp.concatenate([x2d, jnp.zeros((1, x2d.shape[1]), x2d.dtype)], axis=0)
    xs = xpad[slot_tok].reshape(n_blocks, MOE_BLOCK, x2d.shape[1])

    def expert_block(args):
        xb, e = args
        gt = jnp.minimum(xb @ w_gate[e] + b_gate[e], SWIGLU_LIMIT)
        up = jnp.clip(xb @ w_up[e] + b_up[e], -SWIGLU_LIMIT, SWIGLU_LIMIT)
        glu = gt * jax.nn.sigmoid(SWIGLU_ALPHA * gt)
        return ((up + 1.0) * glu) @ w_down[e] + b_down[e]

    ys = lax.map(expert_block, (xs, block_e)).reshape(n_blocks * MOE_BLOCK, -1)
    contrib = ys[dest] * gates.reshape(-1)[order][:, None].astype(ys.dtype)
    return jnp.zeros((T, ys.shape[1]), ys.dtype).at[tok].add(contrib)


def _layer(x, conv_prev, ssm_prev, attend, ln1_w, w_in, w_conv, a_log, dt_bias, dn_norm_w,
           q_norm_w, k_norm_w, w_o, ln2_w, w_router, b_router, w_gate, b_gate, w_up, b_up,
           w_down, b_down):
    B, L, _ = x.shape
    h = _rmsnorm(x, ln1_w) @ w_in
    qkv, conv_tail = _short_conv(h[..., :CONV_DIM], conv_prev, w_conv)
    qd, kd, vd = (qkv[..., i * DN_WIDTH:(i + 1) * DN_WIDTH].reshape(B, L, DN_HEADS, HEAD_DIM).astype(jnp.float32)
                  for i in range(3))
    qd = _l2norm(qd) * (HEAD_DIM ** -0.5)
    kd = _l2norm(kd)
    z = h[..., OFF_Z:OFF_A].reshape(B, L, DN_HEADS, HEAD_DIM).astype(jnp.float32)
    g = -jnp.exp(a_log.astype(jnp.float32)) * jax.nn.softplus(h[..., OFF_A:OFF_B].astype(jnp.float32) + dt_bias.astype(jnp.float32))
    beta = jax.nn.sigmoid(h[..., OFF_B:OFF_MQ].astype(jnp.float32))
    od, ssm_new = _gated_delta(qd, kd, vd, g, beta, ssm_prev.astype(jnp.float32), math.gcd(DN_CHUNK, L))
    od = _rmsnorm(od, dn_norm_w) * jax.nn.silu(z)
    qm = _rmsnorm(h[..., OFF_MQ:OFF_MK].reshape(B, L, MOBA_HEADS, HEAD_DIM), q_norm_w)
    km = _rmsnorm(h[..., OFF_MK:OFF_MV].reshape(B, L, MOBA_HEADS, HEAD_DIM), k_norm_w)
    vm = h[..., OFF_MV:PROJ].reshape(B, L, MOBA_HEADS, HEAD_DIM)
    om = attend(qm, km, vm)
    mix = jnp.concatenate([od.astype(x.dtype).reshape(B, L, DN_WIDTH), om.astype(x.dtype).reshape(B, L, ATT_WIDTH)], axis=-1)
    x = x + mix @ w_o
    f = _moe(_rmsnorm(x, ln2_w).reshape(B * L, D_MODEL), w_router, b_router, w_gate, b_gate, w_up, b_up, w_down, b_down)
    x = x + f.reshape(B, L, D_MODEL).astype(x.dtype)
    return x, conv_tail, ssm_new, km.transpose(0, 2, 1, 3), vm.transpose(0, 2, 1, 3)


def setup_inputs(seed: int = 0) -> dict:
    key = jax.random.key(seed)
    ks = jax.random.split(key, 26)
    n_pages = PAST_LEN // PAGE_SIZE
    n_used = DEC_BATCH * n_pages
    n_phys = n_used + (n_used + 3) // 4

    def nrm(k, shape, s):
        return jax.random.normal(k, shape, jnp.float32) * s

    return {
        "x_prompt": nrm(ks[0], (BATCH, SEQ, D_MODEL), 1.0),
        "x_sample": nrm(ks[1], (DEC_BATCH, DEC_SEQ, D_MODEL), 1.0),
        "cache_k": nrm(ks[2], (DEPTH, n_phys, MOBA_HEADS, PAGE_SIZE, HEAD_DIM), 1.0),
        "cache_v": nrm(ks[3], (DEPTH, n_phys, MOBA_HEADS, PAGE_SIZE, HEAD_DIM), 1.0),
        "state_conv": nrm(ks[4], (DEPTH, DEC_BATCH, CONV_W - 1, CONV_DIM), 1.0),
        "state_ssm": nrm(ks[5], (DEPTH, DEC_BATCH, DN_HEADS, HEAD_DIM, HEAD_DIM), 0.5),
        "page_table": jax.random.permutation(ks[6], n_phys)[:n_used].reshape(DEC_BATCH, n_pages).astype(jnp.int32),
        "ln1_w": 1.0 + nrm(ks[7], (DEPTH, D_MODEL), 0.02),
        "w_in": nrm(ks[8], (DEPTH, D_MODEL, PROJ), D_MODEL ** -0.5),
        "w_conv": nrm(ks[9], (DEPTH, CONV_W, CONV_DIM), CONV_W ** -0.5),
        "a_log": jnp.log(jax.random.uniform(ks[10], (DEPTH, DN_HEADS), jnp.float32, 1.0, 16.0)),
        "dt_bias": nrm(ks[11], (DEPTH, DN_HEADS), 0.1),
        "dn_norm_w": 1.0 + nrm(ks[12], (DEPTH, HEAD_DIM), 0.02),
        "q_norm_w": 1.0 + nrm(ks[13], (DEPTH, HEAD_DIM), 0.02),
        "k_norm_w": 1.0 + nrm(ks[14], (DEPTH, HEAD_DIM), 0.02),
        "w_o": nrm(ks[15], (DEPTH, MIX_WIDTH, D_MODEL), MIX_WIDTH ** -0.5),
        "ln2_w": 1.0 + nrm(ks[16], (DEPTH, D_MODEL), 0.02),
        "w_router": nrm(ks[17], (DEPTH, D_MODEL, N_EXPERTS), D_MODEL ** -0.5),
        "b_router": nrm(ks[18], (DEPTH, N_EXPERTS), 0.01),
        "w_gate": nrm(ks[19], (DEPTH, N_EXPERTS, D_MODEL, D_FF), D_MODEL ** -0.5),
        "b_gate": nrm(ks[20], (DEPTH, N_EXPERTS, D_FF), 0.02),
        "w_up": nrm(ks[21], (DEPTH, N_EXPERTS, D_MODEL, D_FF), D_MODEL ** -0.5),
        "b_up": nrm(ks[22], (DEPTH, N_EXPERTS, D_FF), 0.02),
        "w_down": nrm(ks[23], (DEPTH, N_EXPERTS, D_FF, D_MODEL), D_FF ** -0.5),
        "b_down": nrm(ks[24], (DEPTH, N_EXPERTS, D_MODEL), 0.02),
    }


def reference(x_prompt, x_sample, cache_k, cache_v, state_conv, state_ssm, page_table,
              ln1_w, w_in, w_conv, a_log, dt_bias, dn_norm_w, q_norm_w, k_norm_w, w_o, ln2_w,
              w_router, b_router, w_gate, b_gate, w_up, b_up, w_down, b_down):
    y_prompt, y_sample = x_prompt, x_sample
    k_p, v_p, conv_p, ssm_p = [], [], [], []
    k_s, v_s, conv_s, ssm_s = [], [], [], []
    B = x_prompt.shape[0]
    for l in range(DEPTH):
        lw = (ln1_w[l], w_in[l], w_conv[l], a_log[l], dt_bias[l], dn_norm_w[l], q_norm_w[l],
              k_norm_w[l], w_o[l], ln2_w[l], w_router[l], b_router[l], w_gate[l], b_gate[l],
              w_up[l], b_up[l], w_down[l], b_down[l])
        conv0 = jnp.zeros((B, CONV_W - 1, CONV_DIM), x_prompt.dtype)
        ssm0 = jnp.zeros((B, DN_HEADS, HEAD_DIM, HEAD_DIM), jnp.float32)
        y_prompt, c1, s1, k1, v1 = _layer(y_prompt, conv0, ssm0, _moba_prompt, *lw)
        attend_s = functools.partial(_moba_sample, cache_k=cache_k[l], cache_v=cache_v[l], page_table=page_table)
        y_sample, c2, s2, k2, v2 = _layer(y_sample, state_conv[l], state_ssm[l], attend_s, *lw)
        k_p.append(k1); v_p.append(v1); conv_p.append(c1); ssm_p.append(s1)
        k_s.append(k2); v_s.append(v2); conv_s.append(c2); ssm_s.append(s2)
    return (y_prompt, y_sample, jnp.stack(k_p), jnp.stack(v_p), jnp.stack(conv_p), jnp.stack(ssm_p),
            jnp.stack(k_s), jnp.stack(v_s), jnp.stack(conv_s), jnp.stack(ssm_s))
```

```python
import functools
import math

import jax
import jax.numpy as jnp
from jax import lax
from jax.experimental import pallas as pl
from jax.experimental.pallas import tpu as pltpu

HEAD_DIM = 128
DN_HEADS = 4
MOBA_HEADS = 4
DN_WIDTH = DN_HEADS * HEAD_DIM
ATT_WIDTH = MOBA_HEADS * HEAD_DIM
CONV_W = 4
CONV_DIM = 3 * DN_WIDTH
DN_CHUNK = 64
MOBA_BLOCK = 256
MOBA_TOPK = 3
N_EXPERTS = 32
TOP_K = 4
SWIGLU_LIMIT = 7.0
SWIGLU_ALPHA = 1.702
EPS = 1e-6

LANES = 128
SUBLANES = 8
MOE_ROWS = 512
VMEM_LIMIT = 56 * 1024 * 1024
NEG_BIG = -1e30

_HI = lax.Precision.HIGHEST
_F32 = jnp.float32
_BF16 = jnp.bfloat16


def _dot(a, b, precision=None):
    return jnp.dot(a, b, preferred_element_type=_F32, precision=precision)


def _dot_nt(a, b, precision=None):
    return lax.dot_general(a, b, (((1,), (1,)), ((), ())),
                           preferred_element_type=_F32, precision=precision)


def _dot_tn(a, b, precision=None):
    return lax.dot_general(a, b, (((0,), (0,)), ((), ())),
                           preferred_element_type=_F32, precision=precision)


def _rms(x, w):
    return x * lax.rsqrt(jnp.mean(x * x, axis=-1, keepdims=True) + EPS) * w


def _silu(x):
    return x * jax.nn.sigmoid(x)


def _params(*sem):
    return pltpu.CompilerParams(dimension_semantics=sem, vmem_limit_bytes=VMEM_LIMIT)


def _proj_kernel(x_ref, ln_ref, w_ref, qn_ref, kn_ref,
                 u_ref, z_ref, ab_ref, mq_ref, mk_ref, mv_ref, *, heads_out):
    xn = _rms(x_ref[...], ln_ref[...]).astype(_BF16)
    h = _dot(xn, w_ref[...])
    u_ref[...] = h[:, :CONV_DIM]
    z_ref[...] = h[:, CONV_DIM:CONV_DIM + DN_WIDTH]
    o = CONV_DIM + DN_WIDTH
    ab_ref[...] = h[:, o + 3 * ATT_WIDTH:]
    for hd in range(MOBA_HEADS):
        sl = slice(o + hd * HEAD_DIM, o + (hd + 1) * HEAD_DIM)
        q = _rms(h[:, sl], qn_ref[...])
        k = _rms(h[:, sl.start + ATT_WIDTH:sl.stop + ATT_WIDTH], kn_ref[...])
        v = h[:, sl.start + 2 * ATT_WIDTH:sl.stop + 2 * ATT_WIDTH]
        if heads_out:
            mq_ref[0, hd] = q
            mk_ref[0, hd] = k
            mv_ref[0, hd] = v
        else:
            hs = slice(hd * HEAD_DIM, (hd + 1) * HEAD_DIM)
            mq_ref[:, hs] = q
            mk_ref[:, hs] = k
            mv_ref[:, hs] = v


def _proj(x2d, ln_w, w_all, qn_w, kn_w, *, tm, seq=None):
    T, D = x2d.shape
    n_all = w_all.shape[1]
    grid = (T // tm,)
    row = lambda t: (t, 0)
    const = lambda t: (0, 0)
    if seq is not None:
        B, S = seq
        per = S // tm
        hshape = jax.ShapeDtypeStruct((B, MOBA_HEADS, S, HEAD_DIM), _F32)
        hspec = pl.BlockSpec((1, MOBA_HEADS, tm, HEAD_DIM), lambda t: (t // per, 0, t % per, 0))
    else:
        hshape = jax.ShapeDtypeStruct((T, ATT_WIDTH), _F32)
        hspec = pl.BlockSpec((tm, ATT_WIDTH), row)
    return pl.pallas_call(
        functools.partial(_proj_kernel, heads_out=seq is not None),
        grid=grid,
        in_specs=[pl.BlockSpec((tm, D), row), pl.BlockSpec((1, D), const),
                  pl.BlockSpec((D, n_all), const), pl.BlockSpec((1, HEAD_DIM), const),
                  pl.BlockSpec((1, HEAD_DIM), const)],
        out_specs=[pl.BlockSpec((tm, CONV_DIM), row), pl.BlockSpec((tm, DN_WIDTH), row),
                   pl.BlockSpec((tm, LANES), row), hspec, hspec, hspec],
        out_shape=[jax.ShapeDtypeStruct((T, CONV_DIM), _F32),
                   jax.ShapeDtypeStruct((T, DN_WIDTH), _F32),
                   jax.ShapeDtypeStruct((T, LANES), _F32), hshape, hshape, hshape],
        compiler_params=_params("parallel"),
        name="proj",
    )(x2d, ln_w, w_all, qn_w, kn_w)


def _inv_unit_lower(low):
    c = low.shape[0]
    rows = lax.broadcasted_iota(jnp.int32, (c, c), 0)
    cols = lax.broadcasted_iota(jnp.int32, (c, c), 1)
    x = jnp.where(rows == cols, 1.0, 0.0) - low
    m = _dot(low, low, _HI)
    n_fac = max(1, int(math.ceil(math.log2(c)))) - 1
    for i in range(n_fac):
        x = x + _dot(x, m, _HI)
        if i + 1 < n_fac:
            m = _dot(m, m, _HI)
    return x


def _gdn_kernel(u_ref, z_ref, ab_ref, conv0_ref, ssm0_ref, wconv_ref, alog_ref, dtb_ref, dnw_ref,
                od_ref, ssm_ref, up_scr, qkv_scr, gb_scr, s_scr, *, lt, valid_len):
    i = pl.program_id(1)
    c = DN_CHUNK

    @pl.when(i == 0)
    def _():
        s_scr[...] = ssm0_ref[0]
        up_scr[0:SUBLANES, :] = conv0_ref[0]

    up_scr[SUBLANES:SUBLANES + lt, :] = u_ref[0]
    w = wconv_ref[...]
    base = SUBLANES - (CONV_W - 1)
    y = up_scr[base:base + lt, :] * w[0:1, :]
    for t in range(1, CONV_W):
        y = y + up_scr[base + t:base + t + lt, :] * w[t:t + 1, :]
    up_scr[0:SUBLANES, :] = up_scr[lt:lt + SUBLANES, :]
    qkv = _silu(y)

    masked = valid_len < lt
    if masked:
        rowv = lax.broadcasted_iota(jnp.int32, (lt, 1), 0) < valid_len
        qkv = jnp.where(rowv, qkv, 0.0)
    for hd in range(DN_HEADS):
        sl = slice(hd * HEAD_DIM, (hd + 1) * HEAD_DIM)
        q = qkv[:, sl]
        k = qkv[:, DN_WIDTH + hd * HEAD_DIM:DN_WIDTH + (hd + 1) * HEAD_DIM]
        qkv_scr[:, sl] = q * lax.rsqrt(jnp.sum(q * q, axis=-1, keepdims=True) + EPS) * (HEAD_DIM ** -0.5)
        qkv_scr[:, DN_WIDTH + hd * HEAD_DIM:DN_WIDTH + (hd + 1) * HEAD_DIM] = (
            k * lax.rsqrt(jnp.sum(k * k, axis=-1, keepdims=True) + EPS))
    qkv_scr[:, 2 * DN_WIDTH:] = qkv[:, 2 * DN_WIDTH:]

    ab = ab_ref[0]
    lane = lax.broadcasted_iota(jnp.int32, (lt, LANES), 1)
    xg = ab + dtb_ref[...]
    softplus = jnp.maximum(xg, 0.0) + jnp.log(1.0 + jnp.exp(-jnp.abs(xg)))
    gb = jnp.where(lane < DN_HEADS, -jnp.exp(alog_ref[...]) * softplus, jax.nn.sigmoid(ab))
    if masked:
        gb = jnp.where(rowv, gb, 0.0)
    gb_scr[...] = gb

    rows = lax.broadcasted_iota(jnp.int32, (c, c), 0)
    cols = lax.broadcasted_iota(jnp.int32, (c, c), 1)
    causal = cols <= rows
    strict = cols < rows
    tril = jnp.where(causal, 1.0, 0.0)
    ones = jnp.ones((c, c), _F32)

    def chunk(ci, carry):
        r0 = pl.multiple_of(ci * c, c)
        gbc = gb_scr[pl.ds(r0, c), :]
        for hd in range(DN_HEADS):
            g_col = gbc[:, hd:hd + 1]
            beta = gbc[:, DN_HEADS + hd:DN_HEADS + hd + 1]
            gbr = jnp.broadcast_to(g_col, (c, c))
            cum_i = _dot(tril, gbr, _HI)
            cum_j = _dot(ones, jnp.where(rows <= cols, gbr, 0.0), _HI)
            gcum = cum_i[:, 0:1]
            decay = jnp.exp(jnp.where(causal, cum_i - cum_j, -jnp.inf))
            q = qkv_scr[pl.ds(r0, c), hd * HEAD_DIM:(hd + 1) * HEAD_DIM]
            k = qkv_scr[pl.ds(r0, c), DN_WIDTH + hd * HEAD_DIM:DN_WIDTH + (hd + 1) * HEAD_DIM]
            v = qkv_scr[pl.ds(r0, c), 2 * DN_WIDTH + hd * HEAD_DIM:2 * DN_WIDTH + (hd + 1) * HEAD_DIM]
            kb = k * beta
            k16 = k.astype(_BF16)
            low = jnp.where(strict, _dot_nt(kb.astype(_BF16), k16) * decay, 0.0)
            tinv = _inv_unit_lower(low)
            eg = jnp.exp(gcum)
            uw = _dot(tinv, jnp.concatenate([v * beta, kb * eg], axis=1), _HI)
            u = uw[:, :HEAD_DIM]
            wmat = uw[:, HEAD_DIM:]
            attn = _dot_nt(q.astype(_BF16), k16) * decay
            s = s_scr[hd]
            s16 = s.astype(_BF16)
            vnew = u - _dot(wmat.astype(_BF16), s16)
            vnew16 = vnew.astype(_BF16)
            o = _dot((q * eg).astype(_BF16), s16) + _dot(attn.astype(_BF16), vnew16)
            glast = gcum[c - 1:c, :]
            kdec = k * jnp.exp(glast - gcum)
            s_scr[hd] = s * jnp.exp(glast) + _dot_tn(kdec.astype(_BF16), vnew16)
            zz = z_ref[0, pl.ds(r0, c), hd * HEAD_DIM:(hd + 1) * HEAD_DIM]
            od_ref[0, pl.ds(r0, c), hd * HEAD_DIM:(hd + 1) * HEAD_DIM] = (
                _rms(o, dnw_ref[...]) * _silu(zz)).astype(_BF16)
        return carry

    lax.fori_loop(0, lt // c, chunk, 0)
    ssm_ref[0] = s_scr[...]


def _gdn(u, z, ab, conv0, ssm0, w_conv, alog_row, dtb_row, dn_w, *, lt, valid_len):
    B, L, _ = u.shape
    grid = (B, L // lt)
    tile = lambda b, i: (b, i, 0)
    perb3 = lambda b, i: (b, 0, 0)
    const = lambda b, i: (0, 0)
    return pl.pallas_call(
        functools.partial(_gdn_kernel, lt=lt, valid_len=valid_len),
        grid=grid,
        in_specs=[pl.BlockSpec((1, lt, CONV_DIM), tile), pl.BlockSpec((1, lt, DN_WIDTH), tile),
                  pl.BlockSpec((1, lt, LANES), tile), pl.BlockSpec((1, SUBLANES, CONV_DIM), perb3),
                  pl.BlockSpec((1, DN_HEADS, HEAD_DIM, HEAD_DIM), lambda b, i: (b, 0, 0, 0)),
                  pl.BlockSpec((CONV_W, CONV_DIM), const), pl.BlockSpec((1, LANES), const),
                  pl.BlockSpec((1, LANES), const), pl.BlockSpec((1, HEAD_DIM), const)],
        out_specs=[pl.BlockSpec((1, lt, DN_WIDTH), tile),
                   pl.BlockSpec((1, DN_HEADS, HEAD_DIM, HEAD_DIM), lambda b, i: (b, 0, 0, 0))],
        out_shape=[jax.ShapeDtypeStruct((B, L, DN_WIDTH), _BF16),
                   jax.ShapeDtypeStruct((B, DN_HEADS, HEAD_DIM, HEAD_DIM), _F32)],
        scratch_shapes=[pltpu.VMEM((lt + 2 * SUBLANES, CONV_DIM), _F32),
                        pltpu.VMEM((lt, CONV_DIM), _F32),
                        pltpu.VMEM((lt, LANES), _F32),
                        pltpu.VMEM((DN_HEADS, HEAD_DIM, HEAD_DIM), _F32)],
        compiler_params=_params("parallel", "arbitrary"),
        name="gdn",
    )(u, z, ab, conv0, ssm0, w_conv, alog_row, dtb_row, dn_w)


def _topk_rows(g, n_sel):
    r = g.shape[0]
    row = lax.broadcasted_iota(jnp.int32, g.shape, 0)
    sel = jnp.zeros(g.shape, _F32)
    for _ in range(n_sel):
        m = jnp.max(g, axis=0, keepdims=True)
        idx = jnp.min(jnp.where(g == m, row, r), axis=0, keepdims=True)
        pick = row == idx
        sel = jnp.where(pick, 1.0, sel)
        g = jnp.where(pick, -jnp.inf, g)
    return sel


def _moba_prompt_kernel(q_ref, k_ref, v_ref, o_ref, kb_scr, vt_scr, kmean_scr, sel_scr, *, nblk):
    qt = pl.program_id(2)
    blk = MOBA_BLOCK
    scale = HEAD_DIM ** -0.5

    @pl.when(qt == 0)
    def _():
        kmean_scr[...] = jnp.zeros(kmean_scr.shape, _F32)
        for j in range(nblk):
            kj = k_ref[0, 0, j * blk:(j + 1) * blk, :]
            kb_scr[j] = kj.astype(_BF16)
            kmean_scr[j:j + 1, :] = jnp.mean(kj, axis=0, keepdims=True)
            vt_scr[j] = v_ref[0, 0, j * blk:(j + 1) * blk, :].T.astype(_BF16)

    q = q_ref[0, 0]
    gate = _dot_nt(kmean_scr[...], q, _HI)
    brow = lax.broadcasted_iota(jnp.int32, gate.shape, 0)
    valid = brow < qt
    sel = _topk_rows(jnp.where(valid, gate, -jnp.inf), min(MOBA_TOPK, nblk))
    sel_scr[...] = jnp.where(valid, sel, 0.0)

    q16 = q.astype(_BF16)
    kpos = lax.broadcasted_iota(jnp.int32, (blk, blk), 0)
    qpos = lax.broadcasted_iota(jnp.int32, (blk, blk), 1)
    s = _dot_nt(kb_scr[qt], q16) * scale
    s = jnp.where(kpos <= qpos, s, NEG_BIG)
    m = jnp.max(s, axis=0, keepdims=True)
    p = jnp.exp(s - m)
    l = jnp.sum(p, axis=0, keepdims=True)
    acc = _dot(vt_scr[qt], p.astype(_BF16))

    def body(j, carry):
        m, l, acc = carry
        s = _dot_nt(kb_scr[j], q16) * scale
        s = jnp.where(sel_scr[pl.ds(j, 1), :] > 0.0, s, NEG_BIG)
        m_new = jnp.maximum(m, jnp.max(s, axis=0, keepdims=True))
        alpha = jnp.exp(m - m_new)
        p = jnp.exp(s - m_new)
        l = alpha * l + jnp.sum(p, axis=0, keepdims=True)
        acc = alpha * acc + _dot(vt_scr[j], p.astype(_BF16))
        return m_new, l, acc

    m, l, acc = lax.fori_loop(0, qt, body, (m, l, acc))
    o_ref[0] = (acc / l).T.astype(_BF16)


def _moba_prompt(q, k, v):
    B, H, S, dh = q.shape
    nblk = S // MOBA_BLOCK
    assert S % MOBA_BLOCK == 0 and nblk <= LANES
    full = lambda b, h, t: (b, h, 0, 0)
    return pl.pallas_call(
        functools.partial(_moba_prompt_kernel, nblk=nblk),
        grid=(B, H, nblk),
        in_specs=[pl.BlockSpec((1, 1, MOBA_BLOCK, dh), lambda b, h, t: (b, h, t, 0)),
                  pl.BlockSpec((1, 1, S, dh), full), pl.BlockSpec((1, 1, S, dh), full)],
        out_specs=pl.BlockSpec((1, MOBA_BLOCK, dh), lambda b, h, t: (b, t, h)),
        out_shape=jax.ShapeDtypeStruct((B, S, H * dh), _BF16),
        scratch_shapes=[pltpu.VMEM((nblk, MOBA_BLOCK, dh), _BF16),
                        pltpu.VMEM((nblk, dh, MOBA_BLOCK), _BF16),
                        pltpu.VMEM((LANES, dh), _F32),
                        pltpu.VMEM((LANES, MOBA_BLOCK), _F32)],
        compiler_params=_params("parallel", "parallel", "arbitrary"),
        name="moba_prompt",
    )(q, k, v)


def _page_sum_kernel(c_ref, o_ref):
    o_ref[...] = jnp.sum(c_ref[...], axis=1)


def _page_sums(cache2d, rows_per_step):
    n, ps, dh = cache2d.shape
    return pl.pallas_call(
        _page_sum_kernel,
        grid=(n // rows_per_step,),
        in_specs=[pl.BlockSpec((rows_per_step, ps, dh), lambda i: (i, 0, 0))],
        out_specs=pl.BlockSpec((rows_per_step, dh), lambda i: (i, 0)),
        out_shape=jax.ShapeDtypeStruct((n, dh), _F32),
        compiler_params=_params("parallel"),
        name="page_sum",
    )(cache2d)


def _moba_sel_kernel(pt_ref, psum_ref, q_ref, sel_ref, km_scr, *, n_pages, ppb, n_blk, n_sel, rows_per_blk):
    b = pl.program_id(0)
    km_scr[...] = jnp.zeros(km_scr.shape, _F32)

    def fill(j, carry):
        acc = psum_ref[pl.ds(pt_ref[b * n_pages + j * ppb], 1), :]
        for r in range(1, ppb):
            acc = acc + psum_ref[pl.ds(pt_ref[b * n_pages + j * ppb + r], 1), :]
        km_scr[pl.ds(j, 1), :] = acc * (1.0 / rows_per_blk)
        return carry

    lax.fori_loop(0, n_blk, fill, 0)
    lane = lax.broadcasted_iota(jnp.int32, (SUBLANES, LANES), 1)
    for hd in range(MOBA_HEADS):
        g = _dot_nt(q_ref[0, hd], km_scr[:, hd * HEAD_DIM:(hd + 1) * HEAD_DIM], _HI)
        g = jnp.where(lane < n_blk, g, -jnp.inf)
        out = jnp.zeros((SUBLANES, LANES), jnp.int32)
        for r in range(n_sel):
            m = jnp.max(g, axis=1, keepdims=True)
            idx = jnp.min(jnp.where(g == m, lane, LANES), axis=1, keepdims=True)
            out = jnp.where(lane == r, idx, out)
            g = jnp.where(lane == idx, -jnp.inf, g)
        sel_ref[0, hd] = out


def _moba_sel(page_table, psum, q8, *, ppb, n_sel, rows_per_blk):
    DB, n_pages = page_table.shape
    n_blk = n_pages // ppb
    assert n_blk <= LANES
    n_phys, width = psum.shape
    return pl.pallas_call(
        functools.partial(_moba_sel_kernel, n_pages=n_pages, ppb=ppb, n_blk=n_blk, n_sel=n_sel,
                          rows_per_blk=rows_per_blk),
        grid_spec=pltpu.PrefetchScalarGridSpec(
            num_scalar_prefetch=1,
            grid=(DB,),
            in_specs=[pl.BlockSpec((n_phys, width), lambda b, pt: (0, 0)),
                      pl.BlockSpec((1, MOBA_HEADS, SUBLANES, HEAD_DIM), lambda b, pt: (b, 0, 0, 0))],
            out_specs=pl.BlockSpec((1, MOBA_HEADS, SUBLANES, LANES), lambda b, pt: (b, 0, 0, 0)),
            scratch_shapes=[pltpu.VMEM((LANES, width), _F32)]),
        out_shape=jax.ShapeDtypeStruct((DB, MOBA_HEADS, SUBLANES, LANES), jnp.int32),
        compiler_params=_params("arbitrary"),
        name="moba_sel",
    )(page_table.reshape(-1), psum, q8)


def _moba_sample_kernel(phys_ref, q_ref, kn_ref, vn_ref, *refs, n_pg, n_new):
    k_refs = refs[:n_pg]
    v_refs = refs[n_pg:2 * n_pg]
    o_ref = refs[2 * n_pg]
    t = pl.program_id(0) % n_new
    scale = HEAD_DIM ** -0.5
    q = q_ref[0, 0]
    q16 = q.astype(_BF16)
    kn = kn_ref[0, 0]
    vn = vn_ref[0, 0]
    rowq = lax.broadcasted_iota(jnp.int32, (SUBLANES, 1), 0)
    s_past = [_dot_nt(q16, kr[0, 0].astype(_BF16)) * scale for kr in k_refs]
    s_own = []
    for c in range(n_new):
        sc = jnp.sum(q * kn[c:c + 1, :], axis=-1, keepdims=True) * scale
        s_own.append(jnp.where(rowq >= c, sc, NEG_BIG))
    m = s_own[0]
    for sc in s_own[1:]:
        m = jnp.maximum(m, sc)
    for sp in s_past:
        m = jnp.maximum(m, jnp.max(sp, axis=-1, keepdims=True))
    l = jnp.zeros((SUBLANES, 1), _F32)
    acc = jnp.zeros((SUBLANES, HEAD_DIM), _F32)
    for sp, vr in zip(s_past, v_refs):
        p = jnp.exp(sp - m)
        l = l + jnp.sum(p, axis=-1, keepdims=True)
        acc = acc + _dot(p.astype(_BF16), vr[0, 0].astype(_BF16))
    for c, sc in enumerate(s_own):
        p = jnp.exp(sc - m)
        l = l + p
        acc = acc + p * vn[c:c + 1, :]
    o = acc / l
    o_ref[0] = jnp.sum(jnp.where(rowq == t, o, 0.0), axis=0, keepdims=True)


def _moba_sample(phys, q8, kn8, vn8, cache_k, cache_v, *, n_new):
    DB, H, _, dh = q8.shape
    ps = cache_k.shape[2]
    n_pg = phys.shape[0] // (DB * H * n_new)
    steps = DB * H * n_new
    small = pl.BlockSpec((1, 1, SUBLANES, dh), lambda s, ph: (s // (H * n_new), (s // n_new) % H, 0, 0))

    def page_spec(r):
        return pl.BlockSpec((1, 1, ps, dh), lambda s, ph: (ph[s * n_pg + r], (s // n_new) % H, 0, 0))

    return pl.pallas_call(
        functools.partial(_moba_sample_kernel, n_pg=n_pg, n_new=n_new),
        grid_spec=pltpu.PrefetchScalarGridSpec(
            num_scalar_prefetch=1,
            grid=(steps,),
            in_specs=[small, small, small] + [page_spec(r) for r in range(n_pg)] * 2,
            out_specs=pl.BlockSpec((1, 1, dh), lambda s, ph: (s, 0, 0))),
        out_shape=jax.ShapeDtypeStruct((steps, 1, dh), _F32),
        compiler_params=_params("arbitrary"),
        name="moba_sample",
    )(phys, q8, kn8, vn8, *([cache_k] * n_pg), *([cache_v] * n_pg))


def _post_kernel(x_ref, od_ref, om_ref, wo1_ref, wo2_ref, ln_ref, wrh_ref, wrl_ref, br_ref, cin_ref,
                 x1_ref, xn_ref, meta_ref, cnt_ref, carry_scr, *, tm):
    i = pl.program_id(0)

    @pl.when(i == 0)
    def _():
        carry_scr[...] = cin_ref[...]

    x1 = x_ref[...] + _dot(od_ref[...], wo1_ref[...]) + _dot(om_ref[...], wo2_ref[...])
    x1_ref[...] = x1
    xn = _rms(x1, ln_ref[...])
    xh = xn.astype(_BF16)
    xn_ref[...] = xh
    xl = (xn - xh.astype(_F32)).astype(_BF16)
    logits = (_dot(xh, wrh_ref[...]) + _dot(xl, wrh_ref[...]) + _dot(xh, wrl_ref[...])) + br_ref[...]

    lane = lax.broadcasted_iota(jnp.int32, (tm, LANES), 1)
    g = jnp.where(lane < N_EXPERTS, logits, -jnp.inf)
    vals, picks = [], []
    for _ in range(TOP_K):
        m = jnp.max(g, axis=-1, keepdims=True)
        idx = jnp.min(jnp.where(g == m, lane, LANES), axis=-1, keepdims=True)
        pick = lane == idx
        vals.append(m)
        picks.append((idx, pick))
        g = jnp.where(pick, -jnp.inf, g)
    es = [jnp.exp(v - vals[0]) for v in vals]
    den = es[0]
    for e in es[1:]:
        den = den + e

    rows = lax.broadcasted_iota(jnp.int32, (tm, tm), 0)
    cols = lax.broadcasted_iota(jnp.int32, (tm, tm), 1)
    before = jnp.where(cols < rows, 1.0, 0.0).astype(_BF16)
    base = carry_scr[...]
    meta = jnp.zeros((tm, LANES), _F32)
    for k, (idx, pick) in enumerate(picks):
        onehot = jnp.where(pick, 1.0, 0.0)
        pref = _dot(before, onehot.astype(_BF16)) + base
        rank = jnp.sum(onehot * pref, axis=-1, keepdims=True)
        meta = jnp.where(lane == k, idx.astype(_F32), meta)
        meta = jnp.where(lane == TOP_K + k, es[k] / den, meta)
        meta = jnp.where(lane == 2 * TOP_K + k, rank, meta)
        base = base + jnp.sum(onehot, axis=0, keepdims=True)
    carry_scr[...] = base
    cnt_ref[...] = base
    meta_ref[...] = meta


def _post(x2d, od, om, wo1, wo2, ln_w, wr_hi, wr_lo, b_r, carry_in, *, tm):
    T, D = x2d.shape
    row = lambda t: (t, 0)
    const = lambda t: (0, 0)
    return pl.pallas_call(
        functools.partial(_post_kernel, tm=tm),
        grid=(T // tm,),
        in_specs=[pl.BlockSpec((tm, D), row), pl.BlockSpec((tm, DN_WIDTH), row),
                  pl.BlockSpec((tm, ATT_WIDTH), row), pl.BlockSpec((DN_WIDTH, D), const),
                  pl.BlockSpec((ATT_WIDTH, D), const), pl.BlockSpec((1, D), const),
                  pl.BlockSpec((D, LANES), const), pl.BlockSpec((D, LANES), const),
                  pl.BlockSpec((1, LANES), const), pl.BlockSpec((1, LANES), const)],
        out_specs=[pl.BlockSpec((tm, D), row), pl.BlockSpec((tm, D), row),
                   pl.BlockSpec((tm, LANES), row), pl.BlockSpec((1, LANES), const)],
        out_shape=[jax.ShapeDtypeStruct((T, D), _F32), jax.ShapeDtypeStruct((T, D), _BF16),
                   jax.ShapeDtypeStruct((T, LANES), _F32), jax.ShapeDtypeStruct((1, LANES), _F32)],
        scratch_shapes=[pltpu.VMEM((1, LANES), _F32)],
        compiler_params=_params("arbitrary"),
        name="post",
    )(x2d, od, om, wo1, wo2, ln_w, wr_hi, wr_lo, b_r, carry_in)


def _moe_kernel(be_ref, nu_ref, x_ref, wg_ref, bg_ref, wu_ref, bu_ref, wd_ref, bd_ref, o_ref,
                wg_scr, wu_scr, wd_scr):
    i = pl.program_id(0)

    @pl.when(i < nu_ref[0])
    def _():
        prev = be_ref[jnp.maximum(i - 1, 0)]

        @pl.when((i == 0) | (be_ref[i] != prev))
        def _():
            wg_scr[...] = wg_ref[0].astype(_BF16)
            wu_scr[...] = wu_ref[0].astype(_BF16)
            wd_scr[...] = wd_ref[0].astype(_BF16)

        x = x_ref[...]
        gt = jnp.minimum(_dot(x, wg_scr[...]) + bg_ref[0], SWIGLU_LIMIT)
        up = jnp.clip(_dot(x, wu_scr[...]) + bu_ref[0], -SWIGLU_LIMIT, SWIGLU_LIMIT)
        act = ((up + 1.0) * (gt * jax.nn.sigmoid(SWIGLU_ALPHA * gt))).astype(_BF16)
        o_ref[...] = _dot(act, wd_scr[...]) + bd_ref[0]


def _moe_ffn(block_e, n_used, xs, w_gate, b_gate, w_up, b_up, w_down, b_down):
    n_slots, D = xs.shape
    E, _, F = w_gate.shape
    n_blocks = n_slots // MOE_ROWS
    blk = lambda i, be, nu: (jnp.minimum(i, nu[0] - 1), 0)
    wsel = lambda i, be, nu: (be[jnp.minimum(i, nu[0] - 1)], 0, 0)
    return pl.pallas_call(
        _moe_kernel,
        grid_spec=pltpu.PrefetchScalarGridSpec(
            num_scalar_prefetch=2,
            grid=(n_blocks,),
            in_specs=[pl.BlockSpec((MOE_ROWS, D), blk),
                      pl.BlockSpec((1, D, F), wsel), pl.BlockSpec((1, 1, F), wsel),
                      pl.BlockSpec((1, D, F), wsel), pl.BlockSpec((1, 1, F), wsel),
                      pl.BlockSpec((1, F, D), wsel), pl.BlockSpec((1, 1, D), wsel)],
            out_specs=pl.BlockSpec((MOE_ROWS, D), blk),
            scratch_shapes=[pltpu.VMEM((D, F), _BF16), pltpu.VMEM((D, F), _BF16),
                            pltpu.VMEM((F, D), _BF16)]),
        out_shape=jax.ShapeDtypeStruct((n_slots, D), _F32),
        compiler_params=_params("arbitrary"),
        name="moe_ffn",
    )(block_e, n_used, xs, w_gate, b_gate.reshape(E, 1, F), w_up, b_up.reshape(E, 1, F),
      w_down, b_down.reshape(E, 1, D))


def _pad_lanes(v, fill=0.0):
    v = v.reshape(1, -1).astype(_F32)
    return jnp.pad(v, ((0, 0), (0, LANES - v.shape[1])), constant_values=fill)


def _prep_w_in(w_in):
    off_a = CONV_DIM + DN_WIDTH
    off_mq = off_a + 2 * DN_HEADS
    ab = jnp.pad(w_in[:, off_a:off_mq], ((0, 0), (0, LANES - 2 * DN_HEADS)))
    return jnp.concatenate([w_in[:, :off_a], w_in[:, off_mq:], ab], axis=1).astype(_BF16)


def _moe(xn_all, x1_all, meta, counts, w_gate, b_gate, w_up, b_up, w_down, b_down):
    T = xn_all.shape[0]
    idx = meta[:, 0:TOP_K].astype(jnp.int32)
    gates = meta[:, TOP_K:2 * TOP_K]
    rank = meta[:, 2 * TOP_K:3 * TOP_K].astype(jnp.int32)
    cnt = counts[0, :N_EXPERTS].astype(jnp.int32)
    padded = (cnt + MOE_ROWS - 1) // MOE_ROWS * MOE_ROWS
    pends = jnp.cumsum(padded)
    pstarts = pends - padded
    dest = pstarts[idx] + rank
    n_blocks = -(-T * TOP_K // MOE_ROWS) + N_EXPERTS
    block_e = jnp.minimum(
        jnp.searchsorted(pends, jnp.arange(n_blocks, dtype=jnp.int32) * MOE_ROWS, side="right"),
        N_EXPERTS - 1).astype(jnp.int32)
    n_used = (pends[-1:] // MOE_ROWS).astype(jnp.int32)
    tok = jnp.broadcast_to(jnp.arange(T, dtype=jnp.int32)[:, None], (T, TOP_K))
    slot_tok = jnp.zeros((n_blocks * MOE_ROWS,), jnp.int32).at[dest.reshape(-1)].set(tok.reshape(-1))
    xs = jnp.take(xn_all, slot_tok, axis=0)
    ys = _moe_ffn(block_e, n_used, xs, w_gate, b_gate, w_up, b_up, w_down, b_down)
    f = jnp.sum(jnp.take(ys, dest, axis=0) * gates[:, :, None], axis=1)
    return x1_all + f


def _layer(xp, xs_, cache_k, cache_v, conv_s, ssm_s, page_table, lw):
    (ln1_w, w_in, w_conv, a_log, dt_bias, dn_norm_w, q_norm_w, k_norm_w, w_o, ln2_w, w_router,
     b_router, w_gate, b_gate, w_up, b_up, w_down, b_down) = lw
    B, S, D = xp.shape
    DB, L, _ = xs_.shape
    n_phys, H, PS, dh = cache_k.shape
    n_pages = page_table.shape[1]
    ppb = MOBA_BLOCK // PS
    assert (n_pages * PS) % MOBA_BLOCK == 0, "paged past must end on a MoBA block boundary"
    assert L <= SUBLANES and L >= CONV_W - 1
    cur = n_pages // ppb
    n_sel = min(MOBA_TOPK, cur)
    assert n_sel > 0

    w_all = _prep_w_in(w_in)
    ln1 = ln1_w.reshape(1, D)
    qn = q_norm_w.reshape(1, dh)
    kn = k_norm_w.reshape(1, dh)
    alog_row = _pad_lanes(a_log)
    dtb_row = _pad_lanes(dt_bias)
    dnw = dn_norm_w.reshape(1, dh)

    u_p, z_p, ab_p, mq_p, mk_p, mv_p = _proj(xp.reshape(B * S, D), ln1, w_all, qn, kn, tm=512, seq=(B, S))
    od_p, ssm_p = _gdn(u_p.reshape(B, S, CONV_DIM), z_p.reshape(B, S, DN_WIDTH), ab_p.reshape(B, S, LANES),
                       jnp.zeros((B, SUBLANES, CONV_DIM), _F32), jnp.zeros((B, DN_HEADS, dh, dh), _F32),
                       w_conv, alog_row, dtb_row, dnw, lt=256, valid_len=256)
    om_p = _moba_prompt(mq_p, mk_p, mv_p)
    conv_p = u_p.reshape(B, S, CONV_DIM)[:, S - (CONV_W - 1):]

    Ts = DB * L
    u_s, z_s, ab_s, mq_s, mk_s, mv_s = _proj(xs_.reshape(Ts, D), ln1, w_all, qn, kn, tm=Ts)
    padl = lambda a: jnp.pad(a.reshape(DB, L, -1), ((0, 0), (0, DN_CHUNK - L), (0, 0)))
    conv0 = jnp.pad(conv_s, ((0, 0), (SUBLANES - (CONV_W - 1), 0), (0, 0)))
    od_s, ssm_s_new = _gdn(padl(u_s), padl(z_s), padl(ab_s), conv0, ssm_s, w_conv, alog_row, dtb_row, dnw,
                           lt=DN_CHUNK, valid_len=L)
    od_s = od_s[:, :L].reshape(Ts, DN_WIDTH)
    conv_s_new = jnp.concatenate([conv_s, u_s.reshape(DB, L, CONV_DIM)], axis=1)[:, L:]

    heads = lambda a: a.reshape(DB, L, H, dh).transpose(0, 2, 1, 3)
    pad8 = lambda a: jnp.pad(a, ((0, 0), (0, 0), (0, SUBLANES - L), (0, 0)))
    q8, k8, v8 = pad8(heads(mq_s)), pad8(heads(mk_s)), pad8(heads(mv_s))
    psum = _page_sums(cache_k.reshape(n_phys * H, PS, dh), 64).reshape(n_phys, H * dh)
    sel = _moba_sel(page_table, psum, q8, ppb=ppb, n_sel=n_sel, rows_per_blk=MOBA_BLOCK)
    sel = sel[:, :, :L, :n_sel]
    logical = sel[..., None] * ppb + jnp.arange(ppb, dtype=jnp.int32)
    phys = page_table[jnp.arange(DB)[:, None, None, None, None], logical]
    om_s = _moba_sample(phys.reshape(-1).astype(jnp.int32), q8, k8, v8, cache_k, cache_v, n_new=L)
    om_s = om_s.reshape(DB, H, L, dh).transpose(0, 2, 1, 3).reshape(Ts, H * dh).astype(_BF16)

    wo = w_o.astype(_BF16)
    wo1, wo2 = wo[:DN_WIDTH], wo[DN_WIDTH:]
    ln2 = ln2_w.reshape(1, D)
    wr = jnp.pad(w_router, ((0, 0), (0, LANES - N_EXPERTS)))
    wr_hi = wr.astype(_BF16)
    wr_lo = (wr - wr_hi.astype(_F32)).astype(_BF16)
    br = _pad_lanes(b_router)
    x1_p, xn_p, meta_p, cnt_p = _post(xp.reshape(B * S, D), od_p.reshape(B * S, DN_WIDTH),
                                      om_p.reshape(B * S, ATT_WIDTH), wo1, wo2, ln2, wr_hi, wr_lo, br,
                                      jnp.zeros((1, LANES), _F32), tm=256)
    x1_s, xn_s, meta_s, cnt = _post(xs_.reshape(Ts, D), od_s, om_s, wo1, wo2, ln2, wr_hi, wr_lo, br,
                                    cnt_p, tm=Ts)
    y_all = _moe(jnp.concatenate([xn_p, xn_s]), jnp.concatenate([x1_p, x1_s]),
                 jnp.concatenate([meta_p, meta_s]), cnt, w_gate, b_gate, w_up, b_up, w_down, b_down)
    y_p = y_all[:B * S].reshape(B, S, D)
    y_s = y_all[B * S:].reshape(DB, L, D)
    return (y_p, y_s, mk_p, mv_p, conv_p, ssm_p, heads(mk_s), heads(mv_s), conv_s_new, ssm_s_new)


def kernel(x_prompt, x_sample, cache_k, cache_v, state_conv, state_ssm, page_table, ln1_w, w_in, w_conv,
           a_log, dt_bias, dn_norm_w, q_norm_w, k_norm_w, w_o, ln2_w, w_router, b_router, w_gate, b_gate,
           w_up, b_up, w_down, b_down):
    weights = (ln1_w, w_in, w_conv, a_log, dt_bias, dn_norm_w, q_norm_w, k_norm_w, w_o, ln2_w, w_router,
               b_router, w_gate, b_gate, w_up, b_up, w_down, b_down)
    depth = w_in.shape[0]
    yp, ys = x_prompt, x_sample
    outs = [[] for _ in range(8)]
    for l in range(depth):
        res = _layer(yp, ys, cache_k[l], cache_v[l], state_conv[l], state_ssm[l], page_table,
                     tuple(w[l] for w in weights))
        yp, ys = res[0], res[1]
        for acc, r in zip(outs, res[2:]):
            acc.append(r)
    return (yp, ys) + tuple(jnp.stack(o) for o in outs)
```

```python
import functools
import math

import jax
import jax.numpy as jnp
from jax import lax
from jax.experimental import pallas as pl
from jax.experimental.pallas import tpu as pltpu

HEAD_DIM = 128
DN_HEADS = 4
MOBA_HEADS = 4
DN_WIDTH = DN_HEADS * HEAD_DIM
ATT_WIDTH = MOBA_HEADS * HEAD_DIM
CONV_W = 4
CONV_DIM = 3 * DN_WIDTH
DN_CHUNK = 64
MOBA_BLOCK = 256
MOBA_TOPK = 3
N_EXPERTS = 32
TOP_K = 4
SWIGLU_LIMIT = 7.0
SWIGLU_ALPHA = 1.702
EPS = 1e-6

LANES = 128
SUBLANES = 8
VMEM_LIMIT = 56 * 1024 * 1024
NEG_BIG = -1e30

PROJ_ROWS = 512
GDN_ROWS = 256
TOK_TILE = 256
MOE_ROWS = 512
PAGE_SUM_ROWS = 64

_HI = lax.Precision.HIGHEST
_F32 = jnp.float32
_BF16 = jnp.bfloat16


def _dot(a, b, precision=None):
    return jnp.dot(a, b, preferred_element_type=_F32, precision=precision)


def _dot_nt(a, b, precision=None):
    return lax.dot_general(a, b, (((1,), (1,)), ((), ())),
                           preferred_element_type=_F32, precision=precision)


def _dot_tn(a, b, precision=None):
    return lax.dot_general(a, b, (((0,), (0,)), ((), ())),
                           preferred_element_type=_F32, precision=precision)


def _split2(x):
    hi = x.astype(_BF16)
    return hi, (x - hi.astype(_F32)).astype(_BF16)


def _split3(x):
    hi = x.astype(_BF16)
    r = x - hi.astype(_F32)
    mid = r.astype(_BF16)
    return hi, mid, (r - mid.astype(_F32)).astype(_BF16)


def _dot3(a, b, dot=_dot):
    return dot(a[0], b[0]) + dot(a[1], b[0]) + dot(a[0], b[1])


def _rms(x, w):
    return x * lax.rsqrt(jnp.mean(x * x, axis=-1, keepdims=True) + EPS) * w


def _store_token_tiles(ref, x):
    n = x.shape[0]
    for j in range(SUBLANES):
        ref[pl.ds(j, n, stride=SUBLANES), :] = x[:, j * LANES:(j + 1) * LANES]


def _load_token_tiles(ref, n, j):
    return ref[pl.ds(j, n, stride=SUBLANES), :]


def _silu(x):
    return x * jax.nn.sigmoid(x)


def _params(*sem):
    return pltpu.CompilerParams(dimension_semantics=sem, vmem_limit_bytes=VMEM_LIMIT)


def _proj_kernel(x_ref, ln_ref, w_ref, qn_ref, kn_ref,
                 u_ref, z_ref, ab_ref, mq_ref, mk_ref, mv_ref, *, heads_out):
    xn = _rms(x_ref[...], ln_ref[...]).astype(_BF16)
    h = _dot(xn, w_ref[...])
    u_ref[...] = h[:, :CONV_DIM]
    z_ref[...] = h[:, CONV_DIM:CONV_DIM + DN_WIDTH]
    o = CONV_DIM + DN_WIDTH
    ab_ref[...] = h[:, o + 3 * ATT_WIDTH:]
    for hd in range(MOBA_HEADS):
        sl = slice(o + hd * HEAD_DIM, o + (hd + 1) * HEAD_DIM)
        q = _rms(h[:, sl], qn_ref[...])
        k = _rms(h[:, sl.start + ATT_WIDTH:sl.stop + ATT_WIDTH], kn_ref[...])
        v = h[:, sl.start + 2 * ATT_WIDTH:sl.stop + 2 * ATT_WIDTH]
        if heads_out:
            mq_ref[0, hd] = q
            mk_ref[0, hd] = k
            mv_ref[0, hd] = v
        else:
            hs = slice(hd * HEAD_DIM, (hd + 1) * HEAD_DIM)
            mq_ref[:, hs] = q
            mk_ref[:, hs] = k
            mv_ref[:, hs] = v


def _proj(x2d, ln_w, w_all, qn_w, kn_w, *, tm, seq=None):
    T, D = x2d.shape
    n_all = w_all.shape[1]
    grid = (T // tm,)
    row = lambda t: (t, 0)
    const = lambda t: (0, 0)
    if seq is not None:
        B, S = seq
        per = S // tm
        hshape = jax.ShapeDtypeStruct((B, MOBA_HEADS, S, HEAD_DIM), _F32)
        hspec = pl.BlockSpec((1, MOBA_HEADS, tm, HEAD_DIM), lambda t: (t // per, 0, t % per, 0))
    else:
        hshape = jax.ShapeDtypeStruct((T, ATT_WIDTH), _F32)
        hspec = pl.BlockSpec((tm, ATT_WIDTH), row)
    return pl.pallas_call(
        functools.partial(_proj_kernel, heads_out=seq is not None),
        grid=grid,
        in_specs=[pl.BlockSpec((tm, D), row), pl.BlockSpec((1, D), const),
                  pl.BlockSpec((D, n_all), const), pl.BlockSpec((1, HEAD_DIM), const),
                  pl.BlockSpec((1, HEAD_DIM), const)],
        out_specs=[pl.BlockSpec((tm, CONV_DIM), row), pl.BlockSpec((tm, DN_WIDTH), row),
                   pl.BlockSpec((tm, LANES), row), hspec, hspec, hspec],
        out_shape=[jax.ShapeDtypeStruct((T, CONV_DIM), _F32),
                   jax.ShapeDtypeStruct((T, DN_WIDTH), _F32),
                   jax.ShapeDtypeStruct((T, LANES), _F32), hshape, hshape, hshape],
        compiler_params=_params("parallel"),
        name="proj",
    )(x2d, ln_w, w_all, qn_w, kn_w)


def _gdn_kernel(u_ref, z_ref, ab_ref, conv0_ref, ssm0_ref, wconv_ref, alog_ref, dtb_ref, dnw_ref,
                od_ref, ssm_ref, up_scr, s_scr, *, lt, valid_len):
    i = pl.program_id(1)
    c = DN_CHUNK
    nc = lt // c
    heads = range(DN_HEADS)

    @pl.when(i == 0)
    def _():
        s_scr[...] = ssm0_ref[0]
        up_scr[0:SUBLANES, :] = conv0_ref[0]

    up_scr[SUBLANES:SUBLANES + lt, :] = u_ref[0]
    w = wconv_ref[...]
    base = SUBLANES - (CONV_W - 1)
    y = up_scr[base:base + lt, :] * w[0:1, :]
    for t in range(1, CONV_W):
        y = y + up_scr[base + t:base + t + lt, :] * w[t:t + 1, :]
    up_scr[0:SUBLANES, :] = up_scr[lt:lt + SUBLANES, :]
    qkv = _silu(y)

    masked = valid_len < lt
    if masked:
        rowv = lax.broadcasted_iota(jnp.int32, (lt, 1), 0) < valid_len
        qkv = jnp.where(rowv, qkv, 0.0)

    ab = ab_ref[0]
    lane = lax.broadcasted_iota(jnp.int32, (lt, LANES), 1)
    xg = ab + dtb_ref[...]
    softplus = jnp.maximum(xg, 0.0) + jnp.log(1.0 + jnp.exp(-jnp.abs(xg)))
    gb = jnp.where(lane < DN_HEADS, -jnp.exp(alog_ref[...]) * softplus, jax.nn.sigmoid(ab))
    if masked:
        gb = jnp.where(rowv, gb, 0.0)

    rows = lax.broadcasted_iota(jnp.int32, (c, c), 0)
    cols = lax.broadcasted_iota(jnp.int32, (c, c), 1)
    causal = cols <= rows
    strict = cols < rows
    eye = jnp.where(rows == cols, 1.0, 0.0)
    tril16 = jnp.where(causal, 1.0, 0.0).astype(_BF16)
    prow = lax.broadcasted_iota(jnp.int32, (DN_HEADS * c, LANES), 0) // c
    plane = lax.broadcasted_iota(jnp.int32, (DN_HEADS * c, LANES), 1)
    pick16 = jnp.where(prow == plane, 1.0, 0.0).astype(_BF16)

    cum, cum_t = [], []
    for ci in range(nc):
        g3 = _split3(gb[ci * c:(ci + 1) * c, :])
        cm = _dot(tril16, g3[0]) + _dot(tril16, g3[1]) + _dot(tril16, g3[2])
        c3 = _split3(cm)
        cum.append(cm)
        cum_t.append(_dot_nt(pick16, c3[0]) + _dot_nt(pick16, c3[1]) + _dot_nt(pick16, c3[2]))
    probs = [(ci, hd) for ci in range(nc) for hd in heads]

    def rows_of(ci):
        return slice(ci * c, (ci + 1) * c)

    q_l, k_l, v_l = [], [], []
    for ci, hd in probs:
        q = qkv[rows_of(ci), hd * HEAD_DIM:(hd + 1) * HEAD_DIM]
        k = qkv[rows_of(ci), DN_WIDTH + hd * HEAD_DIM:DN_WIDTH + (hd + 1) * HEAD_DIM]
        q_l.append(q * lax.rsqrt(jnp.sum(q * q, axis=-1, keepdims=True) + EPS) * (HEAD_DIM ** -0.5))
        k_l.append(k * lax.rsqrt(jnp.sum(k * k, axis=-1, keepdims=True) + EPS))
        v_l.append(qkv[rows_of(ci), 2 * DN_WIDTH + hd * HEAD_DIM:2 * DN_WIDTH + (hd + 1) * HEAD_DIM])
    gcum_l = [cum[ci][:, hd:hd + 1] for ci, hd in probs]
    beta_l = [gb[rows_of(ci), DN_HEADS + hd:DN_HEADS + hd + 1] for ci, hd in probs]
    decay_l = [jnp.exp(jnp.where(causal, g - cum_t[ci][hd * c:(hd + 1) * c, :], -jnp.inf))
               for (ci, hd), g in zip(probs, gcum_l)]
    k16_l = [k.astype(_BF16) for k in k_l]
    kb_l = [k * b for k, b in zip(k_l, beta_l)]
    low_l = [jnp.where(strict, _dot_nt(kb.astype(_BF16), k16) * d, 0.0)
             for kb, k16, d in zip(kb_l, k16_l, decay_l)]
    attn16_l = [(_dot_nt(q.astype(_BF16), k16) * d).astype(_BF16)
                for q, k16, d in zip(q_l, k16_l, decay_l)]

    low2_l = [_split2(l) for l in low_l]
    x_l = [eye - l for l in low_l]
    m_l = [_dot3(l2, l2) for l2 in low2_l]
    n_fac = max(1, int(math.ceil(math.log2(c)))) - 1
    for f in range(n_fac):
        m2_l = [_split2(m) for m in m_l]
        x_l = [x + _dot3(_split2(x), m2) for x, m2 in zip(x_l, m2_l)]
        if f + 1 < n_fac:
            m_l = [_dot3(m2, m2) for m2 in m2_l]
    eg_l = [jnp.exp(g) for g in gcum_l]
    uw_l = [_dot3(_split2(x), _split2(jnp.concatenate([v * b, kb * eg], axis=1)))
            for x, v, b, kb, eg in zip(x_l, v_l, beta_l, kb_l, eg_l)]
    u_l = [uw[:, :HEAD_DIM] for uw in uw_l]
    w16_l = [uw[:, HEAD_DIM:].astype(_BF16) for uw in uw_l]
    qe16_l = [(q * eg).astype(_BF16) for q, eg in zip(q_l, eg_l)]
    glast_l = [g[c - 1:c, :] for g in gcum_l]
    kdec16_l = [(k * jnp.exp(gl - g)).astype(_BF16) for k, gl, g in zip(k_l, glast_l, gcum_l)]
    eglast_l = [jnp.exp(gl) for gl in glast_l]

    state = [s_scr[hd] for hd in heads]
    for ci in range(nc):
        p0 = ci * DN_HEADS
        s16 = [s.astype(_BF16) for s in state]
        vnew = [u_l[p0 + hd] - _dot(w16_l[p0 + hd], s16[hd]) for hd in heads]
        vnew16 = [v.astype(_BF16) for v in vnew]
        o = [_dot(qe16_l[p0 + hd], s16[hd]) + _dot(attn16_l[p0 + hd], vnew16[hd]) for hd in heads]
        state = [state[hd] * eglast_l[p0 + hd] + _dot_tn(kdec16_l[p0 + hd], vnew16[hd]) for hd in heads]
        for hd in heads:
            zz = z_ref[0, rows_of(ci), hd * HEAD_DIM:(hd + 1) * HEAD_DIM]
            od_ref[0, rows_of(ci), hd * HEAD_DIM:(hd + 1) * HEAD_DIM] = (
                _rms(o[hd], dnw_ref[...]) * _silu(zz)).astype(_BF16)
    for hd in heads:
        s_scr[hd] = state[hd]
    ssm_ref[0] = s_scr[...]


def _gdn(u, z, ab, conv0, ssm0, w_conv, alog_row, dtb_row, dn_w, *, lt, valid_len):
    B, L, _ = u.shape
    grid = (B, L // lt)
    tile = lambda b, i: (b, i, 0)
    perb3 = lambda b, i: (b, 0, 0)
    const = lambda b, i: (0, 0)
    return pl.pallas_call(
        functools.partial(_gdn_kernel, lt=lt, valid_len=valid_len),
        grid=grid,
        in_specs=[pl.BlockSpec((1, lt, CONV_DIM), tile), pl.BlockSpec((1, lt, DN_WIDTH), tile),
                  pl.BlockSpec((1, lt, LANES), tile), pl.BlockSpec((1, SUBLANES, CONV_DIM), perb3),
                  pl.BlockSpec((1, DN_HEADS, HEAD_DIM, HEAD_DIM), lambda b, i: (b, 0, 0, 0)),
                  pl.BlockSpec((CONV_W, CONV_DIM), const), pl.BlockSpec((1, LANES), const),
                  pl.BlockSpec((1, LANES), const), pl.BlockSpec((1, HEAD_DIM), const)],
        out_specs=[pl.BlockSpec((1, lt, DN_WIDTH), tile),
                   pl.BlockSpec((1, DN_HEADS, HEAD_DIM, HEAD_DIM), lambda b, i: (b, 0, 0, 0))],
        out_shape=[jax.ShapeDtypeStruct((B, L, DN_WIDTH), _BF16),
                   jax.ShapeDtypeStruct((B, DN_HEADS, HEAD_DIM, HEAD_DIM), _F32)],
        scratch_shapes=[pltpu.VMEM((lt + 2 * SUBLANES, CONV_DIM), _F32),
                        pltpu.VMEM((DN_HEADS, HEAD_DIM, HEAD_DIM), _F32)],
        compiler_params=_params("parallel", "arbitrary"),
        name="gdn",
    )(u, z, ab, conv0, ssm0, w_conv, alog_row, dtb_row, dn_w)


def _topk_rows(g, n_sel):
    r = g.shape[0]
    row = lax.broadcasted_iota(jnp.int32, g.shape, 0)
    sel = jnp.zeros(g.shape, _F32)
    for _ in range(n_sel):
        m = jnp.max(g, axis=0, keepdims=True)
        idx = jnp.min(jnp.where(g == m, row, r), axis=0, keepdims=True)
        pick = row == idx
        sel = jnp.where(pick, 1.0, sel)
        g = jnp.where(pick, -jnp.inf, g)
    return sel


def _moba_prompt_kernel(q_ref, k_ref, v_ref, o_ref, kb_scr, vt_scr, kmean_scr, sel_scr, *, nblk):
    qt = pl.program_id(2)
    blk = MOBA_BLOCK
    scale = HEAD_DIM ** -0.5

    @pl.when(qt == 0)
    def _():
        kmean_scr[...] = jnp.zeros(kmean_scr.shape, _F32)
        for j in range(nblk):
            kj = k_ref[0, 0, j * blk:(j + 1) * blk, :]
            kb_scr[j] = kj.astype(_BF16)
            kmean_scr[j:j + 1, :] = jnp.mean(kj, axis=0, keepdims=True)
            vt_scr[j] = v_ref[0, 0, j * blk:(j + 1) * blk, :].T.astype(_BF16)

    q = q_ref[0, 0]
    gate = _dot_nt(kmean_scr[...], q, _HI)
    brow = lax.broadcasted_iota(jnp.int32, gate.shape, 0)
    valid = brow < qt
    sel = _topk_rows(jnp.where(valid, gate, -jnp.inf), min(MOBA_TOPK, nblk))
    sel_scr[...] = jnp.where(valid, sel, 0.0)

    q16 = (q * scale).astype(_BF16)
    kpos = lax.broadcasted_iota(jnp.int32, (blk, blk), 0)
    qpos = lax.broadcasted_iota(jnp.int32, (blk, blk), 1)
    def scores(j):
        return jnp.where(sel_scr[pl.ds(j, 1), :] > 0.0, _dot_nt(kb_scr[j], q16), NEG_BIG)

    s = jnp.where(kpos <= qpos, _dot_nt(kb_scr[qt], q16), NEG_BIG)
    m = jnp.max(s, axis=0, keepdims=True)
    p = jnp.exp(s - m)
    l = jnp.sum(p, axis=0, keepdims=True)

    def body(j, carry):
        s_cur, p_prev, j_prev, m, l, acc = carry
        s_next = scores(j + 1)
        pv = _dot(vt_scr[j_prev], p_prev)
        m_new = jnp.maximum(m, jnp.max(s_cur, axis=0, keepdims=True))
        alpha = jnp.exp(m - m_new)
        p = jnp.exp(s_cur - m_new)
        l = alpha * l + jnp.sum(p, axis=0, keepdims=True)
        return s_next, p.astype(_BF16), j, m_new, l, alpha * (acc + pv)

    init = (scores(0), p.astype(_BF16), qt, m, l, jnp.zeros((HEAD_DIM, blk), _F32))
    _, p_prev, j_prev, m, l, acc = lax.fori_loop(0, qt, body, init)
    acc = acc + _dot(vt_scr[j_prev], p_prev)
    o_ref[0] = (acc / l).T.astype(_BF16)


def _moba_prompt(q, k, v):
    B, H, S, dh = q.shape
    nblk = S // MOBA_BLOCK
    nb8 = -(-nblk // SUBLANES) * SUBLANES
    assert S % MOBA_BLOCK == 0
    full = lambda b, h, t: (b, h, 0, 0)
    return pl.pallas_call(
        functools.partial(_moba_prompt_kernel, nblk=nblk),
        grid=(B, H, nblk),
        in_specs=[pl.BlockSpec((1, 1, MOBA_BLOCK, dh), lambda b, h, t: (b, h, t, 0)),
                  pl.BlockSpec((1, 1, S, dh), full), pl.BlockSpec((1, 1, S, dh), full)],
        out_specs=pl.BlockSpec((1, MOBA_BLOCK, dh), lambda b, h, t: (b, t, h)),
        out_shape=jax.ShapeDtypeStruct((B, S, H * dh), _BF16),
        scratch_shapes=[pltpu.VMEM((nblk, MOBA_BLOCK, dh), _BF16),
                        pltpu.VMEM((nblk, dh, MOBA_BLOCK), _BF16),
                        pltpu.VMEM((nb8, dh), _F32),
                        pltpu.VMEM((nb8, MOBA_BLOCK), _F32)],
        compiler_params=_params("parallel", "parallel", "arbitrary"),
        name="moba_prompt",
    )(q, k, v)


def _page_sum_kernel(c_ref, o_ref):
    o_ref[...] = jnp.sum(c_ref[...], axis=1)


def _page_sums(cache2d, rows_per_step):
    n, ps, dh = cache2d.shape
    return pl.pallas_call(
        _page_sum_kernel,
        grid=(n // rows_per_step,),
        in_specs=[pl.BlockSpec((rows_per_step, ps, dh), lambda i: (i, 0, 0))],
        out_specs=pl.BlockSpec((rows_per_step, dh), lambda i: (i, 0)),
        out_shape=jax.ShapeDtypeStruct((n, dh), _F32),
        compiler_params=_params("parallel"),
        name="page_sum",
    )(cache2d)


def _moba_sel_kernel(pt_ref, psum_ref, q_ref, sel_ref, km_scr, *, n_pages, ppb, n_blk, n_sel, rows_per_blk):
    b = pl.program_id(0)
    km_scr[...] = jnp.zeros(km_scr.shape, _F32)

    def fill(j, carry):
        acc = psum_ref[pl.ds(pt_ref[b * n_pages + j * ppb], 1), :]
        for r in range(1, ppb):
            acc = acc + psum_ref[pl.ds(pt_ref[b * n_pages + j * ppb + r], 1), :]
        km_scr[pl.ds(j, 1), :] = acc * (1.0 / rows_per_blk)
        return carry

    lax.fori_loop(0, n_blk, fill, 0)
    lane = lax.broadcasted_iota(jnp.int32, (SUBLANES, LANES), 1)
    for hd in range(MOBA_HEADS):
        g = _dot_nt(q_ref[0, hd], km_scr[:, hd * HEAD_DIM:(hd + 1) * HEAD_DIM], _HI)
        g = jnp.where(lane < n_blk, g, -jnp.inf)
        out = jnp.zeros((SUBLANES, LANES), jnp.int32)
        for r in range(n_sel):
            m = jnp.max(g, axis=1, keepdims=True)
            idx = jnp.min(jnp.where(g == m, lane, LANES), axis=1, keepdims=True)
            out = jnp.where(lane == r, idx, out)
            g = jnp.where(lane == idx, -jnp.inf, g)
        sel_ref[0, hd] = out


def _moba_sel(page_table, psum, q8, *, ppb, n_sel, rows_per_blk):
    DB, n_pages = page_table.shape
    n_blk = n_pages // ppb
    assert n_blk <= LANES
    n_phys, width = psum.shape
    return pl.pallas_call(
        functools.partial(_moba_sel_kernel, n_pages=n_pages, ppb=ppb, n_blk=n_blk, n_sel=n_sel,
                          rows_per_blk=rows_per_blk),
        grid_spec=pltpu.PrefetchScalarGridSpec(
            num_scalar_prefetch=1,
            grid=(DB,),
            in_specs=[pl.BlockSpec((n_phys, width), lambda b, pt: (0, 0)),
                      pl.BlockSpec((1, MOBA_HEADS, SUBLANES, HEAD_DIM), lambda b, pt: (b, 0, 0, 0))],
            out_specs=pl.BlockSpec((1, MOBA_HEADS, SUBLANES, LANES), lambda b, pt: (b, 0, 0, 0)),
            scratch_shapes=[pltpu.VMEM((LANES, width), _F32)]),
        out_shape=jax.ShapeDtypeStruct((DB, MOBA_HEADS, SUBLANES, LANES), jnp.int32),
        compiler_params=_params("arbitrary"),
        name="moba_sel",
    )(page_table.reshape(-1), psum, q8)


def _moba_sample_kernel(phys_ref, q_ref, kn_ref, vn_ref, *refs, n_pg, n_new):
    k_refs = refs[:n_pg]
    v_refs = refs[n_pg:2 * n_pg]
    o_ref = refs[2 * n_pg]
    t = pl.program_id(0) % n_new
    scale = HEAD_DIM ** -0.5
    q = q_ref[0, 0]
    q16 = q.astype(_BF16)
    kn = kn_ref[0, 0]
    vn = vn_ref[0, 0]
    rowq = lax.broadcasted_iota(jnp.int32, (SUBLANES, 1), 0)
    s_past = [_dot_nt(q16, kr[0, 0].astype(_BF16)) * scale for kr in k_refs]
    s_own = []
    for c in range(n_new):
        sc = jnp.sum(q * kn[c:c + 1, :], axis=-1, keepdims=True) * scale
        s_own.append(jnp.where(rowq >= c, sc, NEG_BIG))
    m = s_own[0]
    for sc in s_own[1:]:
        m = jnp.maximum(m, sc)
    for sp in s_past:
        m = jnp.maximum(m, jnp.max(sp, axis=-1, keepdims=True))
    l = jnp.zeros((SUBLANES, 1), _F32)
    acc = jnp.zeros((SUBLANES, HEAD_DIM), _F32)
    for sp, vr in zip(s_past, v_refs):
        p = jnp.exp(sp - m)
        l = l + jnp.sum(p, axis=-1, keepdims=True)
        acc = acc + _dot(p.astype(_BF16), vr[0, 0].astype(_BF16))
    for c, sc in enumerate(s_own):
        p = jnp.exp(sc - m)
        l = l + p
        acc = acc + p * vn[c:c + 1, :]
    o = acc / l
    o_ref[0] = jnp.sum(jnp.where(rowq == t, o, 0.0), axis=0, keepdims=True)


def _moba_sample(phys, q8, kn8, vn8, cache_k, cache_v, *, n_new):
    DB, H, _, dh = q8.shape
    ps = cache_k.shape[2]
    n_pg = phys.shape[0] // (DB * H * n_new)
    steps = DB * H * n_new
    small = pl.BlockSpec((1, 1, SUBLANES, dh), lambda s, ph: (s // (H * n_new), (s // n_new) % H, 0, 0))

    def page_spec(r):
        return pl.BlockSpec((1, 1, ps, dh), lambda s, ph: (ph[s * n_pg + r], (s // n_new) % H, 0, 0))

    return pl.pallas_call(
        functools.partial(_moba_sample_kernel, n_pg=n_pg, n_new=n_new),
        grid_spec=pltpu.PrefetchScalarGridSpec(
            num_scalar_prefetch=1,
            grid=(steps,),
            in_specs=[small, small, small] + [page_spec(r) for r in range(n_pg)] * 2,
            out_specs=pl.BlockSpec((1, 1, dh), lambda s, ph: (s, 0, 0))),
        out_shape=jax.ShapeDtypeStruct((steps, 1, dh), _F32),
        compiler_params=_params("arbitrary"),
        name="moba_sample",
    )(phys, q8, kn8, vn8, *([cache_k] * n_pg), *([cache_v] * n_pg))


def _post_kernel(xp_ref, odp_ref, omp_ref, xs_ref, ods_ref, oms_ref, wo1_ref, wo2_ref, ln_ref,
                 wrh_ref, wrl_ref, br_ref, x1_ref, xn_ref, meta_ref, cnt_ref, carry_scr,
                 *, tm, n_prompt_tiles, n_tokens):
    i = pl.program_id(0)

    @pl.when(i == 0)
    def _():
        carry_scr[...] = jnp.zeros(carry_scr.shape, _F32)

    is_p = i < n_prompt_tiles
    x = jnp.where(is_p, xp_ref[...], xs_ref[...])
    od = jnp.where(is_p, odp_ref[...], ods_ref[...])
    om = jnp.where(is_p, omp_ref[...], oms_ref[...])
    x1 = x + _dot(od, wo1_ref[...]) + _dot(om, wo2_ref[...])
    x1_ref[...] = x1
    xn = _rms(x1, ln_ref[...])
    _store_token_tiles(xn_ref, xn)
    xh, xl = _split2(xn)
    logits = (_dot(xh, wrh_ref[...]) + _dot(xl, wrh_ref[...]) + _dot(xh, wrl_ref[...])) + br_ref[...]

    lane = lax.broadcasted_iota(jnp.int32, (tm, LANES), 1)
    g = jnp.where(lane < N_EXPERTS, logits, -jnp.inf)
    vals, picks = [], []
    for _ in range(TOP_K):
        m = jnp.max(g, axis=-1, keepdims=True)
        idx = jnp.min(jnp.where(g == m, lane, LANES), axis=-1, keepdims=True)
        pick = lane == idx
        vals.append(m)
        picks.append((idx, pick))
        g = jnp.where(pick, -jnp.inf, g)
    es = [jnp.exp(v - vals[0]) for v in vals]
    den = es[0]
    for e in es[1:]:
        den = den + e

    rows = lax.broadcasted_iota(jnp.int32, (tm, tm), 0)
    cols = lax.broadcasted_iota(jnp.int32, (tm, tm), 1)
    before = jnp.where(cols < rows, 1.0, 0.0).astype(_BF16)
    real = jnp.where((i * tm + lax.broadcasted_iota(jnp.int32, (tm, 1), 0)) < n_tokens, 1.0, 0.0)
    base = carry_scr[...]
    meta = jnp.zeros((tm, LANES), _F32)
    for k, (idx, pick) in enumerate(picks):
        onehot = jnp.where(pick, real, 0.0)
        pref = _dot(before, onehot.astype(_BF16)) + base
        rank = jnp.sum(onehot * pref, axis=-1, keepdims=True)
        meta = jnp.where(lane == k, idx.astype(_F32), meta)
        meta = jnp.where(lane == TOP_K + k, es[k] / den, meta)
        meta = jnp.where(lane == 2 * TOP_K + k, rank, meta)
        base = base + jnp.sum(onehot, axis=0, keepdims=True)
    carry_scr[...] = base
    cnt_ref[...] = base
    meta_ref[...] = meta


def _post(xp, odp, omp, xs, ods, oms, wo1, wo2, ln_w, wr_hi, wr_lo, b_r, *, tm, n_tokens):
    Tp, D = xp.shape
    npt = Tp // tm
    T = Tp + tm
    prow = lambda t: (jnp.minimum(t, npt - 1), 0)
    srow = lambda t: (0, 0)
    row = lambda t: (t, 0)
    const = lambda t: (0, 0)
    return pl.pallas_call(
        functools.partial(_post_kernel, tm=tm, n_prompt_tiles=npt, n_tokens=n_tokens),
        grid=(npt + 1,),
        in_specs=[pl.BlockSpec((tm, D), prow), pl.BlockSpec((tm, DN_WIDTH), prow),
                  pl.BlockSpec((tm, ATT_WIDTH), prow),
                  pl.BlockSpec((tm, D), srow), pl.BlockSpec((tm, DN_WIDTH), srow),
                  pl.BlockSpec((tm, ATT_WIDTH), srow),
                  pl.BlockSpec((DN_WIDTH, D), const), pl.BlockSpec((ATT_WIDTH, D), const),
                  pl.BlockSpec((1, D), const), pl.BlockSpec((D, LANES), const),
                  pl.BlockSpec((D, LANES), const), pl.BlockSpec((1, LANES), const)],
        out_specs=[pl.BlockSpec((tm, D), row), pl.BlockSpec((tm * SUBLANES, LANES), row),
                   pl.BlockSpec((tm, LANES), row), pl.BlockSpec((1, LANES), const)],
        out_shape=[jax.ShapeDtypeStruct((T, D), _F32), jax.ShapeDtypeStruct((T * SUBLANES, LANES), _F32),
                   jax.ShapeDtypeStruct((T, LANES), _F32), jax.ShapeDtypeStruct((1, LANES), _F32)],
        scratch_shapes=[pltpu.VMEM((1, LANES), _F32)],
        compiler_params=_params("arbitrary"),
        name="post",
    )(xp, odp, omp, xs, ods, oms, wo1, wo2, ln_w, wr_hi, wr_lo, b_r)


def _dispatch_kernel(tail_ref, nu_ref, dest_hbm, xn_ref, xs_hbm, idx_smem, zero_scr, sem_idx, sem_zero,
                     sem_rows, *, tm, n_blocks):
    i = pl.program_id(0)
    n_asg = tm * TOP_K
    idx_copy = pltpu.make_async_copy(dest_hbm.at[pl.ds(i * n_asg, n_asg)], idx_smem, sem_idx)
    idx_copy.start()

    @pl.when(i == 0)
    def _():
        zero_scr[...] = jnp.zeros(zero_scr.shape, _F32)

        def clear(first_row):
            first = pl.multiple_of(first_row * SUBLANES, MOE_ROWS * SUBLANES)
            cp = pltpu.make_async_copy(zero_scr, xs_hbm.at[pl.ds(first, MOE_ROWS * SUBLANES)], sem_zero)
            cp.start()
            cp.wait()

        for e in range(N_EXPERTS):
            @pl.when(tail_ref[e] >= 0)
            def _():
                clear(tail_ref[e])

        def clear_block(b, carry):
            clear(b * MOE_ROWS)
            return carry

        lax.fori_loop(nu_ref[0], n_blocks, clear_block, 0)

    idx_copy.wait()

    def tile_copy(t, k):
        src = pl.multiple_of(t * SUBLANES, SUBLANES)
        dst = pl.multiple_of(idx_smem[t * TOP_K + k] * SUBLANES, SUBLANES)
        return pltpu.make_async_copy(xn_ref.at[pl.ds(src, SUBLANES)], xs_hbm.at[pl.ds(dst, SUBLANES)], sem_rows)

    def start(t, carry):
        for k in range(TOP_K):
            tile_copy(t, k).start()
        return carry

    def wait(t, carry):
        for k in range(TOP_K):
            tile_copy(t, k).wait()
        return carry

    lax.fori_loop(0, tm, start, 0, unroll=4)
    lax.fori_loop(0, tm, wait, 0, unroll=4)


def _dispatch(tail, n_used, dest, xn_tiles, n_blocks, n_rows, *, tm):
    T = xn_tiles.shape[0] // SUBLANES
    return pl.pallas_call(
        functools.partial(_dispatch_kernel, tm=tm, n_blocks=n_blocks),
        grid_spec=pltpu.PrefetchScalarGridSpec(
            num_scalar_prefetch=2,
            grid=(T // tm,),
            in_specs=[pl.BlockSpec(memory_space=pl.ANY),
                      pl.BlockSpec((tm * SUBLANES, LANES), lambda i, tail, nu: (i, 0))],
            out_specs=pl.BlockSpec(memory_space=pl.ANY),
            scratch_shapes=[pltpu.SMEM((tm * TOP_K,), jnp.int32),
                            pltpu.VMEM((MOE_ROWS * SUBLANES, LANES), _F32),
                            pltpu.SemaphoreType.DMA, pltpu.SemaphoreType.DMA, pltpu.SemaphoreType.DMA]),
        out_shape=jax.ShapeDtypeStruct((n_rows * SUBLANES, LANES), _F32),
        compiler_params=_params("arbitrary"),
        name="dispatch",
    )(tail, n_used, dest, xn_tiles)


def _moe_kernel(be_ref, nu_ref, x_ref, wg_ref, bg_ref, wu_ref, bu_ref, wd_ref, bd_ref, o_ref,
                wg_scr, wu_scr, wd_scr, x_scr):
    i = pl.program_id(0)

    @pl.when(i >= nu_ref[0])
    def _():
        o_ref[...] = jnp.zeros(o_ref.shape, _F32)

    @pl.when(i < nu_ref[0])
    def _():
        prev = be_ref[jnp.maximum(i - 1, 0)]

        @pl.when((i == 0) | (be_ref[i] != prev))
        def _():
            wg_scr[...] = wg_ref[0].astype(_BF16)
            wu_scr[...] = wu_ref[0].astype(_BF16)
            wd_scr[...] = wd_ref[0].astype(_BF16)

        for j in range(SUBLANES):
            x_scr[:, j * LANES:(j + 1) * LANES] = _load_token_tiles(x_ref, MOE_ROWS, j).astype(_BF16)
        x = x_scr[...]
        gt = jnp.minimum(_dot(x, wg_scr[...]) + bg_ref[0], SWIGLU_LIMIT)
        up = jnp.clip(_dot(x, wu_scr[...]) + bu_ref[0], -SWIGLU_LIMIT, SWIGLU_LIMIT)
        act = ((up + 1.0) * (gt * jax.nn.sigmoid(SWIGLU_ALPHA * gt))).astype(_BF16)
        _store_token_tiles(o_ref, _dot(act, wd_scr[...]) + bd_ref[0])


def _moe_ffn(block_e, n_used, xs, n_blocks, w_gate, b_gate, w_up, b_up, w_down, b_down):
    E, D, F = w_gate.shape
    assert D == SUBLANES * LANES
    rows = MOE_ROWS * SUBLANES
    xblk = lambda i, be, nu: (jnp.minimum(i, nu[0] - 1), 0)
    wsel = lambda i, be, nu: (be[jnp.minimum(i, nu[0] - 1)], 0, 0)
    return pl.pallas_call(
        _moe_kernel,
        grid_spec=pltpu.PrefetchScalarGridSpec(
            num_scalar_prefetch=2,
            grid=(n_blocks,),
            in_specs=[pl.BlockSpec((rows, LANES), xblk),
                      pl.BlockSpec((1, D, F), wsel), pl.BlockSpec((1, 1, F), wsel),
                      pl.BlockSpec((1, D, F), wsel), pl.BlockSpec((1, 1, F), wsel),
                      pl.BlockSpec((1, F, D), wsel), pl.BlockSpec((1, 1, D), wsel)],
            out_specs=pl.BlockSpec((rows, LANES), lambda i, be, nu: (i, 0)),
            scratch_shapes=[pltpu.VMEM((D, F), _BF16), pltpu.VMEM((D, F), _BF16),
                            pltpu.VMEM((F, D), _BF16), pltpu.VMEM((MOE_ROWS, D), _BF16)]),
        out_shape=jax.ShapeDtypeStruct((n_blocks * rows, LANES), _F32),
        compiler_params=_params("arbitrary"),
        name="moe_ffn",
    )(block_e, n_used, xs, w_gate, b_gate.reshape(E, 1, F), w_up, b_up.reshape(E, 1, F),
      w_down, b_down.reshape(E, 1, D))


def _combine_kernel(dest_hbm, ys_hbm, x1_ref, meta_ref, yp_ref, ysm_ref, idx_smem, buf, sem_idx, sem_rows,
                    *, tm, n_prompt_tiles):
    i = pl.program_id(0)
    n_asg = tm * TOP_K
    idx_copy = pltpu.make_async_copy(dest_hbm.at[pl.ds(i * n_asg, n_asg)], idx_smem, sem_idx)
    idx_copy.start()
    idx_copy.wait()

    def tile_copy(t, k):
        src = pl.multiple_of(idx_smem[t * TOP_K + k] * SUBLANES, SUBLANES)
        dst = pl.multiple_of(t * SUBLANES, SUBLANES)
        return pltpu.make_async_copy(ys_hbm.at[pl.ds(src, SUBLANES)], buf.at[k, pl.ds(dst, SUBLANES)], sem_rows)

    def start(t, carry):
        for k in range(TOP_K):
            tile_copy(t, k).start()
        return carry

    def wait(t, carry):
        for k in range(TOP_K):
            tile_copy(t, k).wait()
        return carry

    lax.fori_loop(0, tm, start, 0, unroll=4)
    lax.fori_loop(0, tm, wait, 0, unroll=4)

    meta = meta_ref[...]
    gates = [meta[:, TOP_K + k:TOP_K + k + 1] for k in range(TOP_K)]

    def emit(o_ref):
        for j in range(SUBLANES):
            cols = slice(j * LANES, (j + 1) * LANES)
            out = x1_ref[:, cols]
            for k in range(TOP_K):
                out = out + gates[k] * _load_token_tiles(buf.at[k], tm, j)
            o_ref[:, cols] = out

    @pl.when(i < n_prompt_tiles)
    def _():
        emit(yp_ref)

    @pl.when(i >= n_prompt_tiles)
    def _():
        emit(ysm_ref)


def _combine(dest, ys, x1_all, meta, *, tm):
    T, D = x1_all.shape
    npt = T // tm - 1
    row = lambda i: (i, 0)
    return pl.pallas_call(
        functools.partial(_combine_kernel, tm=tm, n_prompt_tiles=npt),
        grid=(npt + 1,),
        in_specs=[pl.BlockSpec(memory_space=pl.ANY), pl.BlockSpec(memory_space=pl.ANY),
                  pl.BlockSpec((tm, D), row), pl.BlockSpec((tm, LANES), row)],
        out_specs=[pl.BlockSpec((tm, D), lambda i: (jnp.minimum(i, npt - 1), 0)),
                   pl.BlockSpec((tm, D), lambda i: (0, 0))],
        out_shape=[jax.ShapeDtypeStruct((T - tm, D), _F32), jax.ShapeDtypeStruct((tm, D), _F32)],
        scratch_shapes=[pltpu.SMEM((tm * TOP_K,), jnp.int32),
                        pltpu.VMEM((TOP_K, tm * SUBLANES, LANES), _F32),
                        pltpu.SemaphoreType.DMA, pltpu.SemaphoreType.DMA],
        compiler_params=_params("arbitrary"),
        name="combine",
    )(dest, ys, x1_all, meta)


def _pad_lanes(v, fill=0.0):
    v = v.reshape(1, -1).astype(_F32)
    return jnp.pad(v, ((0, 0), (0, LANES - v.shape[1])), constant_values=fill)


def _pad_rows(a, rows):
    return jnp.pad(a, ((0, rows - a.shape[0]), (0, 0)))


def _prep_w_in(w_in):
    off_a = CONV_DIM + DN_WIDTH
    off_mq = off_a + 2 * DN_HEADS
    ab = jnp.pad(w_in[:, off_a:off_mq], ((0, 0), (0, LANES - 2 * DN_HEADS)))
    return jnp.concatenate([w_in[:, :off_a], w_in[:, off_mq:], ab], axis=1).astype(_BF16)


def _moe(xn_all, x1_all, meta, counts, n_tokens, w_gate, b_gate, w_up, b_up, w_down, b_down):
    T = x1_all.shape[0]
    idx = meta[:, 0:TOP_K].astype(jnp.int32)
    rank = meta[:, 2 * TOP_K:3 * TOP_K].astype(jnp.int32)
    cnt = counts[0, :N_EXPERTS].astype(jnp.int32)
    padded = (cnt + MOE_ROWS - 1) // MOE_ROWS * MOE_ROWS
    pends = jnp.cumsum(padded)
    pstarts = pends - padded
    n_blocks = -(-n_tokens * TOP_K // MOE_ROWS) + N_EXPERTS
    real = (jnp.arange(T, dtype=jnp.int32) < n_tokens)[:, None]
    dest = pstarts[idx] + rank
    spare = n_blocks * MOE_ROWS + jnp.arange((T - n_tokens) * TOP_K, dtype=jnp.int32)
    spare = jnp.pad(spare, (n_tokens * TOP_K, 0)).reshape(T, TOP_K)
    dest_scatter = jnp.where(real, dest, spare).reshape(-1)
    dest_gather = jnp.where(real, dest, 0).reshape(-1)
    block_e = jnp.minimum(
        jnp.searchsorted(pends, jnp.arange(n_blocks, dtype=jnp.int32) * MOE_ROWS, side="right"),
        N_EXPERTS - 1).astype(jnp.int32)
    n_used = (pends[-1:] // MOE_ROWS).astype(jnp.int32)
    tail = jnp.where(cnt > 0, pends - MOE_ROWS, -1).astype(jnp.int32)
    n_rows = n_blocks * MOE_ROWS + -(-(T - n_tokens) * TOP_K // MOE_ROWS) * MOE_ROWS
    xs = _dispatch(tail, n_used, dest_scatter, xn_all, n_blocks, n_rows, tm=TOK_TILE)
    ys = _moe_ffn(block_e, n_used, xs, n_blocks, w_gate, b_gate, w_up, b_up, w_down, b_down)
    return _combine(dest_gather, ys, x1_all, meta, tm=TOK_TILE)


def _layer(xp, xs_, cache_k, cache_v, conv_s, ssm_s, page_table, lw):
    (ln1_w, w_in, w_conv, a_log, dt_bias, dn_norm_w, q_norm_w, k_norm_w, w_o, ln2_w, w_router,
     b_router, w_gate, b_gate, w_up, b_up, w_down, b_down) = lw
    B, S, D = xp.shape
    DB, L, _ = xs_.shape
    n_phys, H, PS, dh = cache_k.shape
    n_pages = page_table.shape[1]
    ppb = MOBA_BLOCK // PS
    Tp, Ts = B * S, DB * L
    assert (n_pages * PS) % MOBA_BLOCK == 0, "paged past must end on a MoBA block boundary"
    assert CONV_W - 1 <= L <= SUBLANES and Ts <= TOK_TILE
    assert Tp % PROJ_ROWS == 0 and S % GDN_ROWS == 0 and Tp % TOK_TILE == 0
    assert (n_phys * H) % PAGE_SUM_ROWS == 0
    cur = n_pages // ppb
    n_sel = min(MOBA_TOPK, cur)
    assert n_sel > 0

    w_all = _prep_w_in(w_in)
    ln1 = ln1_w.reshape(1, D)
    qn = q_norm_w.reshape(1, dh)
    kn = k_norm_w.reshape(1, dh)
    alog_row = _pad_lanes(a_log)
    dtb_row = _pad_lanes(dt_bias)
    dnw = dn_norm_w.reshape(1, dh)

    u_p, z_p, ab_p, mq_p, mk_p, mv_p = _proj(xp.reshape(Tp, D), ln1, w_all, qn, kn, tm=PROJ_ROWS, seq=(B, S))
    od_p, ssm_p = _gdn(u_p.reshape(B, S, CONV_DIM), z_p.reshape(B, S, DN_WIDTH), ab_p.reshape(B, S, LANES),
                       jnp.zeros((B, SUBLANES, CONV_DIM), _F32), jnp.zeros((B, DN_HEADS, dh, dh), _F32),
                       w_conv, alog_row, dtb_row, dnw, lt=GDN_ROWS, valid_len=GDN_ROWS)
    om_p = _moba_prompt(mq_p, mk_p, mv_p)
    conv_p = u_p.reshape(B, S, CONV_DIM)[:, S - (CONV_W - 1):]

    u_s, z_s, ab_s, mq_s, mk_s, mv_s = _proj(xs_.reshape(Ts, D), ln1, w_all, qn, kn, tm=Ts)
    padl = lambda a: jnp.pad(a.reshape(DB, L, -1), ((0, 0), (0, DN_CHUNK - L), (0, 0)))
    conv0 = jnp.pad(conv_s, ((0, 0), (SUBLANES - (CONV_W - 1), 0), (0, 0)))
    od_s, ssm_s_new = _gdn(padl(u_s), padl(z_s), padl(ab_s), conv0, ssm_s, w_conv, alog_row, dtb_row, dnw,
                           lt=DN_CHUNK, valid_len=L)
    od_s = od_s[:, :L].reshape(Ts, DN_WIDTH)
    conv_s_new = jnp.concatenate([conv_s, u_s.reshape(DB, L, CONV_DIM)], axis=1)[:, L:]

    heads = lambda a: a.reshape(DB, L, H, dh).transpose(0, 2, 1, 3)
    pad8 = lambda a: jnp.pad(a, ((0, 0), (0, 0), (0, SUBLANES - L), (0, 0)))
    q8, k8, v8 = pad8(heads(mq_s)), pad8(heads(mk_s)), pad8(heads(mv_s))
    psum = _page_sums(cache_k.reshape(n_phys * H, PS, dh), PAGE_SUM_ROWS).reshape(n_phys, H * dh)
    sel = _moba_sel(page_table, psum, q8, ppb=ppb, n_sel=n_sel, rows_per_blk=MOBA_BLOCK)
    sel = sel[:, :, :L, :n_sel]
    logical = sel[..., None] * ppb + jnp.arange(ppb, dtype=jnp.int32)
    phys = page_table[jnp.arange(DB)[:, None, None, None, None], logical]
    om_s = _moba_sample(phys.reshape(-1).astype(jnp.int32), q8, k8, v8, cache_k, cache_v, n_new=L)
    om_s = om_s.reshape(DB, H, L, dh).transpose(0, 2, 1, 3).reshape(Ts, H * dh).astype(_BF16)

    wo = w_o.astype(_BF16)
    wr = jnp.pad(w_router, ((0, 0), (0, LANES - N_EXPERTS)))
    wr_hi = wr.astype(_BF16)
    wr_lo = (wr - wr_hi.astype(_F32)).astype(_BF16)
    x1_all, xn_all, meta, cnt = _post(
        xp.reshape(Tp, D), od_p.reshape(Tp, DN_WIDTH), om_p.reshape(Tp, ATT_WIDTH),
        _pad_rows(xs_.reshape(Ts, D), TOK_TILE), _pad_rows(od_s, TOK_TILE), _pad_rows(om_s, TOK_TILE),
        wo[:DN_WIDTH], wo[DN_WIDTH:], ln2_w.reshape(1, D), wr_hi, wr_lo, _pad_lanes(b_router),
        tm=TOK_TILE, n_tokens=Tp + Ts)
    y_p, y_s = _moe(xn_all, x1_all, meta, cnt, Tp + Ts, w_gate, b_gate, w_up, b_up, w_down, b_down)
    return (y_p.reshape(B, S, D), y_s[:Ts].reshape(DB, L, D), mk_p, mv_p, conv_p, ssm_p,
            heads(mk_s), heads(mv_s), conv_s_new, ssm_s_new)


def kernel(x_prompt, x_sample, cache_k, cache_v, state_conv, state_ssm, page_table, ln1_w, w_in, w_conv,
           a_log, dt_bias, dn_norm_w, q_norm_w, k_norm_w, w_o, ln2_w, w_router, b_router, w_gate, b_gate,
           w_up, b_up, w_down, b_down):
    weights = (ln1_w, w_in, w_conv, a_log, dt_bias, dn_norm_w, q_norm_w, k_norm_w, w_o, ln2_w, w_router,
               b_router, w_gate, b_gate, w_up, b_up, w_down, b_down)
    depth = w_in.shape[0]
    yp, ys = x_prompt, x_sample
    outs = [[] for _ in range(8)]
    for l in range(depth):
        res = _layer(yp, ys, cache_k[l], cache_v[l], state_conv[l], state_ssm[l], page_table,
                     tuple(w[l] for w in weights))
        yp, ys = res[0], res[1]
        for acc, r in zip(outs, res[2:]):
            acc.append(r)
    return (yp, ys) + tuple(jnp.stack(o) for o in outs)
```

```python
import functools
import math

import jax
import jax.numpy as jnp
from jax import lax
from jax.experimental import pallas as pl
from jax.experimental.pallas import tpu as pltpu

HEAD_DIM = 128
DN_HEADS = 4
MOBA_HEADS = 4
DN_WIDTH = DN_HEADS * HEAD_DIM
ATT_WIDTH = MOBA_HEADS * HEAD_DIM
CONV_W = 4
CONV_DIM = 3 * DN_WIDTH
DN_CHUNK = 64
MOBA_BLOCK = 256
MOBA_TOPK = 3
N_EXPERTS = 32
TOP_K = 4
SWIGLU_LIMIT = 7.0
SWIGLU_ALPHA = 1.702
EPS = 1e-6

LANES = 128
SUBLANES = 8
VMEM_LIMIT = 56 * 1024 * 1024
NEG_BIG = -1e30

PROJ_ROWS = 512
GDN_ROWS = 256
POST_ROWS = 512
TOK_TILE = 256
MOE_ROWS = 512
ROUTE_ROWS = 16

_HI = lax.Precision.HIGHEST
_F32 = jnp.float32
_BF16 = jnp.bfloat16


def _dot(a, b, precision=None):
    return jnp.dot(a, b, preferred_element_type=_F32, precision=precision)


def _dot_nt(a, b, precision=None):
    return lax.dot_general(a, b, (((1,), (1,)), ((), ())),
                           preferred_element_type=_F32, precision=precision)


def _dot_tn(a, b, precision=None):
    return lax.dot_general(a, b, (((0,), (0,)), ((), ())),
                           preferred_element_type=_F32, precision=precision)


def _split2(x):
    hi = x.astype(_BF16)
    return hi, (x - hi.astype(_F32)).astype(_BF16)


def _split3(x):
    hi = x.astype(_BF16)
    r = x - hi.astype(_F32)
    mid = r.astype(_BF16)
    return hi, mid, (r - mid.astype(_F32)).astype(_BF16)


def _dot3(a, b, dot=_dot):
    return dot(a[0], b[0]) + dot(a[1], b[0]) + dot(a[0], b[1])


def _rms(x, w):
    return x * lax.rsqrt(jnp.mean(x * x, axis=-1, keepdims=True) + EPS) * w


def _store_token_tiles(ref, x):
    n = x.shape[0]
    for j in range(SUBLANES):
        ref[pl.ds(j, n, stride=SUBLANES), :] = x[:, j * LANES:(j + 1) * LANES]


def _load_token_tiles(ref, n, j):
    return ref[pl.ds(j, n, stride=SUBLANES), :]


def _silu(x):
    return x * jax.nn.sigmoid(x)


def _params(*sem):
    return pltpu.CompilerParams(dimension_semantics=sem, vmem_limit_bytes=VMEM_LIMIT)


def _proj_kernel(x_ref, ln_ref, w_ref, qn_ref, kn_ref,
                 u_ref, z_ref, ab_ref, mq_ref, mk_ref, mv_ref, *, heads_out):
    xn = _rms(x_ref[...], ln_ref[...]).astype(_BF16)
    h = _dot(xn, w_ref[...])
    u_ref[...] = h[:, :CONV_DIM]
    z_ref[...] = h[:, CONV_DIM:CONV_DIM + DN_WIDTH]
    o = CONV_DIM + DN_WIDTH
    ab_ref[...] = h[:, o + 3 * ATT_WIDTH:]
    for hd in range(MOBA_HEADS):
        sl = slice(o + hd * HEAD_DIM, o + (hd + 1) * HEAD_DIM)
        q = _rms(h[:, sl], qn_ref[...])
        k = _rms(h[:, sl.start + ATT_WIDTH:sl.stop + ATT_WIDTH], kn_ref[...])
        v = h[:, sl.start + 2 * ATT_WIDTH:sl.stop + 2 * ATT_WIDTH]
        if heads_out:
            mq_ref[0, hd] = q
            mk_ref[0, hd] = k
            mv_ref[0, hd] = v
        else:
            hs = slice(hd * HEAD_DIM, (hd + 1) * HEAD_DIM)
            mq_ref[:, hs] = q
            mk_ref[:, hs] = k
            mv_ref[:, hs] = v


def _proj(x2d, ln_w, w_all, qn_w, kn_w, *, tm, seq=None):
    T, D = x2d.shape
    n_all = w_all.shape[1]
    grid = (T // tm,)
    row = lambda t: (t, 0)
    const = lambda t: (0, 0)
    if seq is not None:
        B, S = seq
        per = S // tm
        hshape = jax.ShapeDtypeStruct((B, MOBA_HEADS, S, HEAD_DIM), _F32)
        hspec = pl.BlockSpec((1, MOBA_HEADS, tm, HEAD_DIM), lambda t: (t // per, 0, t % per, 0))
    else:
        hshape = jax.ShapeDtypeStruct((T, ATT_WIDTH), _F32)
        hspec = pl.BlockSpec((tm, ATT_WIDTH), row)
    return pl.pallas_call(
        functools.partial(_proj_kernel, heads_out=seq is not None),
        grid=grid,
        in_specs=[pl.BlockSpec((tm, D), row), pl.BlockSpec((1, D), const),
                  pl.BlockSpec((D, n_all), const), pl.BlockSpec((1, HEAD_DIM), const),
                  pl.BlockSpec((1, HEAD_DIM), const)],
        out_specs=[pl.BlockSpec((tm, CONV_DIM), row), pl.BlockSpec((tm, DN_WIDTH), row),
                   pl.BlockSpec((tm, LANES), row), hspec, hspec, hspec],
        out_shape=[jax.ShapeDtypeStruct((T, CONV_DIM), _F32),
                   jax.ShapeDtypeStruct((T, DN_WIDTH), _F32),
                   jax.ShapeDtypeStruct((T, LANES), _F32), hshape, hshape, hshape],
        compiler_params=_params("parallel"),
        name="proj",
    )(x2d, ln_w, w_all, qn_w, kn_w)


def _gdn_kernel(u_ref, z_ref, ab_ref, conv0_ref, ssm0_ref, wconv_ref, alog_ref, dtb_ref, dnw_ref,
                od_ref, ssm_ref, up_scr, s_scr, *, lt, valid_len):
    i = pl.program_id(1)
    c = DN_CHUNK
    nc = lt // c
    heads = range(DN_HEADS)

    @pl.when(i == 0)
    def _():
        s_scr[...] = ssm0_ref[0]
        up_scr[0:SUBLANES, :] = conv0_ref[0]

    up_scr[SUBLANES:SUBLANES + lt, :] = u_ref[0]
    w = wconv_ref[...]
    base = SUBLANES - (CONV_W - 1)
    y = up_scr[base:base + lt, :] * w[0:1, :]
    for t in range(1, CONV_W):
        y = y + up_scr[base + t:base + t + lt, :] * w[t:t + 1, :]
    up_scr[0:SUBLANES, :] = up_scr[lt:lt + SUBLANES, :]
    qkv = _silu(y)

    masked = valid_len < lt
    if masked:
        rowv = lax.broadcasted_iota(jnp.int32, (lt, 1), 0) < valid_len
        qkv = jnp.where(rowv, qkv, 0.0)

    ab = ab_ref[0]
    lane = lax.broadcasted_iota(jnp.int32, (lt, LANES), 1)
    xg = ab + dtb_ref[...]
    softplus = jnp.maximum(xg, 0.0) + jnp.log(1.0 + jnp.exp(-jnp.abs(xg)))
    gb = jnp.where(lane < DN_HEADS, -jnp.exp(alog_ref[...]) * softplus, jax.nn.sigmoid(ab))
    if masked:
        gb = jnp.where(rowv, gb, 0.0)

    rows = lax.broadcasted_iota(jnp.int32, (c, c), 0)
    cols = lax.broadcasted_iota(jnp.int32, (c, c), 1)
    causal = cols <= rows
    strict = cols < rows
    eye = jnp.where(rows == cols, 1.0, 0.0)
    tril16 = jnp.where(causal, 1.0, 0.0).astype(_BF16)
    prow = lax.broadcasted_iota(jnp.int32, (DN_HEADS * c, LANES), 0) // c
    plane = lax.broadcasted_iota(jnp.int32, (DN_HEADS * c, LANES), 1)
    pick16 = jnp.where(prow == plane, 1.0, 0.0).astype(_BF16)

    cum, cum_t = [], []
    for ci in range(nc):
        g3 = _split3(gb[ci * c:(ci + 1) * c, :])
        cm = _dot(tril16, g3[0]) + _dot(tril16, g3[1]) + _dot(tril16, g3[2])
        c3 = _split3(cm)
        cum.append(cm)
        cum_t.append(_dot_nt(pick16, c3[0]) + _dot_nt(pick16, c3[1]) + _dot_nt(pick16, c3[2]))
    probs = [(ci, hd) for ci in range(nc) for hd in heads]

    def rows_of(ci):
        return slice(ci * c, (ci + 1) * c)

    q_l, k_l, v_l = [], [], []
    for ci, hd in probs:
        q = qkv[rows_of(ci), hd * HEAD_DIM:(hd + 1) * HEAD_DIM]
        k = qkv[rows_of(ci), DN_WIDTH + hd * HEAD_DIM:DN_WIDTH + (hd + 1) * HEAD_DIM]
        q_l.append(q * lax.rsqrt(jnp.sum(q * q, axis=-1, keepdims=True) + EPS) * (HEAD_DIM ** -0.5))
        k_l.append(k * lax.rsqrt(jnp.sum(k * k, axis=-1, keepdims=True) + EPS))
        v_l.append(qkv[rows_of(ci), 2 * DN_WIDTH + hd * HEAD_DIM:2 * DN_WIDTH + (hd + 1) * HEAD_DIM])
    gcum_l = [cum[ci][:, hd:hd + 1] for ci, hd in probs]
    beta_l = [gb[rows_of(ci), DN_HEADS + hd:DN_HEADS + hd + 1] for ci, hd in probs]
    decay_l = [jnp.exp(jnp.where(causal, g - cum_t[ci][hd * c:(hd + 1) * c, :], -jnp.inf))
               for (ci, hd), g in zip(probs, gcum_l)]
    k16_l = [k.astype(_BF16) for k in k_l]
    kb_l = [k * b for k, b in zip(k_l, beta_l)]
    low_l = [jnp.where(strict, _dot_nt(kb.astype(_BF16), k16) * d, 0.0)
             for kb, k16, d in zip(kb_l, k16_l, decay_l)]
    attn16_l = [(_dot_nt(q.astype(_BF16), k16) * d).astype(_BF16)
                for q, k16, d in zip(q_l, k16_l, decay_l)]

    low2_l = [_split2(l) for l in low_l]
    x_l = [eye - l for l in low_l]
    m_l = [_dot3(l2, l2) for l2 in low2_l]
    n_fac = max(1, int(math.ceil(math.log2(c)))) - 1
    for f in range(n_fac):
        m2_l = [_split2(m) for m in m_l]
        x_l = [x + _dot3(_split2(x), m2) for x, m2 in zip(x_l, m2_l)]
        if f + 1 < n_fac:
            m_l = [_dot3(m2, m2) for m2 in m2_l]
    eg_l = [jnp.exp(g) for g in gcum_l]
    uw_l = [_dot3(_split2(x), _split2(jnp.concatenate([v * b, kb * eg], axis=1)))
            for x, v, b, kb, eg in zip(x_l, v_l, beta_l, kb_l, eg_l)]
    u_l = [uw[:, :HEAD_DIM] for uw in uw_l]
    w16_l = [uw[:, HEAD_DIM:].astype(_BF16) for uw in uw_l]
    qe16_l = [(q * eg).astype(_BF16) for q, eg in zip(q_l, eg_l)]
    glast_l = [g[c - 1:c, :] for g in gcum_l]
    kdec16_l = [(k * jnp.exp(gl - g)).astype(_BF16) for k, gl, g in zip(k_l, glast_l, gcum_l)]
    eglast_l = [jnp.exp(gl) for gl in glast_l]

    state = [s_scr[hd] for hd in heads]
    for ci in range(nc):
        p0 = ci * DN_HEADS
        s16 = [s.astype(_BF16) for s in state]
        vnew = [u_l[p0 + hd] - _dot(w16_l[p0 + hd], s16[hd]) for hd in heads]
        vnew16 = [v.astype(_BF16) for v in vnew]
        o = [_dot(qe16_l[p0 + hd], s16[hd]) + _dot(attn16_l[p0 + hd], vnew16[hd]) for hd in heads]
        state = [state[hd] * eglast_l[p0 + hd] + _dot_tn(kdec16_l[p0 + hd], vnew16[hd]) for hd in heads]
        for hd in heads:
            zz = z_ref[0, rows_of(ci), hd * HEAD_DIM:(hd + 1) * HEAD_DIM]
            od_ref[0, rows_of(ci), hd * HEAD_DIM:(hd + 1) * HEAD_DIM] = (
                _rms(o[hd], dnw_ref[...]) * _silu(zz)).astype(_BF16)
    for hd in heads:
        s_scr[hd] = state[hd]
    ssm_ref[0] = s_scr[...]


def _gdn(u, z, ab, conv0, ssm0, w_conv, alog_row, dtb_row, dn_w, *, lt, valid_len):
    B, L, _ = u.shape
    grid = (B, L // lt)
    tile = lambda b, i: (b, i, 0)
    perb3 = lambda b, i: (b, 0, 0)
    const = lambda b, i: (0, 0)
    return pl.pallas_call(
        functools.partial(_gdn_kernel, lt=lt, valid_len=valid_len),
        grid=grid,
        in_specs=[pl.BlockSpec((1, lt, CONV_DIM), tile), pl.BlockSpec((1, lt, DN_WIDTH), tile),
                  pl.BlockSpec((1, lt, LANES), tile), pl.BlockSpec((1, SUBLANES, CONV_DIM), perb3),
                  pl.BlockSpec((1, DN_HEADS, HEAD_DIM, HEAD_DIM), lambda b, i: (b, 0, 0, 0)),
                  pl.BlockSpec((CONV_W, CONV_DIM), const), pl.BlockSpec((1, LANES), const),
                  pl.BlockSpec((1, LANES), const), pl.BlockSpec((1, HEAD_DIM), const)],
        out_specs=[pl.BlockSpec((1, lt, DN_WIDTH), tile),
                   pl.BlockSpec((1, DN_HEADS, HEAD_DIM, HEAD_DIM), lambda b, i: (b, 0, 0, 0))],
        out_shape=[jax.ShapeDtypeStruct((B, L, DN_WIDTH), _BF16),
                   jax.ShapeDtypeStruct((B, DN_HEADS, HEAD_DIM, HEAD_DIM), _F32)],
        scratch_shapes=[pltpu.VMEM((lt + 2 * SUBLANES, CONV_DIM), _F32),
                        pltpu.VMEM((DN_HEADS, HEAD_DIM, HEAD_DIM), _F32)],
        compiler_params=_params("parallel", "arbitrary"),
        name="gdn",
    )(u, z, ab, conv0, ssm0, w_conv, alog_row, dtb_row, dn_w)


def _topk_rows(g, n_sel):
    r = g.shape[0]
    row = lax.broadcasted_iota(jnp.int32, g.shape, 0)
    sel = jnp.zeros(g.shape, _F32)
    for _ in range(n_sel):
        m = jnp.max(g, axis=0, keepdims=True)
        idx = jnp.min(jnp.where(g == m, row, r), axis=0, keepdims=True)
        pick = row == idx
        sel = jnp.where(pick, 1.0, sel)
        g = jnp.where(pick, -jnp.inf, g)
    return sel


def _moba_prompt_kernel(q_ref, k_ref, v_ref, c_ref, o_ref, ps_ref, kb_scr, vt_scr, kmean_scr, sel_scr,
                        *, nblk):
    qt = pl.program_id(2)
    blk = MOBA_BLOCK
    scale = HEAD_DIM ** -0.5

    ps_ref[...] = jnp.sum(c_ref[...], axis=1)

    @pl.when(qt == 0)
    def _():
        kmean_scr[...] = jnp.zeros(kmean_scr.shape, _F32)
        for j in range(nblk):
            kj = k_ref[0, 0, j * blk:(j + 1) * blk, :]
            kb_scr[j] = kj.astype(_BF16)
            kmean_scr[j:j + 1, :] = jnp.mean(kj, axis=0, keepdims=True)
            vt_scr[j] = v_ref[0, 0, j * blk:(j + 1) * blk, :].T.astype(_BF16)

    q = q_ref[0, 0]
    gate = _dot_nt(kmean_scr[...], q, _HI)
    brow = lax.broadcasted_iota(jnp.int32, gate.shape, 0)
    valid = brow < qt
    sel = _topk_rows(jnp.where(valid, gate, -jnp.inf), min(MOBA_TOPK, nblk))
    sel_scr[...] = jnp.where(valid, sel, 0.0)

    q16 = (q * scale).astype(_BF16)
    kpos = lax.broadcasted_iota(jnp.int32, (blk, blk), 0)
    qpos = lax.broadcasted_iota(jnp.int32, (blk, blk), 1)
    def scores(j):
        return jnp.where(sel_scr[pl.ds(j, 1), :] > 0.0, _dot_nt(kb_scr[j], q16), NEG_BIG)

    s = jnp.where(kpos <= qpos, _dot_nt(kb_scr[qt], q16), NEG_BIG)
    m = jnp.max(s, axis=0, keepdims=True)
    p = jnp.exp(s - m)
    l = jnp.sum(p, axis=0, keepdims=True)

    def body(j, carry):
        s_cur, p_prev, j_prev, m, l, acc = carry
        s_next = scores(j + 1)
        pv = _dot(vt_scr[j_prev], p_prev)
        m_new = jnp.maximum(m, jnp.max(s_cur, axis=0, keepdims=True))
        alpha = jnp.exp(m - m_new)
        p = jnp.exp(s_cur - m_new)
        l = alpha * l + jnp.sum(p, axis=0, keepdims=True)
        return s_next, p.astype(_BF16), j, m_new, l, alpha * (acc + pv)

    init = (scores(0), p.astype(_BF16), qt, m, l, jnp.zeros((HEAD_DIM, blk), _F32))
    _, p_prev, j_prev, m, l, acc = lax.fori_loop(0, qt, body, init)
    acc = acc + _dot(vt_scr[j_prev], p_prev)
    o_ref[0] = (acc / l).T.astype(_BF16)


def _moba_prompt(q, k, v, slabs):
    B, H, S, dh = q.shape
    nblk = S // MOBA_BLOCK
    nb8 = -(-nblk // SUBLANES) * SUBLANES
    assert S % MOBA_BLOCK == 0
    n_slabs, ps, _ = slabs.shape
    steps = B * H * nblk
    share = n_slabs // steps
    assert share * steps == n_slabs and share % SUBLANES == 0, "key cache does not split evenly over the grid"
    full = lambda b, h, t: (b, h, 0, 0)
    flat = lambda b, h, t: (b * H + h) * nblk + t
    return pl.pallas_call(
        functools.partial(_moba_prompt_kernel, nblk=nblk),
        grid=(B, H, nblk),
        in_specs=[pl.BlockSpec((1, 1, MOBA_BLOCK, dh), lambda b, h, t: (b, h, t, 0)),
                  pl.BlockSpec((1, 1, S, dh), full), pl.BlockSpec((1, 1, S, dh), full),
                  pl.BlockSpec((share, ps, dh), lambda b, h, t: (flat(b, h, t), 0, 0))],
        out_specs=[pl.BlockSpec((1, MOBA_BLOCK, dh), lambda b, h, t: (b, t, h)),
                   pl.BlockSpec((share, dh), lambda b, h, t: (flat(b, h, t), 0))],
        out_shape=[jax.ShapeDtypeStruct((B, S, H * dh), _BF16),
                   jax.ShapeDtypeStruct((n_slabs, dh), _F32)],
        scratch_shapes=[pltpu.VMEM((nblk, MOBA_BLOCK, dh), _BF16),
                        pltpu.VMEM((nblk, dh, MOBA_BLOCK), _BF16),
                        pltpu.VMEM((nb8, dh), _F32),
                        pltpu.VMEM((nb8, MOBA_BLOCK), _F32)],
        compiler_params=_params("parallel", "parallel", "arbitrary"),
        name="moba_prompt",
    )(q, k, v, slabs)


def _moba_sel_kernel(pt_ref, psum_ref, q_ref, sel_ref, km_scr, *, n_pages, ppb, n_blk, n_sel, rows_per_blk):
    b = pl.program_id(0)
    km_scr[...] = jnp.zeros(km_scr.shape, _F32)

    def fill(j, carry):
        acc = psum_ref[pl.ds(pt_ref[b * n_pages + j * ppb], 1), :]
        for r in range(1, ppb):
            acc = acc + psum_ref[pl.ds(pt_ref[b * n_pages + j * ppb + r], 1), :]
        km_scr[pl.ds(j, 1), :] = acc * (1.0 / rows_per_blk)
        return carry

    lax.fori_loop(0, n_blk, fill, 0)
    lane = lax.broadcasted_iota(jnp.int32, (SUBLANES, LANES), 1)
    for hd in range(MOBA_HEADS):
        g = _dot_nt(q_ref[0, hd], km_scr[:, hd * HEAD_DIM:(hd + 1) * HEAD_DIM], _HI)
        g = jnp.where(lane < n_blk, g, -jnp.inf)
        out = jnp.zeros((SUBLANES, LANES), jnp.int32)
        for r in range(n_sel):
            m = jnp.max(g, axis=1, keepdims=True)
            idx = jnp.min(jnp.where(g == m, lane, LANES), axis=1, keepdims=True)
            out = jnp.where(lane == r, idx, out)
            g = jnp.where(lane == idx, -jnp.inf, g)
        sel_ref[0, hd] = out


def _moba_sel(page_table, psum, q8, *, ppb, n_sel, rows_per_blk):
    DB, n_pages = page_table.shape
    n_blk = n_pages // ppb
    assert n_blk <= LANES
    n_phys, width = psum.shape
    return pl.pallas_call(
        functools.partial(_moba_sel_kernel, n_pages=n_pages, ppb=ppb, n_blk=n_blk, n_sel=n_sel,
                          rows_per_blk=rows_per_blk),
        grid_spec=pltpu.PrefetchScalarGridSpec(
            num_scalar_prefetch=1,
            grid=(DB,),
            in_specs=[pl.BlockSpec((n_phys, width), lambda b, pt: (0, 0)),
                      pl.BlockSpec((1, MOBA_HEADS, SUBLANES, HEAD_DIM), lambda b, pt: (b, 0, 0, 0))],
            out_specs=pl.BlockSpec((1, MOBA_HEADS, SUBLANES, LANES), lambda b, pt: (b, 0, 0, 0)),
            scratch_shapes=[pltpu.VMEM((LANES, width), _F32)]),
        out_shape=jax.ShapeDtypeStruct((DB, MOBA_HEADS, SUBLANES, LANES), jnp.int32),
        compiler_params=_params("arbitrary"),
        name="moba_sel",
    )(page_table.reshape(-1), psum, q8)


def _moba_sample_kernel(phys_ref, q_ref, kn_ref, vn_ref, ck_hbm, cv_hbm, o_ref, kbuf, vbuf, sem,
                        *, n_pg, n_new, n_heads, n_steps):
    step = pl.program_id(0)
    per = n_new * n_pg
    scale = HEAD_DIM ** -0.5

    def page_copies(st, slot):
        hd = st % n_heads
        cps = []
        for j in range(per):
            page = phys_ref[st * per + j]
            cps.append(pltpu.make_async_copy(ck_hbm.at[page, hd], kbuf.at[slot, j], sem.at[slot]))
            cps.append(pltpu.make_async_copy(cv_hbm.at[page, hd], vbuf.at[slot, j], sem.at[slot]))
        return cps

    def attend(slot):
        q = q_ref[0, 0]
        q16 = q.astype(_BF16)
        kn = kn_ref[0, 0]
        vn = vn_ref[0, 0]
        rowq = lax.broadcasted_iota(jnp.int32, (SUBLANES, 1), 0)
        s_own = []
        for c in range(n_new):
            sc = jnp.sum(q * kn[c:c + 1, :], axis=-1, keepdims=True) * scale
            s_own.append(jnp.where(rowq >= c, sc, NEG_BIG))
        m_own = s_own[0]
        for sc in s_own[1:]:
            m_own = jnp.maximum(m_own, sc)
        out = jnp.zeros((SUBLANES, HEAD_DIM), _F32)
        for t in range(n_new):
            pages = range(t * n_pg, (t + 1) * n_pg)
            s_past = [_dot_nt(q16, kbuf[slot, j].astype(_BF16)) * scale for j in pages]
            m = m_own
            for sp in s_past:
                m = jnp.maximum(m, jnp.max(sp, axis=-1, keepdims=True))
            l = jnp.zeros((SUBLANES, 1), _F32)
            acc = jnp.zeros((SUBLANES, HEAD_DIM), _F32)
            for sp, j in zip(s_past, pages):
                p = jnp.exp(sp - m)
                l = l + jnp.sum(p, axis=-1, keepdims=True)
                acc = acc + _dot(p.astype(_BF16), vbuf[slot, j].astype(_BF16))
            for c, sc in enumerate(s_own):
                p = jnp.exp(sc - m)
                l = l + p
                acc = acc + p * vn[c:c + 1, :]
            out = jnp.where(rowq == t, acc / l, out)
        o_ref[0, 0] = out

    def run(slot):
        if slot == 0:
            @pl.when(step == 0)
            def _():
                for cp in page_copies(0, 0):
                    cp.start()

        @pl.when(step + 1 < n_steps)
        def _():
            for cp in page_copies(step + 1, 1 - slot):
                cp.start()

        for cp in page_copies(step, slot):
            cp.wait()
        attend(slot)

    for slot in range(2):
        @pl.when(step % 2 == slot)
        def _():
            run(slot)


def _moba_sample(phys, q8, kn8, vn8, cache_k, cache_v, *, n_new):
    DB, H, _, dh = q8.shape
    ps = cache_k.shape[2]
    n_pg = phys.shape[0] // (DB * H * n_new)
    steps = DB * H
    small = pl.BlockSpec((1, 1, SUBLANES, dh), lambda s, ph: (s // H, s % H, 0, 0))
    return pl.pallas_call(
        functools.partial(_moba_sample_kernel, n_pg=n_pg, n_new=n_new, n_heads=H, n_steps=steps),
        grid_spec=pltpu.PrefetchScalarGridSpec(
            num_scalar_prefetch=1,
            grid=(steps,),
            in_specs=[small, small, small, pl.BlockSpec(memory_space=pl.ANY),
                      pl.BlockSpec(memory_space=pl.ANY)],
            out_specs=small,
            scratch_shapes=[pltpu.VMEM((2, n_new * n_pg, ps, dh), _F32),
                            pltpu.VMEM((2, n_new * n_pg, ps, dh), _F32),
                            pltpu.SemaphoreType.DMA((2,))]),
        out_shape=jax.ShapeDtypeStruct((DB, H, SUBLANES, dh), _F32),
        compiler_params=_params("arbitrary"),
        name="moba_sample",
    )(phys, q8, kn8, vn8, cache_k, cache_v)


def _post_kernel(xp_ref, odp_ref, omp_ref, xs_ref, ods_ref, oms_ref, wo1_ref, wo2_ref, ln_ref,
                 wrh_ref, wrl_ref, br_ref, x1_ref, xn_ref, meta_ref, route_ref, cnt_ref, carry_scr,
                 *, tm, n_prompt_tiles, n_tokens):
    i = pl.program_id(0)

    @pl.when(i == 0)
    def _():
        carry_scr[...] = jnp.zeros(carry_scr.shape, _F32)

    is_p = i < n_prompt_tiles
    x = jnp.where(is_p, xp_ref[...], xs_ref[...])
    od = jnp.where(is_p, odp_ref[...], ods_ref[...])
    om = jnp.where(is_p, omp_ref[...], oms_ref[...])
    x1 = x + _dot(od, wo1_ref[...]) + _dot(om, wo2_ref[...])
    x1_ref[...] = x1
    xn = _rms(x1, ln_ref[...])
    _store_token_tiles(xn_ref, xn)
    xh, xl = _split2(xn)
    logits = (_dot_nt(wrh_ref[...], xh) + _dot_nt(wrh_ref[...], xl) + _dot_nt(wrl_ref[...], xh)) + br_ref[...]

    erow = lax.broadcasted_iota(jnp.int32, (N_EXPERTS, tm), 0)
    g = logits
    vals, picks = [], []
    for _ in range(TOP_K):
        m = jnp.max(g, axis=0, keepdims=True)
        idx = jnp.min(jnp.where(g == m, erow, N_EXPERTS), axis=0, keepdims=True)
        pick = erow == idx
        vals.append(m)
        picks.append((idx, pick))
        g = jnp.where(pick, -jnp.inf, g)
    es = [jnp.exp(v - vals[0]) for v in vals]
    den = es[0]
    for e in es[1:]:
        den = den + e

    rows = lax.broadcasted_iota(jnp.int32, (tm, tm), 0)
    cols = lax.broadcasted_iota(jnp.int32, (tm, tm), 1)
    earlier = jnp.where(rows < cols, 1.0, 0.0).astype(_BF16)
    real = jnp.where((i * tm + lax.broadcasted_iota(jnp.int32, (1, tm), 1)) < n_tokens, 1.0, 0.0)
    base = carry_scr[...]
    ranks = []
    for idx, pick in picks:
        onehot = jnp.where(pick, real, 0.0)
        pref = _dot(onehot.astype(_BF16), earlier) + base
        ranks.append(jnp.sum(onehot * pref, axis=0, keepdims=True))
        base = base + jnp.sum(onehot, axis=1, keepdims=True)
    carry_scr[...] = base
    cnt_ref[...] = base
    record = ([idx.astype(_F32) for idx, _ in picks] + [e / den for e in es] + ranks
              + [jnp.zeros((LANES - 3 * TOP_K, tm), _F32)])
    record = jnp.concatenate(record, axis=0)
    route_ref[...] = record[:route_ref.shape[0], :]
    meta_ref[...] = record.T


def _post(xp, odp, omp, xs, ods, oms, wo1, wo2, ln_w, wr_hi, wr_lo, b_r, *, tm, n_tokens):
    Tp, D = xp.shape
    npt = Tp // tm
    T = Tp + tm
    prow = lambda t: (jnp.minimum(t, npt - 1), 0)
    srow = lambda t: (0, 0)
    row = lambda t: (t, 0)
    const = lambda t: (0, 0)
    return pl.pallas_call(
        functools.partial(_post_kernel, tm=tm, n_prompt_tiles=npt, n_tokens=n_tokens),
        grid=(npt + 1,),
        in_specs=[pl.BlockSpec((tm, D), prow), pl.BlockSpec((tm, DN_WIDTH), prow),
                  pl.BlockSpec((tm, ATT_WIDTH), prow),
                  pl.BlockSpec((tm, D), srow), pl.BlockSpec((tm, DN_WIDTH), srow),
                  pl.BlockSpec((tm, ATT_WIDTH), srow),
                  pl.BlockSpec((DN_WIDTH, D), const), pl.BlockSpec((ATT_WIDTH, D), const),
                  pl.BlockSpec((1, D), const), pl.BlockSpec((N_EXPERTS, D), const),
                  pl.BlockSpec((N_EXPERTS, D), const), pl.BlockSpec((N_EXPERTS, 1), const)],
        out_specs=[pl.BlockSpec((tm, D), row), pl.BlockSpec((tm * SUBLANES, LANES), row),
                   pl.BlockSpec((tm, LANES), row), pl.BlockSpec((ROUTE_ROWS, tm), lambda t: (0, t)),
                   pl.BlockSpec((N_EXPERTS, 1), const)],
        out_shape=[jax.ShapeDtypeStruct((T, D), _F32), jax.ShapeDtypeStruct((T * SUBLANES, LANES), _F32),
                   jax.ShapeDtypeStruct((T, LANES), _F32), jax.ShapeDtypeStruct((ROUTE_ROWS, T), _F32),
                   jax.ShapeDtypeStruct((N_EXPERTS, 1), _F32)],
        scratch_shapes=[pltpu.VMEM((N_EXPERTS, 1), _F32)],
        compiler_params=_params("arbitrary"),
        name="post",
    )(xp, odp, omp, xs, ods, oms, wo1, wo2, ln_w, wr_hi, wr_lo, b_r)


def _dispatch_kernel(tail_ref, nu_ref, dest_hbm, xn_ref, xs_hbm, idx_smem, zero_scr, sem_idx, sem_zero,
                     sem_rows, *, tm, n_blocks):
    i = pl.program_id(0)
    n_asg = tm * TOP_K
    idx_copy = pltpu.make_async_copy(dest_hbm.at[pl.ds(i * n_asg, n_asg)], idx_smem, sem_idx)
    idx_copy.start()

    @pl.when(i == 0)
    def _():
        zero_scr[...] = jnp.zeros(zero_scr.shape, _F32)

        def clear(first_row):
            first = pl.multiple_of(first_row * SUBLANES, MOE_ROWS * SUBLANES)
            cp = pltpu.make_async_copy(zero_scr, xs_hbm.at[pl.ds(first, MOE_ROWS * SUBLANES)], sem_zero)
            cp.start()
            cp.wait()

        for e in range(N_EXPERTS):
            @pl.when(tail_ref[e] >= 0)
            def _():
                clear(tail_ref[e])

        def clear_block(b, carry):
            clear(b * MOE_ROWS)
            return carry

        lax.fori_loop(nu_ref[0], n_blocks, clear_block, 0)

    idx_copy.wait()

    def tile_copy(t, k):
        src = pl.multiple_of(t * SUBLANES, SUBLANES)
        dst = pl.multiple_of(idx_smem[k * tm + t] * SUBLANES, SUBLANES)
        return pltpu.make_async_copy(xn_ref.at[pl.ds(src, SUBLANES)], xs_hbm.at[pl.ds(dst, SUBLANES)], sem_rows)

    def start(t, carry):
        for k in range(TOP_K):
            tile_copy(t, k).start(priority=k % 2)
        return carry

    def wait(t, carry):
        for k in range(TOP_K):
            tile_copy(t, k).wait()
        return carry

    lax.fori_loop(0, tm, start, 0, unroll=4)
    lax.fori_loop(0, tm, wait, 0, unroll=4)


def _dispatch(tail, n_used, dest, xn_tiles, n_blocks, n_rows, *, tm):
    T = xn_tiles.shape[0] // SUBLANES
    return pl.pallas_call(
        functools.partial(_dispatch_kernel, tm=tm, n_blocks=n_blocks),
        grid_spec=pltpu.PrefetchScalarGridSpec(
            num_scalar_prefetch=2,
            grid=(T // tm,),
            in_specs=[pl.BlockSpec(memory_space=pl.ANY),
                      pl.BlockSpec((tm * SUBLANES, LANES), lambda i, tail, nu: (i, 0))],
            out_specs=pl.BlockSpec(memory_space=pl.ANY),
            scratch_shapes=[pltpu.SMEM((tm * TOP_K,), jnp.int32),
                            pltpu.VMEM((MOE_ROWS * SUBLANES, LANES), _F32),
                            pltpu.SemaphoreType.DMA, pltpu.SemaphoreType.DMA, pltpu.SemaphoreType.DMA]),
        out_shape=jax.ShapeDtypeStruct((n_rows * SUBLANES, LANES), _F32),
        compiler_params=_params("arbitrary"),
        name="dispatch",
    )(tail, n_used, dest, xn_tiles)


def _moe_kernel(be_ref, nu_ref, x_ref, wg_ref, bg_ref, wu_ref, bu_ref, wd_ref, bd_ref, o_ref,
                wg_scr, wu_scr, wd_scr, x_scr):
    i = pl.program_id(0)

    @pl.when(i >= nu_ref[0])
    def _():
        o_ref[...] = jnp.zeros(o_ref.shape, _F32)

    @pl.when(i < nu_ref[0])
    def _():
        prev = be_ref[jnp.maximum(i - 1, 0)]

        @pl.when((i == 0) | (be_ref[i] != prev))
        def _():
            wg_scr[...] = wg_ref[0].astype(_BF16)
            wu_scr[...] = wu_ref[0].astype(_BF16)
            wd_scr[...] = wd_ref[0].astype(_BF16)

        for j in range(SUBLANES):
            x_scr[:, j * LANES:(j + 1) * LANES] = _load_token_tiles(x_ref, MOE_ROWS, j).astype(_BF16)
        x = x_scr[...]
        gt = jnp.minimum(_dot(x, wg_scr[...]) + bg_ref[0], SWIGLU_LIMIT)
        up = jnp.clip(_dot(x, wu_scr[...]) + bu_ref[0], -SWIGLU_LIMIT, SWIGLU_LIMIT)
        act = ((up + 1.0) * (gt * jax.nn.sigmoid(SWIGLU_ALPHA * gt))).astype(_BF16)
        _store_token_tiles(o_ref, _dot(act, wd_scr[...]) + bd_ref[0])


def _moe_ffn(block_e, n_used, xs, n_blocks, w_gate, b_gate, w_up, b_up, w_down, b_down):
    E, D, F = w_gate.shape
    assert D == SUBLANES * LANES
    rows = MOE_ROWS * SUBLANES
    xblk = lambda i, be, nu: (jnp.minimum(i, nu[0] - 1), 0)
    wsel = lambda i, be, nu: (be[jnp.minimum(i, nu[0] - 1)], 0, 0)
    return pl.pallas_call(
        _moe_kernel,
        grid_spec=pltpu.PrefetchScalarGridSpec(
            num_scalar_prefetch=2,
            grid=(n_blocks,),
            in_specs=[pl.BlockSpec((rows, LANES), xblk),
                      pl.BlockSpec((1, D, F), wsel), pl.BlockSpec((1, 1, F), wsel),
                      pl.BlockSpec((1, D, F), wsel), pl.BlockSpec((1, 1, F), wsel),
                      pl.BlockSpec((1, F, D), wsel), pl.BlockSpec((1, 1, D), wsel)],
            out_specs=pl.BlockSpec((rows, LANES), lambda i, be, nu: (i, 0)),
            scratch_shapes=[pltpu.VMEM((D, F), _BF16), pltpu.VMEM((D, F), _BF16),
                            pltpu.VMEM((F, D), _BF16), pltpu.VMEM((MOE_ROWS, D), _BF16)]),
        out_shape=jax.ShapeDtypeStruct((n_blocks * rows, LANES), _F32),
        compiler_params=_params("arbitrary"),
        name="moe_ffn",
    )(block_e, n_used, xs, w_gate, b_gate.reshape(E, 1, F), w_up, b_up.reshape(E, 1, F),
      w_down, b_down.reshape(E, 1, D))


def _combine_kernel(dest_hbm, ys_hbm, x1_ref, meta_ref, yp_ref, ysm_ref, idx_smem, buf, sem_idx, sem_rows,
                    *, tm, n_prompt_tiles, n_tiles):
    i = pl.program_id(0)
    n_asg = tm * TOP_K

    def tile_copy(slot, t, k):
        src = pl.multiple_of(idx_smem[slot, k * tm + t] * SUBLANES, SUBLANES)
        dst = pl.multiple_of(t * SUBLANES, SUBLANES)
        return pltpu.make_async_copy(ys_hbm.at[pl.ds(src, SUBLANES)],
                                     buf.at[slot, k, pl.ds(dst, SUBLANES)], sem_rows.at[slot])

    def fetch(tile, slot):
        cp = pltpu.make_async_copy(dest_hbm.at[pl.ds(tile * n_asg, n_asg)], idx_smem.at[slot], sem_idx)
        cp.start()
        cp.wait()

        def start(t, carry):
            for k in range(TOP_K):
                tile_copy(slot, t, k).start(priority=k % 2)
            return carry

        lax.fori_loop(0, tm, start, 0, unroll=4)

    def step(slot):
        if slot == 0:
            @pl.when(i == 0)
            def _():
                fetch(0, 0)

        @pl.when(i + 1 < n_tiles)
        def _():
            fetch(i + 1, 1 - slot)

        def wait(t, carry):
            for k in range(TOP_K):
                tile_copy(slot, t, k).wait()
            return carry

        lax.fori_loop(0, tm, wait, 0, unroll=4)

        meta = meta_ref[...]
        gates = [meta[:, TOP_K + k:TOP_K + k + 1] for k in range(TOP_K)]

        def emit(o_ref):
            for j in range(SUBLANES):
                cols = slice(j * LANES, (j + 1) * LANES)
                out = x1_ref[:, cols]
                for k in range(TOP_K):
                    out = out + gates[k] * _load_token_tiles(buf.at[slot, k], tm, j)
                o_ref[:, cols] = out

        @pl.when(i < n_prompt_tiles)
        def _():
            emit(yp_ref)

        @pl.when(i >= n_prompt_tiles)
        def _():
            emit(ysm_ref)

    for slot in range(2):
        @pl.when(i % 2 == slot)
        def _():
            step(slot)


def _combine(dest, ys, x1_all, meta, n_prompt, *, tm):
    T, D = x1_all.shape
    npt = n_prompt // tm
    row = lambda i: (i, 0)
    return pl.pallas_call(
        functools.partial(_combine_kernel, tm=tm, n_prompt_tiles=npt, n_tiles=T // tm),
        grid=(T // tm,),
        in_specs=[pl.BlockSpec(memory_space=pl.ANY), pl.BlockSpec(memory_space=pl.ANY),
                  pl.BlockSpec((tm, D), row), pl.BlockSpec((tm, LANES), row)],
        out_specs=[pl.BlockSpec((tm, D), lambda i: (jnp.minimum(i, npt - 1), 0)),
                   pl.BlockSpec((tm, D), lambda i: (jnp.maximum(i - npt, 0), 0))],
        out_shape=[jax.ShapeDtypeStruct((n_prompt, D), _F32), jax.ShapeDtypeStruct((T - n_prompt, D), _F32)],
        scratch_shapes=[pltpu.SMEM((2, tm * TOP_K), jnp.int32),
                        pltpu.VMEM((2, TOP_K, tm * SUBLANES, LANES), _F32),
                        pltpu.SemaphoreType.DMA, pltpu.SemaphoreType.DMA((2,))],
        compiler_params=_params("arbitrary"),
        name="combine",
    )(dest, ys, x1_all, meta)


def _pad_lanes(v, fill=0.0):
    v = v.reshape(1, -1).astype(_F32)
    return jnp.pad(v, ((0, 0), (0, LANES - v.shape[1])), constant_values=fill)


def _pad_rows(a, rows):
    return jnp.pad(a, ((0, rows - a.shape[0]), (0, 0)))


def _prep_w_in(w_in):
    off_a = CONV_DIM + DN_WIDTH
    off_mq = off_a + 2 * DN_HEADS
    ab = jnp.pad(w_in[:, off_a:off_mq], ((0, 0), (0, LANES - 2 * DN_HEADS)))
    return jnp.concatenate([w_in[:, :off_a], w_in[:, off_mq:], ab], axis=1).astype(_BF16)


def _moe(xn_all, x1_all, meta, route_t, counts, n_prompt, n_tokens, w_gate, b_gate, w_up, b_up, w_down, b_down):
    T = x1_all.shape[0]
    tm = TOK_TILE
    idx = route_t[0:TOP_K].astype(jnp.int32)
    rank = route_t[2 * TOP_K:3 * TOP_K].astype(jnp.int32)
    cnt = counts[:, 0].astype(jnp.int32)
    padded = (cnt + MOE_ROWS - 1) // MOE_ROWS * MOE_ROWS
    pends = jnp.cumsum(padded)
    pstarts = pends - padded
    n_blocks = -(-n_tokens * TOP_K // MOE_ROWS) + N_EXPERTS
    tok = jnp.arange(T, dtype=jnp.int32)[None, :]
    real = tok < n_tokens
    dest = pstarts[idx] + rank
    spare = n_blocks * MOE_ROWS + (tok - n_tokens) * TOP_K + jnp.arange(TOP_K, dtype=jnp.int32)[:, None]
    per_tile = lambda d: d.reshape(TOP_K, T // tm, tm).transpose(1, 0, 2).reshape(-1)
    dest_scatter = per_tile(jnp.where(real, dest, spare))
    dest_gather = per_tile(jnp.where(real, dest, 0))
    block_start = jnp.arange(n_blocks, dtype=jnp.int32) * MOE_ROWS
    block_e = jnp.minimum(jnp.sum((pends[None, :] <= block_start[:, None]).astype(jnp.int32), axis=1),
                          N_EXPERTS - 1)
    n_used = (pends[-1:] // MOE_ROWS).astype(jnp.int32)
    tail = jnp.where(cnt > 0, pends - MOE_ROWS, -1).astype(jnp.int32)
    n_rows = n_blocks * MOE_ROWS + -(-(T - n_tokens) * TOP_K // MOE_ROWS) * MOE_ROWS
    xs = _dispatch(tail, n_used, dest_scatter, xn_all, n_blocks, n_rows, tm=tm)
    ys = _moe_ffn(block_e, n_used, xs, n_blocks, w_gate, b_gate, w_up, b_up, w_down, b_down)
    return _combine(dest_gather, ys, x1_all, meta, n_prompt, tm=tm)


def _layer(xp, xs_, cache_k, cache_v, conv_s, ssm_s, page_table, lw):
    (ln1_w, w_in, w_conv, a_log, dt_bias, dn_norm_w, q_norm_w, k_norm_w, w_o, ln2_w, w_router,
     b_router, w_gate, b_gate, w_up, b_up, w_down, b_down) = lw
    B, S, D = xp.shape
    DB, L, _ = xs_.shape
    n_phys, H, PS, dh = cache_k.shape
    n_pages = page_table.shape[1]
    ppb = MOBA_BLOCK // PS
    Tp, Ts = B * S, DB * L
    assert (n_pages * PS) % MOBA_BLOCK == 0, "paged past must end on a MoBA block boundary"
    assert CONV_W - 1 <= L <= SUBLANES and Ts <= POST_ROWS and POST_ROWS % TOK_TILE == 0
    assert Tp % PROJ_ROWS == 0 and S % GDN_ROWS == 0 and Tp % POST_ROWS == 0
    cur = n_pages // ppb
    n_sel = min(MOBA_TOPK, cur)
    assert n_sel > 0

    w_all = _prep_w_in(w_in)
    ln1 = ln1_w.reshape(1, D)
    qn = q_norm_w.reshape(1, dh)
    kn = k_norm_w.reshape(1, dh)
    alog_row = _pad_lanes(a_log)
    dtb_row = _pad_lanes(dt_bias)
    dnw = dn_norm_w.reshape(1, dh)

    u_p, z_p, ab_p, mq_p, mk_p, mv_p = _proj(xp.reshape(Tp, D), ln1, w_all, qn, kn, tm=PROJ_ROWS, seq=(B, S))
    od_p, ssm_p = _gdn(u_p.reshape(B, S, CONV_DIM), z_p.reshape(B, S, DN_WIDTH), ab_p.reshape(B, S, LANES),
                       jnp.zeros((B, SUBLANES, CONV_DIM), _F32), jnp.zeros((B, DN_HEADS, dh, dh), _F32),
                       w_conv, alog_row, dtb_row, dnw, lt=GDN_ROWS, valid_len=GDN_ROWS)
    om_p, psum = _moba_prompt(mq_p, mk_p, mv_p, cache_k.reshape(n_phys * H, PS, dh))
    conv_p = u_p.reshape(B, S, CONV_DIM)[:, S - (CONV_W - 1):]

    u_s, z_s, ab_s, mq_s, mk_s, mv_s = _proj(xs_.reshape(Ts, D), ln1, w_all, qn, kn, tm=Ts)
    padl = lambda a: jnp.pad(a.reshape(DB, L, -1), ((0, 0), (0, DN_CHUNK - L), (0, 0)))
    conv0 = jnp.pad(conv_s, ((0, 0), (SUBLANES - (CONV_W - 1), 0), (0, 0)))
    od_s, ssm_s_new = _gdn(padl(u_s), padl(z_s), padl(ab_s), conv0, ssm_s, w_conv, alog_row, dtb_row, dnw,
                           lt=DN_CHUNK, valid_len=L)
    od_s = od_s[:, :L].reshape(Ts, DN_WIDTH)
    conv_s_new = jnp.concatenate([conv_s, u_s.reshape(DB, L, CONV_DIM)], axis=1)[:, L:]

    heads = lambda a: a.reshape(DB, L, H, dh).transpose(0, 2, 1, 3)
    pad8 = lambda a: jnp.pad(a, ((0, 0), (0, 0), (0, SUBLANES - L), (0, 0)))
    q8, k8, v8 = pad8(heads(mq_s)), pad8(heads(mk_s)), pad8(heads(mv_s))
    psum = psum.reshape(n_phys, H * dh)
    sel = _moba_sel(page_table, psum, q8, ppb=ppb, n_sel=n_sel, rows_per_blk=MOBA_BLOCK)
    sel = sel[:, :, :L, :n_sel]
    logical = sel[..., None] * ppb + jnp.arange(ppb, dtype=jnp.int32)
    phys = page_table[jnp.arange(DB)[:, None, None, None, None], logical]
    om_s = _moba_sample(phys.reshape(-1).astype(jnp.int32), q8, k8, v8, cache_k, cache_v, n_new=L)
    om_s = om_s[:, :, :L].transpose(0, 2, 1, 3).reshape(Ts, H * dh).astype(_BF16)

    wo = w_o.astype(_BF16)
    wr = w_router.T
    wr_hi = wr.astype(_BF16)
    wr_lo = (wr - wr_hi.astype(_F32)).astype(_BF16)
    x1_all, xn_all, meta, route_t, cnt = _post(
        xp.reshape(Tp, D), od_p.reshape(Tp, DN_WIDTH), om_p.reshape(Tp, ATT_WIDTH),
        _pad_rows(xs_.reshape(Ts, D), POST_ROWS), _pad_rows(od_s, POST_ROWS), _pad_rows(om_s, POST_ROWS),
        wo[:DN_WIDTH], wo[DN_WIDTH:], ln2_w.reshape(1, D), wr_hi, wr_lo, b_router.reshape(N_EXPERTS, 1),
        tm=POST_ROWS, n_tokens=Tp + Ts)
    y_p, y_s = _moe(xn_all, x1_all, meta, route_t, cnt, Tp, Tp + Ts, w_gate, b_gate, w_up, b_up, w_down, b_down)
    return (y_p.reshape(B, S, D), y_s[:Ts].reshape(DB, L, D), mk_p, mv_p, conv_p, ssm_p,
            heads(mk_s), heads(mv_s), conv_s_new, ssm_s_new)


def kernel(x_prompt, x_sample, cache_k, cache_v, state_conv, state_ssm, page_table, ln1_w, w_in, w_conv,
           a_log, dt_bias, dn_norm_w, q_norm_w, k_norm_w, w_o, ln2_w, w_router, b_router, w_gate, b_gate,
           w_up, b_up, w_down, b_down):
    weights = (ln1_w, w_in, w_conv, a_log, dt_bias, dn_norm_w, q_norm_w, k_norm_w, w_o, ln2_w, w_router,
               b_router, w_gate, b_gate, w_up, b_up, w_down, b_down)
    depth = w_in.shape[0]
    yp, ys = x_prompt, x_sample
    outs = [[] for _ in range(8)]
    for l in range(depth):
        res = _layer(yp, ys, cache_k[l], cache_v[l], state_conv[l], state_ssm[l], page_table,
                     tuple(w[l] for w in weights))
        yp, ys = res[0], res[1]
        for acc, r in zip(outs, res[2:]):
            acc.append(r)
    return (yp, ys) + tuple(jnp.stack(o) for o in outs)
```

```python
import functools
import math

import jax
import jax.numpy as jnp
from jax import lax
from jax.experimental import pallas as pl
from jax.experimental.pallas import tpu as pltpu

HEAD_DIM = 128
DN_HEADS = 4
MOBA_HEADS = 4
DN_WIDTH = DN_HEADS * HEAD_DIM
ATT_WIDTH = MOBA_HEADS * HEAD_DIM
CONV_W = 4
CONV_DIM = 3 * DN_WIDTH
DN_CHUNK = 64
MOBA_BLOCK = 256
MOBA_TOPK = 3
N_EXPERTS = 32
TOP_K = 4
SWIGLU_LIMIT = 7.0
SWIGLU_ALPHA = 1.702
EPS = 1e-6

LANES = 128
SUBLANES = 8
VMEM_LIMIT = 56 * 1024 * 1024
NEG_BIG = -1e30

PROJ_ROWS = 512
GDN_ROWS = 256
POST_ROWS = 512
TOK_TILE = 256
MOE_ROWS = 512
ROUTE_ROWS = 16

_HI = lax.Precision.HIGHEST
_F32 = jnp.float32
_BF16 = jnp.bfloat16


def _dot(a, b, precision=None):
    return jnp.dot(a, b, preferred_element_type=_F32, precision=precision)


def _dot_nt(a, b, precision=None):
    return lax.dot_general(a, b, (((1,), (1,)), ((), ())),
                           preferred_element_type=_F32, precision=precision)


def _dot_tn(a, b, precision=None):
    return lax.dot_general(a, b, (((0,), (0,)), ((), ())),
                           preferred_element_type=_F32, precision=precision)


def _split2(x):
    hi = x.astype(_BF16)
    return hi, (x - hi.astype(_F32)).astype(_BF16)


def _split3(x):
    hi = x.astype(_BF16)
    r = x - hi.astype(_F32)
    mid = r.astype(_BF16)
    return hi, mid, (r - mid.astype(_F32)).astype(_BF16)


def _dot3(a, b, dot=_dot):
    return dot(a[0], b[0]) + dot(a[1], b[0]) + dot(a[0], b[1])


def _rms(x, w):
    return x * lax.rsqrt(jnp.mean(x * x, axis=-1, keepdims=True) + EPS) * w


def _store_token_tiles(ref, x):
    n = x.shape[0]
    for j in range(SUBLANES):
        ref[pl.ds(j, n, stride=SUBLANES), :] = x[:, j * LANES:(j + 1) * LANES]


def _load_token_tiles(ref, n, j):
    return ref[pl.ds(j, n, stride=SUBLANES), :]


def _silu(x):
    return x * jax.nn.sigmoid(x)


def _params(*sem):
    return pltpu.CompilerParams(dimension_semantics=sem, vmem_limit_bytes=VMEM_LIMIT)


def _proj_kernel(x_ref, ln_ref, w_ref, qn_ref, kn_ref,
                 u_ref, z_ref, ab_ref, mq_ref, mk_ref, mv_ref, *, heads_out):
    xn = _rms(x_ref[...], ln_ref[...]).astype(_BF16)
    h = _dot(xn, w_ref[...])
    u_ref[...] = h[:, :CONV_DIM]
    z_ref[...] = h[:, CONV_DIM:CONV_DIM + DN_WIDTH]
    o = CONV_DIM + DN_WIDTH
    ab_ref[...] = h[:, o + 3 * ATT_WIDTH:]
    for hd in range(MOBA_HEADS):
        sl = slice(o + hd * HEAD_DIM, o + (hd + 1) * HEAD_DIM)
        q = _rms(h[:, sl], qn_ref[...])
        k = _rms(h[:, sl.start + ATT_WIDTH:sl.stop + ATT_WIDTH], kn_ref[...])
        v = h[:, sl.start + 2 * ATT_WIDTH:sl.stop + 2 * ATT_WIDTH]
        if heads_out:
            mq_ref[0, hd] = q
            mk_ref[0, hd] = k
            mv_ref[0, hd] = v
        else:
            hs = slice(hd * HEAD_DIM, (hd + 1) * HEAD_DIM)
            mq_ref[:, hs] = q
            mk_ref[:, hs] = k
            mv_ref[:, hs] = v


def _proj(x2d, ln_w, w_all, qn_w, kn_w, *, tm, seq=None):
    T, D = x2d.shape
    n_all = w_all.shape[1]
    grid = (T // tm,)
    row = lambda t: (t, 0)
    const = lambda t: (0, 0)
    if seq is not None:
        B, S = seq
        per = S // tm
        hshape = jax.ShapeDtypeStruct((B, MOBA_HEADS, S, HEAD_DIM), _F32)
        hspec = pl.BlockSpec((1, MOBA_HEADS, tm, HEAD_DIM), lambda t: (t // per, 0, t % per, 0))
    else:
        hshape = jax.ShapeDtypeStruct((T, ATT_WIDTH), _F32)
        hspec = pl.BlockSpec((tm, ATT_WIDTH), row)
    return pl.pallas_call(
        functools.partial(_proj_kernel, heads_out=seq is not None),
        grid=grid,
        in_specs=[pl.BlockSpec((tm, D), row), pl.BlockSpec((1, D), const),
                  pl.BlockSpec((D, n_all), const), pl.BlockSpec((1, HEAD_DIM), const),
                  pl.BlockSpec((1, HEAD_DIM), const)],
        out_specs=[pl.BlockSpec((tm, CONV_DIM), row), pl.BlockSpec((tm, DN_WIDTH), row),
                   pl.BlockSpec((tm, LANES), row), hspec, hspec, hspec],
        out_shape=[jax.ShapeDtypeStruct((T, CONV_DIM), _F32),
                   jax.ShapeDtypeStruct((T, DN_WIDTH), _F32),
                   jax.ShapeDtypeStruct((T, LANES), _F32), hshape, hshape, hshape],
        compiler_params=_params("parallel"),
        name="proj",
    )(x2d, ln_w, w_all, qn_w, kn_w)


def _gdn_kernel(u_ref, z_ref, ab_ref, conv0_ref, ssm0_ref, wconv_ref, alog_ref, dtb_ref, dnw_ref,
                od_ref, ssm_ref, up_scr, s_scr, *, lt, valid_len):
    i = pl.program_id(1)
    c = DN_CHUNK
    nc = lt // c
    heads = range(DN_HEADS)

    @pl.when(i == 0)
    def _():
        s_scr[...] = ssm0_ref[0]
        up_scr[0:SUBLANES, :] = conv0_ref[0]

    up_scr[SUBLANES:SUBLANES + lt, :] = u_ref[0]
    w = wconv_ref[...]
    base = SUBLANES - (CONV_W - 1)
    y = up_scr[base:base + lt, :] * w[0:1, :]
    for t in range(1, CONV_W):
        y = y + up_scr[base + t:base + t + lt, :] * w[t:t + 1, :]
    up_scr[0:SUBLANES, :] = up_scr[lt:lt + SUBLANES, :]
    qkv = _silu(y)

    masked = valid_len < lt
    if masked:
        rowv = lax.broadcasted_iota(jnp.int32, (lt, 1), 0) < valid_len
        qkv = jnp.where(rowv, qkv, 0.0)

    ab = ab_ref[0]
    lane = lax.broadcasted_iota(jnp.int32, (lt, LANES), 1)
    xg = ab + dtb_ref[...]
    softplus = jnp.maximum(xg, 0.0) + jnp.log(1.0 + jnp.exp(-jnp.abs(xg)))
    gb = jnp.where(lane < DN_HEADS, -jnp.exp(alog_ref[...]) * softplus, jax.nn.sigmoid(ab))
    if masked:
        gb = jnp.where(rowv, gb, 0.0)

    rows = lax.broadcasted_iota(jnp.int32, (c, c), 0)
    cols = lax.broadcasted_iota(jnp.int32, (c, c), 1)
    causal = cols <= rows
    strict = cols < rows
    eye = jnp.where(rows == cols, 1.0, 0.0)
    tril16 = jnp.where(causal, 1.0, 0.0).astype(_BF16)
    prow = lax.broadcasted_iota(jnp.int32, (DN_HEADS * c, LANES), 0) // c
    plane = lax.broadcasted_iota(jnp.int32, (DN_HEADS * c, LANES), 1)
    pick16 = jnp.where(prow == plane, 1.0, 0.0).astype(_BF16)

    cum, cum_t = [], []
    for ci in range(nc):
        g3 = _split3(gb[ci * c:(ci + 1) * c, :])
        cm = _dot(tril16, g3[0]) + _dot(tril16, g3[1]) + _dot(tril16, g3[2])
        c3 = _split3(cm)
        cum.append(cm)
        cum_t.append(_dot_nt(pick16, c3[0]) + _dot_nt(pick16, c3[1]) + _dot_nt(pick16, c3[2]))
    probs = [(ci, hd) for ci in range(nc) for hd in heads]

    def rows_of(ci):
        return slice(ci * c, (ci + 1) * c)

    q_l, k_l, v_l = [], [], []
    for ci, hd in probs:
        q = qkv[rows_of(ci), hd * HEAD_DIM:(hd + 1) * HEAD_DIM]
        k = qkv[rows_of(ci), DN_WIDTH + hd * HEAD_DIM:DN_WIDTH + (hd + 1) * HEAD_DIM]
        q_l.append(q * lax.rsqrt(jnp.sum(q * q, axis=-1, keepdims=True) + EPS) * (HEAD_DIM ** -0.5))
        k_l.append(k * lax.rsqrt(jnp.sum(k * k, axis=-1, keepdims=True) + EPS))
        v_l.append(qkv[rows_of(ci), 2 * DN_WIDTH + hd * HEAD_DIM:2 * DN_WIDTH + (hd + 1) * HEAD_DIM])
    gcum_l = [cum[ci][:, hd:hd + 1] for ci, hd in probs]
    beta_l = [gb[rows_of(ci), DN_HEADS + hd:DN_HEADS + hd + 1] for ci, hd in probs]
    decay_l = [jnp.exp(jnp.where(causal, g - cum_t[ci][hd * c:(hd + 1) * c, :], -jnp.inf))
               for (ci, hd), g in zip(probs, gcum_l)]
    k16_l = [k.astype(_BF16) for k in k_l]
    kb_l = [k * b for k, b in zip(k_l, beta_l)]
    low_l = [jnp.where(strict, _dot_nt(kb.astype(_BF16), k16) * d, 0.0)
             for kb, k16, d in zip(kb_l, k16_l, decay_l)]
    attn16_l = [(_dot_nt(q.astype(_BF16), k16) * d).astype(_BF16)
                for q, k16, d in zip(q_l, k16_l, decay_l)]

    low2_l = [_split2(l) for l in low_l]
    x_l = [eye - l for l in low_l]
    m_l = [_dot3(l2, l2) for l2 in low2_l]
    n_fac = max(1, int(math.ceil(math.log2(c)))) - 1
    for f in range(n_fac):
        m2_l = [_split2(m) for m in m_l]
        x_l = [x + _dot3(_split2(x), m2) for x, m2 in zip(x_l, m2_l)]
        if f + 1 < n_fac:
            m_l = [_dot3(m2, m2) for m2 in m2_l]
    eg_l = [jnp.exp(g) for g in gcum_l]
    uw_l = [_dot3(_split2(x), _split2(jnp.concatenate([v * b, kb * eg], axis=1)))
            for x, v, b, kb, eg in zip(x_l, v_l, beta_l, kb_l, eg_l)]
    u_l = [uw[:, :HEAD_DIM] for uw in uw_l]
    w16_l = [uw[:, HEAD_DIM:].astype(_BF16) for uw in uw_l]
    qe16_l = [(q * eg).astype(_BF16) for q, eg in zip(q_l, eg_l)]
    glast_l = [g[c - 1:c, :] for g in gcum_l]
    kdec16_l = [(k * jnp.exp(gl - g)).astype(_BF16) for k, gl, g in zip(k_l, glast_l, gcum_l)]
    eglast_l = [jnp.exp(gl) for gl in glast_l]

    state = [s_scr[hd] for hd in heads]
    for ci in range(nc):
        p0 = ci * DN_HEADS
        s16 = [s.astype(_BF16) for s in state]
        vnew = [u_l[p0 + hd] - _dot(w16_l[p0 + hd], s16[hd]) for hd in heads]
        vnew16 = [v.astype(_BF16) for v in vnew]
        o = [_dot(qe16_l[p0 + hd], s16[hd]) + _dot(attn16_l[p0 + hd], vnew16[hd]) for hd in heads]
        state = [state[hd] * eglast_l[p0 + hd] + _dot_tn(kdec16_l[p0 + hd], vnew16[hd]) for hd in heads]
        for hd in heads:
            zz = z_ref[0, rows_of(ci), hd * HEAD_DIM:(hd + 1) * HEAD_DIM]
            od_ref[0, rows_of(ci), hd * HEAD_DIM:(hd + 1) * HEAD_DIM] = (
                _rms(o[hd], dnw_ref[...]) * _silu(zz)).astype(_BF16)
    for hd in heads:
        s_scr[hd] = state[hd]
    ssm_ref[0] = s_scr[...]


def _gdn(u, z, ab, conv0, ssm0, w_conv, alog_row, dtb_row, dn_w, *, lt, valid_len):
    B, L, _ = u.shape
    grid = (B, L // lt)
    tile = lambda b, i: (b, i, 0)
    perb3 = lambda b, i: (b, 0, 0)
    const = lambda b, i: (0, 0)
    return pl.pallas_call(
        functools.partial(_gdn_kernel, lt=lt, valid_len=valid_len),
        grid=grid,
        in_specs=[pl.BlockSpec((1, lt, CONV_DIM), tile), pl.BlockSpec((1, lt, DN_WIDTH), tile),
                  pl.BlockSpec((1, lt, LANES), tile), pl.BlockSpec((1, SUBLANES, CONV_DIM), perb3),
                  pl.BlockSpec((1, DN_HEADS, HEAD_DIM, HEAD_DIM), lambda b, i: (b, 0, 0, 0)),
                  pl.BlockSpec((CONV_W, CONV_DIM), const), pl.BlockSpec((1, LANES), const),
                  pl.BlockSpec((1, LANES), const), pl.BlockSpec((1, HEAD_DIM), const)],
        out_specs=[pl.BlockSpec((1, lt, DN_WIDTH), tile),
                   pl.BlockSpec((1, DN_HEADS, HEAD_DIM, HEAD_DIM), lambda b, i: (b, 0, 0, 0))],
        out_shape=[jax.ShapeDtypeStruct((B, L, DN_WIDTH), _BF16),
                   jax.ShapeDtypeStruct((B, DN_HEADS, HEAD_DIM, HEAD_DIM), _F32)],
        scratch_shapes=[pltpu.VMEM((lt + 2 * SUBLANES, CONV_DIM), _F32),
                        pltpu.VMEM((DN_HEADS, HEAD_DIM, HEAD_DIM), _F32)],
        compiler_params=_params("parallel", "arbitrary"),
        name="gdn",
    )(u, z, ab, conv0, ssm0, w_conv, alog_row, dtb_row, dn_w)


def _topk_rows(g, n_sel):
    r = g.shape[0]
    row = lax.broadcasted_iota(jnp.int32, g.shape, 0)
    sel = jnp.zeros(g.shape, _F32)
    for _ in range(n_sel):
        m = jnp.max(g, axis=0, keepdims=True)
        idx = jnp.min(jnp.where(g == m, row, r), axis=0, keepdims=True)
        pick = row == idx
        sel = jnp.where(pick, 1.0, sel)
        g = jnp.where(pick, -jnp.inf, g)
    return sel


def _moba_prompt_kernel(q_ref, k_ref, v_ref, c_ref, o_ref, ps_ref, kb_scr, vt_scr, kmean_scr, sel_scr,
                        *, nblk):
    qt = pl.program_id(2)
    blk = MOBA_BLOCK
    scale = HEAD_DIM ** -0.5

    ps_ref[...] = jnp.sum(c_ref[...], axis=1)

    @pl.when(qt == 0)
    def _():
        kmean_scr[...] = jnp.zeros(kmean_scr.shape, _F32)
        for j in range(nblk):
            kj = k_ref[0, 0, j * blk:(j + 1) * blk, :]
            kb_scr[j] = kj.astype(_BF16)
            kmean_scr[j:j + 1, :] = jnp.mean(kj, axis=0, keepdims=True)
            vt_scr[j] = v_ref[0, 0, j * blk:(j + 1) * blk, :].T.astype(_BF16)

    q = q_ref[0, 0]
    gate = _dot_nt(kmean_scr[...], q, _HI)
    brow = lax.broadcasted_iota(jnp.int32, gate.shape, 0)
    valid = brow < qt
    sel = _topk_rows(jnp.where(valid, gate, -jnp.inf), min(MOBA_TOPK, nblk))
    sel_scr[...] = jnp.where(valid, sel, 0.0)

    q16 = (q * scale).astype(_BF16)
    kpos = lax.broadcasted_iota(jnp.int32, (blk, blk), 0)
    qpos = lax.broadcasted_iota(jnp.int32, (blk, blk), 1)
    def scores(j):
        return jnp.where(sel_scr[pl.ds(j, 1), :] > 0.0, _dot_nt(kb_scr[j], q16), NEG_BIG)

    s = jnp.where(kpos <= qpos, _dot_nt(kb_scr[qt], q16), NEG_BIG)
    m = jnp.max(s, axis=0, keepdims=True)
    p = jnp.exp(s - m)
    l = jnp.sum(p, axis=0, keepdims=True)

    def body(j, carry):
        s_cur, p_prev, j_prev, m, l, acc = carry
        s_next = scores(j + 1)
        pv = _dot(vt_scr[j_prev], p_prev)
        m_new = jnp.maximum(m, jnp.max(s_cur, axis=0, keepdims=True))
        alpha = jnp.exp(m - m_new)
        p = jnp.exp(s_cur - m_new)
        l = alpha * l + jnp.sum(p, axis=0, keepdims=True)
        return s_next, p.astype(_BF16), j, m_new, l, alpha * (acc + pv)

    init = (scores(0), p.astype(_BF16), qt, m, l, jnp.zeros((HEAD_DIM, blk), _F32))
    _, p_prev, j_prev, m, l, acc = lax.fori_loop(0, qt, body, init)
    acc = acc + _dot(vt_scr[j_prev], p_prev)
    o_ref[0] = (acc / l).T.astype(_BF16)


def _moba_prompt(q, k, v, slabs):
    B, H, S, dh = q.shape
    nblk = S // MOBA_BLOCK
    nb8 = -(-nblk // SUBLANES) * SUBLANES
    assert S % MOBA_BLOCK == 0
    n_slabs, ps, _ = slabs.shape
    steps = B * H * nblk
    share = n_slabs // steps
    assert share * steps == n_slabs and share % SUBLANES == 0, "key cache does not split evenly over the grid"
    full = lambda b, h, t: (b, h, 0, 0)
    flat = lambda b, h, t: (b * H + h) * nblk + t
    return pl.pallas_call(
        functools.partial(_moba_prompt_kernel, nblk=nblk),
        grid=(B, H, nblk),
        in_specs=[pl.BlockSpec((1, 1, MOBA_BLOCK, dh), lambda b, h, t: (b, h, t, 0)),
                  pl.BlockSpec((1, 1, S, dh), full), pl.BlockSpec((1, 1, S, dh), full),
                  pl.BlockSpec((share, ps, dh), lambda b, h, t: (flat(b, h, t), 0, 0))],
        out_specs=[pl.BlockSpec((1, MOBA_BLOCK, dh), lambda b, h, t: (b, t, h)),
                   pl.BlockSpec((share, dh), lambda b, h, t: (flat(b, h, t), 0))],
        out_shape=[jax.ShapeDtypeStruct((B, S, H * dh), _BF16),
                   jax.ShapeDtypeStruct((n_slabs, dh), _F32)],
        scratch_shapes=[pltpu.VMEM((nblk, MOBA_BLOCK, dh), _BF16),
                        pltpu.VMEM((nblk, dh, MOBA_BLOCK), _BF16),
                        pltpu.VMEM((nb8, dh), _F32),
                        pltpu.VMEM((nb8, MOBA_BLOCK), _F32)],
        compiler_params=_params("parallel", "parallel", "arbitrary"),
        name="moba_prompt",
    )(q, k, v, slabs)


def _moba_sel_kernel(pt_ref, psum_ref, q_ref, sel_ref, km_scr, *, n_pages, ppb, n_blk, n_sel, rows_per_blk):
    b = pl.program_id(0)
    km_scr[...] = jnp.zeros(km_scr.shape, _F32)

    def fill(j, carry):
        acc = psum_ref[pl.ds(pt_ref[b * n_pages + j * ppb], 1), :]
        for r in range(1, ppb):
            acc = acc + psum_ref[pl.ds(pt_ref[b * n_pages + j * ppb + r], 1), :]
        km_scr[pl.ds(j, 1), :] = acc * (1.0 / rows_per_blk)
        return carry

    lax.fori_loop(0, n_blk, fill, 0)
    lane = lax.broadcasted_iota(jnp.int32, (SUBLANES, LANES), 1)
    for hd in range(MOBA_HEADS):
        g = _dot_nt(q_ref[0, hd], km_scr[:, hd * HEAD_DIM:(hd + 1) * HEAD_DIM], _HI)
        g = jnp.where(lane < n_blk, g, -jnp.inf)
        out = jnp.zeros((SUBLANES, LANES), jnp.int32)
        for r in range(n_sel):
            m = jnp.max(g, axis=1, keepdims=True)
            idx = jnp.min(jnp.where(g == m, lane, LANES), axis=1, keepdims=True)
            out = jnp.where(lane == r, idx, out)
            g = jnp.where(lane == idx, -jnp.inf, g)
        sel_ref[0, hd] = out


def _moba_sel(page_table, psum, q8, *, ppb, n_sel, rows_per_blk):
    DB, n_pages = page_table.shape
    n_blk = n_pages // ppb
    assert n_blk <= LANES
    n_phys, width = psum.shape
    return pl.pallas_call(
        functools.partial(_moba_sel_kernel, n_pages=n_pages, ppb=ppb, n_blk=n_blk, n_sel=n_sel,
                          rows_per_blk=rows_per_blk),
        grid_spec=pltpu.PrefetchScalarGridSpec(
            num_scalar_prefetch=1,
            grid=(DB,),
            in_specs=[pl.BlockSpec((n_phys, width), lambda b, pt: (0, 0)),
                      pl.BlockSpec((1, MOBA_HEADS, SUBLANES, HEAD_DIM), lambda b, pt: (b, 0, 0, 0))],
            out_specs=pl.BlockSpec((1, MOBA_HEADS, SUBLANES, LANES), lambda b, pt: (b, 0, 0, 0)),
            scratch_shapes=[pltpu.VMEM((LANES, width), _F32)]),
        out_shape=jax.ShapeDtypeStruct((DB, MOBA_HEADS, SUBLANES, LANES), jnp.int32),
        compiler_params=_params("arbitrary"),
        name="moba_sel",
    )(page_table.reshape(-1), psum, q8)


def _moba_sample_kernel(phys_ref, q_ref, kn_ref, vn_ref, ck_hbm, cv_hbm, o_ref, kbuf, vbuf, sem,
                        *, n_pg, n_new, n_heads, n_steps):
    step = pl.program_id(0)
    per = n_new * n_pg
    scale = HEAD_DIM ** -0.5

    def page_copies(st, slot):
        hd = st % n_heads
        cps = []
        for j in range(per):
            page = phys_ref[st * per + j]
            cps.append(pltpu.make_async_copy(ck_hbm.at[page, hd], kbuf.at[slot, j], sem.at[slot]))
            cps.append(pltpu.make_async_copy(cv_hbm.at[page, hd], vbuf.at[slot, j], sem.at[slot]))
        return cps

    def attend(slot):
        q = q_ref[0, 0]
        q16 = q.astype(_BF16)
        kn = kn_ref[0, 0]
        vn = vn_ref[0, 0]
        rowq = lax.broadcasted_iota(jnp.int32, (SUBLANES, 1), 0)
        s_own = []
        for c in range(n_new):
            sc = jnp.sum(q * kn[c:c + 1, :], axis=-1, keepdims=True) * scale
            s_own.append(jnp.where(rowq >= c, sc, NEG_BIG))
        m_own = s_own[0]
        for sc in s_own[1:]:
            m_own = jnp.maximum(m_own, sc)
        out = jnp.zeros((SUBLANES, HEAD_DIM), _F32)
        for t in range(n_new):
            pages = range(t * n_pg, (t + 1) * n_pg)
            s_past = [_dot_nt(q16, kbuf[slot, j].astype(_BF16)) * scale for j in pages]
            m = m_own
            for sp in s_past:
                m = jnp.maximum(m, jnp.max(sp, axis=-1, keepdims=True))
            l = jnp.zeros((SUBLANES, 1), _F32)
            acc = jnp.zeros((SUBLANES, HEAD_DIM), _F32)
            for sp, j in zip(s_past, pages):
                p = jnp.exp(sp - m)
                l = l + jnp.sum(p, axis=-1, keepdims=True)
                acc = acc + _dot(p.astype(_BF16), vbuf[slot, j].astype(_BF16))
            for c, sc in enumerate(s_own):
                p = jnp.exp(sc - m)
                l = l + p
                acc = acc + p * vn[c:c + 1, :]
            out = jnp.where(rowq == t, acc / l, out)
        o_ref[0, 0] = out

    def run(slot):
        if slot == 0:
            @pl.when(step == 0)
            def _():
                for cp in page_copies(0, 0):
                    cp.start()

        @pl.when(step + 1 < n_steps)
        def _():
            for cp in page_copies(step + 1, 1 - slot):
                cp.start()

        for cp in page_copies(step, slot):
            cp.wait()
        attend(slot)

    for slot in range(2):
        @pl.when(step % 2 == slot)
        def _():
            run(slot)


def _moba_sample(phys, q8, kn8, vn8, cache_k, cache_v, *, n_new):
    DB, H, _, dh = q8.shape
    ps = cache_k.shape[2]
    n_pg = phys.shape[0] // (DB * H * n_new)
    steps = DB * H
    small = pl.BlockSpec((1, 1, SUBLANES, dh), lambda s, ph: (s // H, s % H, 0, 0))
    return pl.pallas_call(
        functools.partial(_moba_sample_kernel, n_pg=n_pg, n_new=n_new, n_heads=H, n_steps=steps),
        grid_spec=pltpu.PrefetchScalarGridSpec(
            num_scalar_prefetch=1,
            grid=(steps,),
            in_specs=[small, small, small, pl.BlockSpec(memory_space=pl.ANY),
                      pl.BlockSpec(memory_space=pl.ANY)],
            out_specs=small,
            scratch_shapes=[pltpu.VMEM((2, n_new * n_pg, ps, dh), _F32),
                            pltpu.VMEM((2, n_new * n_pg, ps, dh), _F32),
                            pltpu.SemaphoreType.DMA((2,))]),
        out_shape=jax.ShapeDtypeStruct((DB, H, SUBLANES, dh), _F32),
        compiler_params=_params("arbitrary"),
        name="moba_sample",
    )(phys, q8, kn8, vn8, cache_k, cache_v)


def _post_kernel(xp_ref, odp_ref, omp_ref, xs_ref, ods_ref, oms_ref, wo1_ref, wo2_ref, ln_ref,
                 wrh_ref, wrl_ref, br_ref, x1_ref, xn_ref, route_ref, cnt_ref, carry_scr,
                 *, tm, n_prompt_tiles, n_tokens):
    i = pl.program_id(0)

    @pl.when(i == 0)
    def _():
        carry_scr[...] = jnp.zeros(carry_scr.shape, _F32)

    is_p = i < n_prompt_tiles
    x = jnp.where(is_p, xp_ref[...], xs_ref[...])
    od = jnp.where(is_p, odp_ref[...], ods_ref[...])
    om = jnp.where(is_p, omp_ref[...], oms_ref[...])
    x1 = x + _dot(od, wo1_ref[...]) + _dot(om, wo2_ref[...])
    _store_token_tiles(x1_ref, x1)
    xn = _rms(x1, ln_ref[...])
    _store_token_tiles(xn_ref, xn)
    xh, xl = _split2(xn)
    logits = (_dot_nt(wrh_ref[...], xh) + _dot_nt(wrh_ref[...], xl) + _dot_nt(wrl_ref[...], xh)) + br_ref[...]

    erow = lax.broadcasted_iota(jnp.int32, (N_EXPERTS, tm), 0)
    g = logits
    vals, picks = [], []
    for _ in range(TOP_K):
        m = jnp.max(g, axis=0, keepdims=True)
        idx = jnp.min(jnp.where(g == m, erow, N_EXPERTS), axis=0, keepdims=True)
        pick = erow == idx
        vals.append(m)
        picks.append((idx, pick))
        g = jnp.where(pick, -jnp.inf, g)
    es = [jnp.exp(v - vals[0]) for v in vals]
    den = es[0]
    for e in es[1:]:
        den = den + e

    rows = lax.broadcasted_iota(jnp.int32, (tm, tm), 0)
    cols = lax.broadcasted_iota(jnp.int32, (tm, tm), 1)
    earlier = jnp.where(rows < cols, 1.0, 0.0).astype(_BF16)
    real = jnp.where((i * tm + lax.broadcasted_iota(jnp.int32, (1, tm), 1)) < n_tokens, 1.0, 0.0)
    base = carry_scr[...]
    tok = lax.broadcasted_iota(jnp.int32, (1, tm), 1)
    ranks = [jnp.zeros((1, tm), _F32) for _ in picks]
    for sub in range(tm // TOK_TILE):
        in_sub = jnp.where((tok >= sub * TOK_TILE) & (tok < (sub + 1) * TOK_TILE), real, 0.0)
        for k, (idx, pick) in enumerate(picks):
            onehot = jnp.where(pick, in_sub, 0.0)
            pref = _dot(onehot.astype(_BF16), earlier) + base
            ranks[k] = ranks[k] + jnp.sum(onehot * pref, axis=0, keepdims=True)
            base = base + jnp.sum(onehot, axis=1, keepdims=True)
    carry_scr[...] = base
    cnt_ref[...] = base
    record = ([idx.astype(_F32) for idx, _ in picks] + [e / den for e in es] + ranks
              + [jnp.zeros((route_ref.shape[0] - 3 * TOP_K, tm), _F32)])
    route_ref[...] = jnp.concatenate(record, axis=0)


def _post(xp, odp, omp, xs, ods, oms, wo1, wo2, ln_w, wr_hi, wr_lo, b_r, *, tm, n_tokens):
    Tp, D = xp.shape
    npt = Tp // tm
    T = Tp + tm
    prow = lambda t: (jnp.minimum(t, npt - 1), 0)
    srow = lambda t: (0, 0)
    row = lambda t: (t, 0)
    const = lambda t: (0, 0)
    return pl.pallas_call(
        functools.partial(_post_kernel, tm=tm, n_prompt_tiles=npt, n_tokens=n_tokens),
        grid=(npt + 1,),
        in_specs=[pl.BlockSpec((tm, D), prow), pl.BlockSpec((tm, DN_WIDTH), prow),
                  pl.BlockSpec((tm, ATT_WIDTH), prow),
                  pl.BlockSpec((tm, D), srow), pl.BlockSpec((tm, DN_WIDTH), srow),
                  pl.BlockSpec((tm, ATT_WIDTH), srow),
                  pl.BlockSpec((DN_WIDTH, D), const), pl.BlockSpec((ATT_WIDTH, D), const),
                  pl.BlockSpec((1, D), const), pl.BlockSpec((N_EXPERTS, D), const),
                  pl.BlockSpec((N_EXPERTS, D), const), pl.BlockSpec((N_EXPERTS, 1), const)],
        out_specs=[pl.BlockSpec((tm * SUBLANES, LANES), row), pl.BlockSpec((tm * SUBLANES, LANES), row),
                   pl.BlockSpec((ROUTE_ROWS, tm), lambda t: (0, t)), pl.BlockSpec((N_EXPERTS, 1), const)],
        out_shape=[jax.ShapeDtypeStruct((T * SUBLANES, LANES), _F32),
                   jax.ShapeDtypeStruct((T * SUBLANES, LANES), _F32),
                   jax.ShapeDtypeStruct((ROUTE_ROWS, T), _F32), jax.ShapeDtypeStruct((N_EXPERTS, 1), _F32)],
        scratch_shapes=[pltpu.VMEM((N_EXPERTS, 1), _F32)],
        compiler_params=_params("arbitrary"),
        name="post",
    )(xp, odp, omp, xs, ods, oms, wo1, wo2, ln_w, wr_hi, wr_lo, b_r)


RUN_PIECES = tuple(1 << b for b in reversed(range(TOK_TILE.bit_length())))
RUN_RARE = 64


def _all_run_copies(make_copy, start_ref, len_ref, tile):
    runs = []
    pool_start = jnp.int32(0)
    for e in range(N_EXPERTS):
        n = len_ref[tile * N_EXPERTS + e]
        first = start_ref[tile * N_EXPERTS + e]
        pieces = []
        for piece in RUN_PIECES:
            done = n & (-2 * piece)
            pieces.append((piece, make_copy(first + done, pool_start + done, piece)))
        runs.append((n, pieces))
        pool_start = pool_start + n
    return runs


def _for_pieces(runs, act):
    for n, pieces in runs:
        def some(which):
            for piece, cp in pieces:
                if which(piece):
                    @pl.when((n & piece) != 0)
                    def _():
                        act(cp)

        @pl.when(n >= RUN_RARE)
        def _():
            some(lambda piece: piece >= RUN_RARE)

        some(lambda piece: piece < RUN_RARE)


def _start_all(runs):
    _for_pieces(runs, lambda cp: cp.start())


def _wait_all(runs):
    _for_pieces(runs, lambda cp: cp.wait())


def _rows(ref, first_row, n_rows):
    return ref.at[pl.ds(pl.multiple_of(first_row * SUBLANES, SUBLANES), n_rows * SUBLANES)]


def _dispatch_kernel(start_ref, len_ref, tail_ref, nu_ref, slot_hbm, xn_ref, xs_hbm, idx_smem, pool, zero_scr,
                     sem_idx, sem_zero, sem_rows, *, tm, n_blocks):
    i = pl.program_id(0)
    n_asg = tm * TOP_K
    idx_copy = pltpu.make_async_copy(slot_hbm.at[pl.ds(i * n_asg, n_asg)], idx_smem, sem_idx)
    idx_copy.start()

    @pl.when(i == 0)
    def _():
        zero_scr[...] = jnp.zeros(zero_scr.shape, _F32)

        def clear(first_row):
            first = pl.multiple_of(first_row * SUBLANES, MOE_ROWS * SUBLANES)
            cp = pltpu.make_async_copy(zero_scr, xs_hbm.at[pl.ds(first, MOE_ROWS * SUBLANES)], sem_zero)
            cp.start()
            cp.wait()

        for e in range(N_EXPERTS):
            @pl.when(tail_ref[e] >= 0)
            def _():
                clear(tail_ref[e])

        def clear_block(b, carry):
            clear(b * MOE_ROWS)
            return carry

        lax.fori_loop(nu_ref[0], n_blocks, clear_block, 0)

    idx_copy.wait()

    def pack(t, carry):
        tile = xn_ref[pl.ds(pl.multiple_of(t * SUBLANES, SUBLANES), SUBLANES), :]
        for k in range(TOP_K):
            row = pl.multiple_of(idx_smem[k * tm + t] * SUBLANES, SUBLANES)
            pool[pl.ds(row, SUBLANES), :] = tile
        return carry

    lax.fori_loop(0, tm, pack, 0, unroll=4)

    copies = _all_run_copies(
        lambda hbm_row, pool_row, n: pltpu.make_async_copy(_rows(pool, pool_row, n), _rows(xs_hbm, hbm_row, n),
                                                           sem_rows),
        start_ref, len_ref, i)
    _start_all(copies)
    _wait_all(copies)


def _dispatch(run_start, run_len, tail, n_used, slot, xn_tiles, n_tiles, n_blocks, *, tm):
    pre = lambda i, *_: (i, 0)
    return pl.pallas_call(
        functools.partial(_dispatch_kernel, tm=tm, n_blocks=n_blocks),
        grid_spec=pltpu.PrefetchScalarGridSpec(
            num_scalar_prefetch=4,
            grid=(n_tiles,),
            in_specs=[pl.BlockSpec(memory_space=pl.ANY), pl.BlockSpec((tm * SUBLANES, LANES), pre)],
            out_specs=pl.BlockSpec(memory_space=pl.ANY),
            scratch_shapes=[pltpu.SMEM((tm * TOP_K,), jnp.int32),
                            pltpu.VMEM(((tm * TOP_K + 1) * SUBLANES, LANES), _F32),
                            pltpu.VMEM((MOE_ROWS * SUBLANES, LANES), _F32),
                            pltpu.SemaphoreType.DMA, pltpu.SemaphoreType.DMA, pltpu.SemaphoreType.DMA]),
        out_shape=jax.ShapeDtypeStruct((n_blocks * MOE_ROWS * SUBLANES, LANES), _F32),
        compiler_params=_params("arbitrary"),
        name="dispatch",
    )(run_start, run_len, tail, n_used, slot, xn_tiles)


def _moe_kernel(be_ref, nu_ref, x_ref, wg_ref, bg_ref, wu_ref, bu_ref, wd_ref, bd_ref, o_ref,
                wg_scr, wu_scr, wd_scr, x_scr):
    i = pl.program_id(0)

    @pl.when(i >= nu_ref[0])
    def _():
        o_ref[...] = jnp.zeros(o_ref.shape, _F32)

    @pl.when(i < nu_ref[0])
    def _():
        prev = be_ref[jnp.maximum(i - 1, 0)]

        @pl.when((i == 0) | (be_ref[i] != prev))
        def _():
            wg_scr[...] = wg_ref[0].astype(_BF16)
            wu_scr[...] = wu_ref[0].astype(_BF16)
            wd_scr[...] = wd_ref[0].astype(_BF16)

        for j in range(SUBLANES):
            x_scr[:, j * LANES:(j + 1) * LANES] = _load_token_tiles(x_ref, MOE_ROWS, j).astype(_BF16)
        x = x_scr[...]
        gt = jnp.minimum(_dot(x, wg_scr[...]) + bg_ref[0], SWIGLU_LIMIT)
        up = jnp.clip(_dot(x, wu_scr[...]) + bu_ref[0], -SWIGLU_LIMIT, SWIGLU_LIMIT)
        act = ((up + 1.0) * (gt * jax.nn.sigmoid(SWIGLU_ALPHA * gt))).astype(_BF16)
        _store_token_tiles(o_ref, _dot(act, wd_scr[...]) + bd_ref[0])


def _moe_ffn(block_e, n_used, xs, n_blocks, w_gate, b_gate, w_up, b_up, w_down, b_down):
    E, D, F = w_gate.shape
    assert D == SUBLANES * LANES
    rows = MOE_ROWS * SUBLANES
    xblk = lambda i, be, nu: (jnp.minimum(i, nu[0] - 1), 0)
    wsel = lambda i, be, nu: (be[jnp.minimum(i, nu[0] - 1)], 0, 0)
    return pl.pallas_call(
        _moe_kernel,
        grid_spec=pltpu.PrefetchScalarGridSpec(
            num_scalar_prefetch=2,
            grid=(n_blocks,),
            in_specs=[pl.BlockSpec((rows, LANES), xblk),
                      pl.BlockSpec((1, D, F), wsel), pl.BlockSpec((1, 1, F), wsel),
                      pl.BlockSpec((1, D, F), wsel), pl.BlockSpec((1, 1, F), wsel),
                      pl.BlockSpec((1, F, D), wsel), pl.BlockSpec((1, 1, D), wsel)],
            out_specs=pl.BlockSpec((rows, LANES), lambda i, be, nu: (i, 0)),
            scratch_shapes=[pltpu.VMEM((D, F), _BF16), pltpu.VMEM((D, F), _BF16),
                            pltpu.VMEM((F, D), _BF16), pltpu.VMEM((MOE_ROWS, D), _BF16)]),
        out_shape=jax.ShapeDtypeStruct((n_blocks * rows, LANES), _F32),
        compiler_params=_params("arbitrary"),
        name="moe_ffn",
    )(block_e, n_used, xs, w_gate, b_gate.reshape(E, 1, F), w_up, b_up.reshape(E, 1, F),
      w_down, b_down.reshape(E, 1, D))


def _combine_kernel(start_ref, len_ref, slot_hbm, gate_hbm, ys_hbm, x1_ref, yp_ref, yr_ref,
                    idx_smem0, idx_smem1, gate_smem0, gate_smem1, pool, out_scr, sem_idx, sem_rows,
                    *, tm, n_prompt_tiles, n_tiles):
    i = pl.program_id(0)
    n_asg = tm * TOP_K
    idx_smem = (idx_smem0, idx_smem1)
    gate_smem = (gate_smem0, gate_smem1)

    def run_copies(tile, slot):
        return _all_run_copies(
            lambda hbm_row, pool_row, n: pltpu.make_async_copy(
                _rows(ys_hbm, hbm_row, n), _rows(pool.at[slot], pool_row, n), sem_rows.at[slot]),
            start_ref, len_ref, tile)

    def fetch(tile, slot):
        cps = [pltpu.make_async_copy(slot_hbm.at[pl.ds(tile * n_asg, n_asg)], idx_smem[slot], sem_idx.at[0]),
               pltpu.make_async_copy(gate_hbm.at[pl.ds(tile * n_asg, n_asg)], gate_smem[slot], sem_idx.at[1])]
        for cp in cps:
            cp.start()
        _start_all(run_copies(tile, slot))
        for cp in cps:
            cp.wait()

    def step(slot):
        if slot == 0:
            @pl.when(i == 0)
            def _():
                fetch(0, 0)

        @pl.when(i + 1 < n_tiles)
        def _():
            fetch(i + 1, 1 - slot)

        _wait_all(run_copies(i, slot))

        def token(t, carry):
            at = pl.ds(pl.multiple_of(t * SUBLANES, SUBLANES), SUBLANES)
            acc = x1_ref[at, :]
            for k in range(TOP_K):
                row = pl.multiple_of(idx_smem[slot][k * tm + t] * SUBLANES, SUBLANES)
                acc = acc + gate_smem[slot][k * tm + t] * pool[slot, pl.ds(row, SUBLANES), :]
            out_scr[at, :] = acc
            return carry

        lax.fori_loop(0, tm, token, 0, unroll=4)

        def emit(o_ref):
            for j in range(SUBLANES):
                o_ref[:, j * LANES:(j + 1) * LANES] = _load_token_tiles(out_scr, tm, j)

        @pl.when(i < n_prompt_tiles)
        def _():
            emit(yp_ref)

        @pl.when(i >= n_prompt_tiles)
        def _():
            emit(yr_ref)

    for slot in range(2):
        @pl.when(i % 2 == slot)
        def _():
            step(slot)


def _combine(run_start, run_len, slot, gates, ys, x1_tiles, n_prompt, n_tiles, *, tm):
    D = SUBLANES * LANES
    npt = n_prompt // tm
    pre = lambda i, *_: (i, 0)
    return pl.pallas_call(
        functools.partial(_combine_kernel, tm=tm, n_prompt_tiles=npt, n_tiles=n_tiles),
        grid_spec=pltpu.PrefetchScalarGridSpec(
            num_scalar_prefetch=2,
            grid=(n_tiles,),
            in_specs=[pl.BlockSpec(memory_space=pl.ANY), pl.BlockSpec(memory_space=pl.ANY),
                      pl.BlockSpec(memory_space=pl.ANY), pl.BlockSpec((tm * SUBLANES, LANES), pre)],
            out_specs=[pl.BlockSpec((tm, D), lambda i, *_: (jnp.minimum(i, npt - 1), 0)),
                       pl.BlockSpec((tm, D), lambda i, *_: (jnp.maximum(i - npt, 0), 0))],
            scratch_shapes=[pltpu.SMEM((tm * TOP_K,), jnp.int32), pltpu.SMEM((tm * TOP_K,), jnp.int32),
                            pltpu.SMEM((tm * TOP_K,), _F32), pltpu.SMEM((tm * TOP_K,), _F32),
                            pltpu.VMEM((2, tm * TOP_K * SUBLANES, LANES), _F32),
                            pltpu.VMEM((tm * SUBLANES, LANES), _F32),
                            pltpu.SemaphoreType.DMA((2,)), pltpu.SemaphoreType.DMA((2,))]),
        out_shape=[jax.ShapeDtypeStruct((n_prompt, D), _F32),
                   jax.ShapeDtypeStruct(((n_tiles - npt) * tm, D), _F32)],
        compiler_params=_params("arbitrary"),
        name="combine",
    )(run_start, run_len, slot, gates, ys, x1_tiles)


def _pad_lanes(v, fill=0.0):
    v = v.reshape(1, -1).astype(_F32)
    return jnp.pad(v, ((0, 0), (0, LANES - v.shape[1])), constant_values=fill)


def _pad_rows(a, rows):
    return jnp.pad(a, ((0, rows - a.shape[0]), (0, 0)))


def _prep_w_in(w_in):
    off_a = CONV_DIM + DN_WIDTH
    off_mq = off_a + 2 * DN_HEADS
    ab = jnp.pad(w_in[:, off_a:off_mq], ((0, 0), (0, LANES - 2 * DN_HEADS)))
    return jnp.concatenate([w_in[:, :off_a], w_in[:, off_mq:], ab], axis=1).astype(_BF16)


def _moe(xn_tiles, x1_tiles, route_t, counts, n_prompt, n_tokens, w_gate, b_gate, w_up, b_up, w_down, b_down):
    tm = TOK_TILE
    n_tiles = -(-n_tokens // tm)
    tc = n_tiles * tm
    idx = route_t[0:TOP_K, :tc].astype(jnp.int32)
    gates = route_t[TOP_K:2 * TOP_K, :tc]
    rank = route_t[2 * TOP_K:3 * TOP_K, :tc].astype(jnp.int32)
    cnt = counts[:, 0].astype(jnp.int32)
    padded = (cnt + MOE_ROWS - 1) // MOE_ROWS * MOE_ROWS
    pends = jnp.cumsum(padded)
    pstarts = pends - padded
    n_blocks = -(-n_tokens * TOP_K // MOE_ROWS) + N_EXPERTS
    real = (jnp.arange(tc, dtype=jnp.int32) < n_tokens)[None, :]
    experts = jnp.arange(N_EXPERTS, dtype=jnp.int32)[:, None, None]
    onehot = (idx[None] == experts) & real[None]
    hist = jnp.sum(onehot.reshape(N_EXPERTS, TOP_K, n_tiles, tm), axis=(1, 3), dtype=jnp.int32).T
    before_tile = jnp.cumsum(hist, axis=0) - hist
    before_expert = jnp.cumsum(hist, axis=1) - hist
    run_start = (pstarts[None, :] + before_tile).reshape(-1)
    shift = jnp.repeat((before_tile - before_expert).T, tm, axis=1)
    slot = rank - jnp.sum(jnp.where(onehot, shift[:, None, :], 0), axis=0)
    per_tile = lambda d: d.reshape(TOP_K, n_tiles, tm).transpose(1, 0, 2).reshape(-1)
    slot_pack = per_tile(jnp.where(real, slot, tm * TOP_K))
    slot_read = per_tile(jnp.where(real, slot, 0))
    block_start = jnp.arange(n_blocks, dtype=jnp.int32) * MOE_ROWS
    block_e = jnp.minimum(jnp.sum((pends[None, :] <= block_start[:, None]).astype(jnp.int32), axis=1),
                          N_EXPERTS - 1)
    n_used = (pends[-1:] // MOE_ROWS).astype(jnp.int32)
    tail = jnp.where(cnt > 0, pends - MOE_ROWS, -1).astype(jnp.int32)
    run_len = hist.reshape(-1)
    xs = _dispatch(run_start, run_len, tail, n_used, slot_pack, xn_tiles, n_tiles, n_blocks, tm=tm)
    ys = _moe_ffn(block_e, n_used, xs, n_blocks, w_gate, b_gate, w_up, b_up, w_down, b_down)
    return _combine(run_start, run_len, slot_read, per_tile(gates), ys, x1_tiles, n_prompt, n_tiles, tm=tm)


def _layer(xp, xs_, cache_k, cache_v, conv_s, ssm_s, page_table, lw):
    (ln1_w, w_in, w_conv, a_log, dt_bias, dn_norm_w, q_norm_w, k_norm_w, w_o, ln2_w, w_router,
     b_router, w_gate, b_gate, w_up, b_up, w_down, b_down) = lw
    B, S, D = xp.shape
    DB, L, _ = xs_.shape
    n_phys, H, PS, dh = cache_k.shape
    n_pages = page_table.shape[1]
    ppb = MOBA_BLOCK // PS
    Tp, Ts = B * S, DB * L
    assert (n_pages * PS) % MOBA_BLOCK == 0, "paged past must end on a MoBA block boundary"
    assert CONV_W - 1 <= L <= SUBLANES and Ts <= POST_ROWS and POST_ROWS % TOK_TILE == 0
    assert Tp % PROJ_ROWS == 0 and S % GDN_ROWS == 0 and Tp % POST_ROWS == 0
    cur = n_pages // ppb
    n_sel = min(MOBA_TOPK, cur)
    assert n_sel > 0

    w_all = _prep_w_in(w_in)
    ln1 = ln1_w.reshape(1, D)
    qn = q_norm_w.reshape(1, dh)
    kn = k_norm_w.reshape(1, dh)
    alog_row = _pad_lanes(a_log)
    dtb_row = _pad_lanes(dt_bias)
    dnw = dn_norm_w.reshape(1, dh)

    u_p, z_p, ab_p, mq_p, mk_p, mv_p = _proj(xp.reshape(Tp, D), ln1, w_all, qn, kn, tm=PROJ_ROWS, seq=(B, S))
    od_p, ssm_p = _gdn(u_p.reshape(B, S, CONV_DIM), z_p.reshape(B, S, DN_WIDTH), ab_p.reshape(B, S, LANES),
                       jnp.zeros((B, SUBLANES, CONV_DIM), _F32), jnp.zeros((B, DN_HEADS, dh, dh), _F32),
                       w_conv, alog_row, dtb_row, dnw, lt=GDN_ROWS, valid_len=GDN_ROWS)
    om_p, psum = _moba_prompt(mq_p, mk_p, mv_p, cache_k.reshape(n_phys * H, PS, dh))
    conv_p = u_p.reshape(B, S, CONV_DIM)[:, S - (CONV_W - 1):]

    u_s, z_s, ab_s, mq_s, mk_s, mv_s = _proj(xs_.reshape(Ts, D), ln1, w_all, qn, kn, tm=Ts)
    padl = lambda a: jnp.pad(a.reshape(DB, L, -1), ((0, 0), (0, DN_CHUNK - L), (0, 0)))
    conv0 = jnp.pad(conv_s, ((0, 0), (SUBLANES - (CONV_W - 1), 0), (0, 0)))
    od_s, ssm_s_new = _gdn(padl(u_s), padl(z_s), padl(ab_s), conv0, ssm_s, w_conv, alog_row, dtb_row, dnw,
                           lt=DN_CHUNK, valid_len=L)
    od_s = od_s[:, :L].reshape(Ts, DN_WIDTH)
    conv_s_new = jnp.concatenate([conv_s, u_s.reshape(DB, L, CONV_DIM)], axis=1)[:, L:]

    heads = lambda a: a.reshape(DB, L, H, dh).transpose(0, 2, 1, 3)
    pad8 = lambda a: jnp.pad(a, ((0, 0), (0, 0), (0, SUBLANES - L), (0, 0)))
    q8, k8, v8 = pad8(heads(mq_s)), pad8(heads(mk_s)), pad8(heads(mv_s))
    psum = psum.reshape(n_phys, H * dh)
    sel = _moba_sel(page_table, psum, q8, ppb=ppb, n_sel=n_sel, rows_per_blk=MOBA_BLOCK)
    sel = sel[:, :, :L, :n_sel]
    logical = sel[..., None] * ppb + jnp.arange(ppb, dtype=jnp.int32)
    phys = page_table[jnp.arange(DB)[:, None, None, None, None], logical]
    om_s = _moba_sample(phys.reshape(-1).astype(jnp.int32), q8, k8, v8, cache_k, cache_v, n_new=L)
    om_s = om_s[:, :, :L].transpose(0, 2, 1, 3).reshape(Ts, H * dh).astype(_BF16)

    wo = w_o.astype(_BF16)
    wr = w_router.T
    wr_hi = wr.astype(_BF16)
    wr_lo = (wr - wr_hi.astype(_F32)).astype(_BF16)
    x1_tiles, xn_tiles, route_t, cnt = _post(
        xp.reshape(Tp, D), od_p.reshape(Tp, DN_WIDTH), om_p.reshape(Tp, ATT_WIDTH),
        _pad_rows(xs_.reshape(Ts, D), POST_ROWS), _pad_rows(od_s, POST_ROWS), _pad_rows(om_s, POST_ROWS),
        wo[:DN_WIDTH], wo[DN_WIDTH:], ln2_w.reshape(1, D), wr_hi, wr_lo, b_router.reshape(N_EXPERTS, 1),
        tm=POST_ROWS, n_tokens=Tp + Ts)
    y_p, y_s = _moe(xn_tiles, x1_tiles, route_t, cnt, Tp, Tp + Ts, w_gate, b_gate, w_up, b_up, w_down, b_down)
    return (y_p.reshape(B, S, D), y_s[:Ts].reshape(DB, L, D), mk_p, mv_p, conv_p, ssm_p,
            heads(mk_s), heads(mv_s), conv_s_new, ssm_s_new)


def kernel(x_prompt, x_sample, cache_k, cache_v, state_conv, state_ssm, page_table, ln1_w, w_in, w_conv,
           a_log, dt_bias, dn_norm_w, q_norm_w, k_norm_w, w_o, ln2_w, w_router, b_router, w_gate, b_gate,
           w_up, b_up, w_down, b_down):
    weights = (ln1_w, w_in, w_conv, a_log, dt_bias, dn_norm_w, q_norm_w, k_norm_w, w_o, ln2_w, w_router,
               b_router, w_gate, b_gate, w_up, b_up, w_down, b_down)
    depth = w_in.shape[0]
    yp, ys = x_prompt, x_sample
    outs = [[] for _ in range(8)]
    for l in range(depth):
        res = _layer(yp, ys, cache_k[l], cache_v[l], state_conv[l], state_ssm[l], page_table,
                     tuple(w[l] for w in weights))
        yp, ys = res[0], res[1]
        for acc, r in zip(outs, res[2:]):
            acc.append(r)
    return (yp, ys) + tuple(jnp.stack(o) for o in outs)
```

```python
import functools
import math

import jax
import jax.numpy as jnp
from jax import lax
from jax.experimental import pallas as pl
from jax.experimental.pallas import tpu as pltpu

HEAD_DIM = 128
DN_HEADS = 4
MOBA_HEADS = 4
DN_WIDTH = DN_HEADS * HEAD_DIM
ATT_WIDTH = MOBA_HEADS * HEAD_DIM
CONV_W = 4
CONV_DIM = 3 * DN_WIDTH
DN_CHUNK = 64
MOBA_BLOCK = 256
MOBA_TOPK = 3
N_EXPERTS = 32
TOP_K = 4
SWIGLU_LIMIT = 7.0
SWIGLU_ALPHA = 1.702
EPS = 1e-6

LANES = 128
SUBLANES = 8
VMEM_LIMIT = 56 * 1024 * 1024
NEG_BIG = -1e30

PROJ_ROWS = 512
GDN_ROWS = 256
POST_ROWS = 512
TOK_TILE = 256
MOE_ROWS = 512
ROUTE_ROWS = 16

_HI = lax.Precision.HIGHEST
_F32 = jnp.float32
_BF16 = jnp.bfloat16


def _dot(a, b, precision=None):
    return jnp.dot(a, b, preferred_element_type=_F32, precision=precision)


def _dot_nt(a, b, precision=None):
    return lax.dot_general(a, b, (((1,), (1,)), ((), ())),
                           preferred_element_type=_F32, precision=precision)


def _dot_tn(a, b, precision=None):
    return lax.dot_general(a, b, (((0,), (0,)), ((), ())),
                           preferred_element_type=_F32, precision=precision)


def _split2(x):
    hi = x.astype(_BF16)
    return hi, (x - hi.astype(_F32)).astype(_BF16)


def _split3(x):
    hi = x.astype(_BF16)
    r = x - hi.astype(_F32)
    mid = r.astype(_BF16)
    return hi, mid, (r - mid.astype(_F32)).astype(_BF16)


def _dot3(a, b, dot=_dot):
    return dot(a[0], b[0]) + dot(a[1], b[0]) + dot(a[0], b[1])


def _rms(x, w):
    return x * lax.rsqrt(jnp.mean(x * x, axis=-1, keepdims=True) + EPS) * w


def _store_token_tiles(ref, x):
    n = x.shape[0]
    for j in range(SUBLANES):
        ref[pl.ds(j, n, stride=SUBLANES), :] = x[:, j * LANES:(j + 1) * LANES]


def _load_token_tiles(ref, n, j):
    return ref[pl.ds(j, n, stride=SUBLANES), :]


def _silu(x):
    return x * jax.nn.sigmoid(x)


def _params(*sem):
    return pltpu.CompilerParams(dimension_semantics=sem, vmem_limit_bytes=VMEM_LIMIT)


def _proj_kernel(x_ref, ln_ref, w_ref, qn_ref, kn_ref,
                 u_ref, z_ref, ab_ref, mq_ref, mk_ref, mv_ref, *, heads_out):
    xn = _rms(x_ref[...], ln_ref[...]).astype(_BF16)
    h = _dot(xn, w_ref[...])
    u_ref[...] = h[:, :CONV_DIM]
    z_ref[...] = h[:, CONV_DIM:CONV_DIM + DN_WIDTH]
    o = CONV_DIM + DN_WIDTH
    ab_ref[...] = h[:, o + 3 * ATT_WIDTH:]
    for hd in range(MOBA_HEADS):
        sl = slice(o + hd * HEAD_DIM, o + (hd + 1) * HEAD_DIM)
        q = _rms(h[:, sl], qn_ref[...])
        k = _rms(h[:, sl.start + ATT_WIDTH:sl.stop + ATT_WIDTH], kn_ref[...])
        v = h[:, sl.start + 2 * ATT_WIDTH:sl.stop + 2 * ATT_WIDTH]
        if heads_out:
            mq_ref[0, hd] = q
            mk_ref[0, hd] = k
            mv_ref[0, hd] = v
        else:
            hs = slice(hd * HEAD_DIM, (hd + 1) * HEAD_DIM)
            mq_ref[:, hs] = q
            mk_ref[:, hs] = k
            mv_ref[:, hs] = v


def _proj(x2d, ln_w, w_all, qn_w, kn_w, *, tm, seq=None):
    T, D = x2d.shape
    n_all = w_all.shape[1]
    grid = (T // tm,)
    row = lambda t: (t, 0)
    const = lambda t: (0, 0)
    if seq is not None:
        B, S = seq
        per = S // tm
        hshape = jax.ShapeDtypeStruct((B, MOBA_HEADS, S, HEAD_DIM), _F32)
        hspec = pl.BlockSpec((1, MOBA_HEADS, tm, HEAD_DIM), lambda t: (t // per, 0, t % per, 0))
    else:
        hshape = jax.ShapeDtypeStruct((T, ATT_WIDTH), _F32)
        hspec = pl.BlockSpec((tm, ATT_WIDTH), row)
    return pl.pallas_call(
        functools.partial(_proj_kernel, heads_out=seq is not None),
        grid=grid,
        in_specs=[pl.BlockSpec((tm, D), row), pl.BlockSpec((1, D), const),
                  pl.BlockSpec((D, n_all), const), pl.BlockSpec((1, HEAD_DIM), const),
                  pl.BlockSpec((1, HEAD_DIM), const)],
        out_specs=[pl.BlockSpec((tm, CONV_DIM), row), pl.BlockSpec((tm, DN_WIDTH), row),
                   pl.BlockSpec((tm, LANES), row), hspec, hspec, hspec],
        out_shape=[jax.ShapeDtypeStruct((T, CONV_DIM), _F32),
                   jax.ShapeDtypeStruct((T, DN_WIDTH), _F32),
                   jax.ShapeDtypeStruct((T, LANES), _F32), hshape, hshape, hshape],
        compiler_params=_params("parallel"),
        name="proj",
    )(x2d, ln_w, w_all, qn_w, kn_w)


def _gdn_kernel(u_ref, z_ref, ab_ref, conv0_ref, ssm0_ref, wconv_ref, alog_ref, dtb_ref, dnw_ref,
                od_ref, ssm_ref, up_scr, s_scr, *, lt, valid_len):
    i = pl.program_id(1)
    c = DN_CHUNK
    nc = lt // c
    heads = range(DN_HEADS)

    @pl.when(i == 0)
    def _():
        s_scr[...] = ssm0_ref[0]
        up_scr[0:SUBLANES, :] = conv0_ref[0]

    up_scr[SUBLANES:SUBLANES + lt, :] = u_ref[0]
    w = wconv_ref[...]
    base = SUBLANES - (CONV_W - 1)
    y = up_scr[base:base + lt, :] * w[0:1, :]
    for t in range(1, CONV_W):
        y = y + up_scr[base + t:base + t + lt, :] * w[t:t + 1, :]
    up_scr[0:SUBLANES, :] = up_scr[lt:lt + SUBLANES, :]
    qkv = _silu(y)

    masked = valid_len < lt
    if masked:
        rowv = lax.broadcasted_iota(jnp.int32, (lt, 1), 0) < valid_len
        qkv = jnp.where(rowv, qkv, 0.0)

    ab = ab_ref[0]
    lane = lax.broadcasted_iota(jnp.int32, (lt, LANES), 1)
    xg = ab + dtb_ref[...]
    softplus = jnp.maximum(xg, 0.0) + jnp.log(1.0 + jnp.exp(-jnp.abs(xg)))
    gb = jnp.where(lane < DN_HEADS, -jnp.exp(alog_ref[...]) * softplus, jax.nn.sigmoid(ab))
    if masked:
        gb = jnp.where(rowv, gb, 0.0)

    rows = lax.broadcasted_iota(jnp.int32, (c, c), 0)
    cols = lax.broadcasted_iota(jnp.int32, (c, c), 1)
    causal = cols <= rows
    strict = cols < rows
    eye = jnp.where(rows == cols, 1.0, 0.0)
    tril16 = jnp.where(causal, 1.0, 0.0).astype(_BF16)
    prow = lax.broadcasted_iota(jnp.int32, (DN_HEADS * c, LANES), 0) // c
    plane = lax.broadcasted_iota(jnp.int32, (DN_HEADS * c, LANES), 1)
    pick16 = jnp.where(prow == plane, 1.0, 0.0).astype(_BF16)

    cum, cum_t = [], []
    for ci in range(nc):
        g3 = _split3(gb[ci * c:(ci + 1) * c, :])
        cm = _dot(tril16, g3[0]) + _dot(tril16, g3[1]) + _dot(tril16, g3[2])
        c3 = _split3(cm)
        cum.append(cm)
        cum_t.append(_dot_nt(pick16, c3[0]) + _dot_nt(pick16, c3[1]) + _dot_nt(pick16, c3[2]))
    probs = [(ci, hd) for ci in range(nc) for hd in heads]

    def rows_of(ci):
        return slice(ci * c, (ci + 1) * c)

    q_l, k_l, v_l = [], [], []
    for ci, hd in probs:
        q = qkv[rows_of(ci), hd * HEAD_DIM:(hd + 1) * HEAD_DIM]
        k = qkv[rows_of(ci), DN_WIDTH + hd * HEAD_DIM:DN_WIDTH + (hd + 1) * HEAD_DIM]
        q_l.append(q * lax.rsqrt(jnp.sum(q * q, axis=-1, keepdims=True) + EPS) * (HEAD_DIM ** -0.5))
        k_l.append(k * lax.rsqrt(jnp.sum(k * k, axis=-1, keepdims=True) + EPS))
        v_l.append(qkv[rows_of(ci), 2 * DN_WIDTH + hd * HEAD_DIM:2 * DN_WIDTH + (hd + 1) * HEAD_DIM])
    gcum_l = [cum[ci][:, hd:hd + 1] for ci, hd in probs]
    beta_l = [gb[rows_of(ci), DN_HEADS + hd:DN_HEADS + hd + 1] for ci, hd in probs]
    decay_l = [jnp.exp(jnp.where(causal, g - cum_t[ci][hd * c:(hd + 1) * c, :], -jnp.inf))
               for (ci, hd), g in zip(probs, gcum_l)]
    k16_l = [k.astype(_BF16) for k in k_l]
    kb_l = [k * b for k, b in zip(k_l, beta_l)]
    low_l = [jnp.where(strict, _dot_nt(kb.astype(_BF16), k16) * d, 0.0)
             for kb, k16, d in zip(kb_l, k16_l, decay_l)]
    attn16_l = [(_dot_nt(q.astype(_BF16), k16) * d).astype(_BF16)
                for q, k16, d in zip(q_l, k16_l, decay_l)]

    low2_l = [_split2(l) for l in low_l]
    x_l = [eye - l for l in low_l]
    m_l = [_dot3(l2, l2) for l2 in low2_l]
    n_fac = max(1, int(math.ceil(math.log2(c)))) - 1
    for f in range(n_fac):
        m2_l = [_split2(m) for m in m_l]
        x_l = [x + _dot3(_split2(x), m2) for x, m2 in zip(x_l, m2_l)]
        if f + 1 < n_fac:
            m_l = [_dot3(m2, m2) for m2 in m2_l]
    eg_l = [jnp.exp(g) for g in gcum_l]
    uw_l = [_dot3(_split2(x), _split2(jnp.concatenate([v * b, kb * eg], axis=1)))
            for x, v, b, kb, eg in zip(x_l, v_l, beta_l, kb_l, eg_l)]
    u_l = [uw[:, :HEAD_DIM] for uw in uw_l]
    w16_l = [uw[:, HEAD_DIM:].astype(_BF16) for uw in uw_l]
    qe16_l = [(q * eg).astype(_BF16) for q, eg in zip(q_l, eg_l)]
    glast_l = [g[c - 1:c, :] for g in gcum_l]
    kdec16_l = [(k * jnp.exp(gl - g)).astype(_BF16) for k, gl, g in zip(k_l, glast_l, gcum_l)]
    eglast_l = [jnp.exp(gl) for gl in glast_l]

    state = [s_scr[hd] for hd in heads]
    for ci in range(nc):
        p0 = ci * DN_HEADS
        s16 = [s.astype(_BF16) for s in state]
        vnew = [u_l[p0 + hd] - _dot(w16_l[p0 + hd], s16[hd]) for hd in heads]
        vnew16 = [v.astype(_BF16) for v in vnew]
        o = [_dot(qe16_l[p0 + hd], s16[hd]) + _dot(attn16_l[p0 + hd], vnew16[hd]) for hd in heads]
        state = [state[hd] * eglast_l[p0 + hd] + _dot_tn(kdec16_l[p0 + hd], vnew16[hd]) for hd in heads]
        for hd in heads:
            zz = z_ref[0, rows_of(ci), hd * HEAD_DIM:(hd + 1) * HEAD_DIM]
            od_ref[0, rows_of(ci), hd * HEAD_DIM:(hd + 1) * HEAD_DIM] = (
                _rms(o[hd], dnw_ref[...]) * _silu(zz)).astype(_BF16)
    for hd in heads:
        s_scr[hd] = state[hd]
    ssm_ref[0] = s_scr[...]


def _gdn(u, z, ab, conv0, ssm0, w_conv, alog_row, dtb_row, dn_w, *, lt, valid_len):
    B, L, _ = u.shape
    grid = (B, L // lt)
    tile = lambda b, i: (b, i, 0)
    perb3 = lambda b, i: (b, 0, 0)
    const = lambda b, i: (0, 0)
    return pl.pallas_call(
        functools.partial(_gdn_kernel, lt=lt, valid_len=valid_len),
        grid=grid,
        in_specs=[pl.BlockSpec((1, lt, CONV_DIM), tile), pl.BlockSpec((1, lt, DN_WIDTH), tile),
                  pl.BlockSpec((1, lt, LANES), tile), pl.BlockSpec((1, SUBLANES, CONV_DIM), perb3),
                  pl.BlockSpec((1, DN_HEADS, HEAD_DIM, HEAD_DIM), lambda b, i: (b, 0, 0, 0)),
                  pl.BlockSpec((CONV_W, CONV_DIM), const), pl.BlockSpec((1, LANES), const),
                  pl.BlockSpec((1, LANES), const), pl.BlockSpec((1, HEAD_DIM), const)],
        out_specs=[pl.BlockSpec((1, lt, DN_WIDTH), tile),
                   pl.BlockSpec((1, DN_HEADS, HEAD_DIM, HEAD_DIM), lambda b, i: (b, 0, 0, 0))],
        out_shape=[jax.ShapeDtypeStruct((B, L, DN_WIDTH), _BF16),
                   jax.ShapeDtypeStruct((B, DN_HEADS, HEAD_DIM, HEAD_DIM), _F32)],
        scratch_shapes=[pltpu.VMEM((lt + 2 * SUBLANES, CONV_DIM), _F32),
                        pltpu.VMEM((DN_HEADS, HEAD_DIM, HEAD_DIM), _F32)],
        compiler_params=_params("parallel", "arbitrary"),
        name="gdn",
    )(u, z, ab, conv0, ssm0, w_conv, alog_row, dtb_row, dn_w)


def _topk_rows(g, n_sel):
    r = g.shape[0]
    row = lax.broadcasted_iota(jnp.int32, g.shape, 0)
    sel = jnp.zeros(g.shape, _F32)
    for _ in range(n_sel):
        m = jnp.max(g, axis=0, keepdims=True)
        idx = jnp.min(jnp.where(g == m, row, r), axis=0, keepdims=True)
        pick = row == idx
        sel = jnp.where(pick, 1.0, sel)
        g = jnp.where(pick, -jnp.inf, g)
    return sel


def _moba_prompt_kernel(q_ref, k_ref, v_ref, c_ref, o_ref, ps_ref, kb_scr, vt_scr, kmean_scr, sel_scr,
                        *, nblk):
    qt = pl.program_id(2)
    blk = MOBA_BLOCK
    scale = HEAD_DIM ** -0.5

    ps_ref[...] = jnp.sum(c_ref[...], axis=1)

    @pl.when(qt == 0)
    def _():
        kmean_scr[...] = jnp.zeros(kmean_scr.shape, _F32)
        for j in range(nblk):
            kj = k_ref[0, 0, j * blk:(j + 1) * blk, :]
            kb_scr[j] = kj.astype(_BF16)
            kmean_scr[j:j + 1, :] = jnp.mean(kj, axis=0, keepdims=True)
            vt_scr[j] = v_ref[0, 0, j * blk:(j + 1) * blk, :].T.astype(_BF16)

    q = q_ref[0, 0]
    gate = _dot_nt(kmean_scr[...], q, _HI)
    brow = lax.broadcasted_iota(jnp.int32, gate.shape, 0)
    valid = brow < qt
    sel = _topk_rows(jnp.where(valid, gate, -jnp.inf), min(MOBA_TOPK, nblk))
    sel_scr[...] = jnp.where(valid, sel, 0.0)

    q16 = (q * scale).astype(_BF16)
    kpos = lax.broadcasted_iota(jnp.int32, (blk, blk), 0)
    qpos = lax.broadcasted_iota(jnp.int32, (blk, blk), 1)
    def scores(j):
        return jnp.where(sel_scr[pl.ds(j, 1), :] > 0.0, _dot_nt(kb_scr[j], q16), NEG_BIG)

    s = jnp.where(kpos <= qpos, _dot_nt(kb_scr[qt], q16), NEG_BIG)
    m = jnp.max(s, axis=0, keepdims=True)
    p = jnp.exp(s - m)
    l = jnp.sum(p, axis=0, keepdims=True)

    def body(j, carry):
        s_cur, p_prev, j_prev, m, l, acc = carry
        s_next = scores(j + 1)
        pv = _dot(vt_scr[j_prev], p_prev)
        m_new = jnp.maximum(m, jnp.max(s_cur, axis=0, keepdims=True))
        alpha = jnp.exp(m - m_new)
        p = jnp.exp(s_cur - m_new)
        l = alpha * l + jnp.sum(p, axis=0, keepdims=True)
        return s_next, p.astype(_BF16), j, m_new, l, alpha * (acc + pv)

    init = (scores(0), p.astype(_BF16), qt, m, l, jnp.zeros((HEAD_DIM, blk), _F32))
    _, p_prev, j_prev, m, l, acc = lax.fori_loop(0, qt, body, init)
    acc = acc + _dot(vt_scr[j_prev], p_prev)
    o_ref[0] = (acc / l).T.astype(_BF16)


def _moba_prompt(q, k, v, slabs):
    B, H, S, dh = q.shape
    nblk = S // MOBA_BLOCK
    nb8 = -(-nblk // SUBLANES) * SUBLANES
    assert S % MOBA_BLOCK == 0
    n_slabs, ps, _ = slabs.shape
    steps = B * H * nblk
    share = n_slabs // steps
    assert share * steps == n_slabs and share % SUBLANES == 0, "key cache does not split evenly over the grid"
    full = lambda b, h, t: (b, h, 0, 0)
    flat = lambda b, h, t: (b * H + h) * nblk + t
    return pl.pallas_call(
        functools.partial(_moba_prompt_kernel, nblk=nblk),
        grid=(B, H, nblk),
        in_specs=[pl.BlockSpec((1, 1, MOBA_BLOCK, dh), lambda b, h, t: (b, h, t, 0)),
                  pl.BlockSpec((1, 1, S, dh), full), pl.BlockSpec((1, 1, S, dh), full),
                  pl.BlockSpec((share, ps, dh), lambda b, h, t: (flat(b, h, t), 0, 0))],
        out_specs=[pl.BlockSpec((1, MOBA_BLOCK, dh), lambda b, h, t: (b, t, h)),
                   pl.BlockSpec((share, dh), lambda b, h, t: (flat(b, h, t), 0))],
        out_shape=[jax.ShapeDtypeStruct((B, S, H * dh), _BF16),
                   jax.ShapeDtypeStruct((n_slabs, dh), _F32)],
        scratch_shapes=[pltpu.VMEM((nblk, MOBA_BLOCK, dh), _BF16),
                        pltpu.VMEM((nblk, dh, MOBA_BLOCK), _BF16),
                        pltpu.VMEM((nb8, dh), _F32),
                        pltpu.VMEM((nb8, MOBA_BLOCK), _F32)],
        compiler_params=_params("parallel", "parallel", "arbitrary"),
        name="moba_prompt",
    )(q, k, v, slabs)


def _moba_sel_kernel(pt_ref, psum_ref, q_ref, sel_ref, km_scr, *, n_pages, ppb, n_blk, n_sel, rows_per_blk):
    b = pl.program_id(0)
    km_scr[...] = jnp.zeros(km_scr.shape, _F32)

    def fill(j, carry):
        acc = psum_ref[pl.ds(pt_ref[b * n_pages + j * ppb], 1), :]
        for r in range(1, ppb):
            acc = acc + psum_ref[pl.ds(pt_ref[b * n_pages + j * ppb + r], 1), :]
        km_scr[pl.ds(j, 1), :] = acc * (1.0 / rows_per_blk)
        return carry

    lax.fori_loop(0, n_blk, fill, 0)
    lane = lax.broadcasted_iota(jnp.int32, (SUBLANES, LANES), 1)
    for hd in range(MOBA_HEADS):
        g = _dot_nt(q_ref[0, hd], km_scr[:, hd * HEAD_DIM:(hd + 1) * HEAD_DIM], _HI)
        g = jnp.where(lane < n_blk, g, -jnp.inf)
        out = jnp.zeros((SUBLANES, LANES), jnp.int32)
        for r in range(n_sel):
            m = jnp.max(g, axis=1, keepdims=True)
            idx = jnp.min(jnp.where(g == m, lane, LANES), axis=1, keepdims=True)
            out = jnp.where(lane == r, idx, out)
            g = jnp.where(lane == idx, -jnp.inf, g)
        sel_ref[0, hd] = out


def _moba_sel(page_table, psum, q8, *, ppb, n_sel, rows_per_blk):
    DB, n_pages = page_table.shape
    n_blk = n_pages // ppb
    assert n_blk <= LANES
    n_phys, width = psum.shape
    return pl.pallas_call(
        functools.partial(_moba_sel_kernel, n_pages=n_pages, ppb=ppb, n_blk=n_blk, n_sel=n_sel,
                          rows_per_blk=rows_per_blk),
        grid_spec=pltpu.PrefetchScalarGridSpec(
            num_scalar_prefetch=1,
            grid=(DB,),
            in_specs=[pl.BlockSpec((n_phys, width), lambda b, pt: (0, 0)),
                      pl.BlockSpec((1, MOBA_HEADS, SUBLANES, HEAD_DIM), lambda b, pt: (b, 0, 0, 0))],
            out_specs=pl.BlockSpec((1, MOBA_HEADS, SUBLANES, LANES), lambda b, pt: (b, 0, 0, 0)),
            scratch_shapes=[pltpu.VMEM((LANES, width), _F32)]),
        out_shape=jax.ShapeDtypeStruct((DB, MOBA_HEADS, SUBLANES, LANES), jnp.int32),
        compiler_params=_params("arbitrary"),
        name="moba_sel",
    )(page_table.reshape(-1), psum, q8)


def _moba_sample_kernel(phys_ref, q_ref, kn_ref, vn_ref, ck_hbm, cv_hbm, o_ref, kbuf, vbuf, sem,
                        *, n_pg, n_new, n_heads, n_steps):
    step = pl.program_id(0)
    per = n_new * n_pg
    scale = HEAD_DIM ** -0.5

    def page_copies(st, slot):
        hd = st % n_heads
        cps = []
        for j in range(per):
            page = phys_ref[st * per + j]
            cps.append(pltpu.make_async_copy(ck_hbm.at[page, hd], kbuf.at[slot, j], sem.at[slot]))
            cps.append(pltpu.make_async_copy(cv_hbm.at[page, hd], vbuf.at[slot, j], sem.at[slot]))
        return cps

    def attend(slot):
        q = q_ref[0, 0]
        q16 = q.astype(_BF16)
        kn = kn_ref[0, 0]
        vn = vn_ref[0, 0]
        rowq = lax.broadcasted_iota(jnp.int32, (SUBLANES, 1), 0)
        s_own = []
        for c in range(n_new):
            sc = jnp.sum(q * kn[c:c + 1, :], axis=-1, keepdims=True) * scale
            s_own.append(jnp.where(rowq >= c, sc, NEG_BIG))
        m_own = s_own[0]
        for sc in s_own[1:]:
            m_own = jnp.maximum(m_own, sc)
        out = jnp.zeros((SUBLANES, HEAD_DIM), _F32)
        for t in range(n_new):
            pages = range(t * n_pg, (t + 1) * n_pg)
            s_past = [_dot_nt(q16, kbuf[slot, j].astype(_BF16)) * scale for j in pages]
            m = m_own
            for sp in s_past:
                m = jnp.maximum(m, jnp.max(sp, axis=-1, keepdims=True))
            l = jnp.zeros((SUBLANES, 1), _F32)
            acc = jnp.zeros((SUBLANES, HEAD_DIM), _F32)
            for sp, j in zip(s_past, pages):
                p = jnp.exp(sp - m)
                l = l + jnp.sum(p, axis=-1, keepdims=True)
                acc = acc + _dot(p.astype(_BF16), vbuf[slot, j].astype(_BF16))
            for c, sc in enumerate(s_own):
                p = jnp.exp(sc - m)
                l = l + p
                acc = acc + p * vn[c:c + 1, :]
            out = jnp.where(rowq == t, acc / l, out)
        o_ref[0, 0] = out

    def run(slot):
        if slot == 0:
            @pl.when(step == 0)
            def _():
                for cp in page_copies(0, 0):
                    cp.start()

        @pl.when(step + 1 < n_steps)
        def _():
            for cp in page_copies(step + 1, 1 - slot):
                cp.start()

        for cp in page_copies(step, slot):
            cp.wait()
        attend(slot)

    for slot in range(2):
        @pl.when(step % 2 == slot)
        def _():
            run(slot)


def _moba_sample(phys, q8, kn8, vn8, cache_k, cache_v, *, n_new):
    DB, H, _, dh = q8.shape
    ps = cache_k.shape[2]
    n_pg = phys.shape[0] // (DB * H * n_new)
    steps = DB * H
    small = pl.BlockSpec((1, 1, SUBLANES, dh), lambda s, ph: (s // H, s % H, 0, 0))
    return pl.pallas_call(
        functools.partial(_moba_sample_kernel, n_pg=n_pg, n_new=n_new, n_heads=H, n_steps=steps),
        grid_spec=pltpu.PrefetchScalarGridSpec(
            num_scalar_prefetch=1,
            grid=(steps,),
            in_specs=[small, small, small, pl.BlockSpec(memory_space=pl.ANY),
                      pl.BlockSpec(memory_space=pl.ANY)],
            out_specs=small,
            scratch_shapes=[pltpu.VMEM((2, n_new * n_pg, ps, dh), _F32),
                            pltpu.VMEM((2, n_new * n_pg, ps, dh), _F32),
                            pltpu.SemaphoreType.DMA((2,))]),
        out_shape=jax.ShapeDtypeStruct((DB, H, SUBLANES, dh), _F32),
        compiler_params=_params("arbitrary"),
        name="moba_sample",
    )(phys, q8, kn8, vn8, cache_k, cache_v)


def _post_kernel(xp_ref, odp_ref, omp_ref, xs_ref, ods_ref, oms_ref, wo1_ref, wo2_ref, ln_ref,
                 wrh_ref, wrl_ref, br_ref, x1_ref, xn_ref, route_ref, cnt_ref, carry_scr,
                 *, tm, n_prompt_tiles, n_tokens):
    i = pl.program_id(0)

    @pl.when(i == 0)
    def _():
        carry_scr[...] = jnp.zeros(carry_scr.shape, _F32)

    is_p = i < n_prompt_tiles
    x = jnp.where(is_p, xp_ref[...], xs_ref[...])
    od = jnp.where(is_p, odp_ref[...], ods_ref[...])
    om = jnp.where(is_p, omp_ref[...], oms_ref[...])
    x1 = x + _dot(od, wo1_ref[...]) + _dot(om, wo2_ref[...])
    _store_token_tiles(x1_ref, x1)
    xn = _rms(x1, ln_ref[...])
    _store_token_tiles(xn_ref, xn)
    xh, xl = _split2(xn)
    logits = (_dot_nt(wrh_ref[...], xh) + _dot_nt(wrh_ref[...], xl) + _dot_nt(wrl_ref[...], xh)) + br_ref[...]

    erow = lax.broadcasted_iota(jnp.int32, (N_EXPERTS, tm), 0)
    g = logits
    vals, picks = [], []
    for _ in range(TOP_K):
        m = jnp.max(g, axis=0, keepdims=True)
        idx = jnp.min(jnp.where(g == m, erow, N_EXPERTS), axis=0, keepdims=True)
        pick = erow == idx
        vals.append(m)
        picks.append((idx, pick))
        g = jnp.where(pick, -jnp.inf, g)
    es = [jnp.exp(v - vals[0]) for v in vals]
    den = es[0]
    for e in es[1:]:
        den = den + e

    rows = lax.broadcasted_iota(jnp.int32, (tm, tm), 0)
    cols = lax.broadcasted_iota(jnp.int32, (tm, tm), 1)
    earlier = jnp.where(rows < cols, 1.0, 0.0).astype(_BF16)
    real = jnp.where((i * tm + lax.broadcasted_iota(jnp.int32, (1, tm), 1)) < n_tokens, 1.0, 0.0)
    base = carry_scr[...]
    tok = lax.broadcasted_iota(jnp.int32, (1, tm), 1)
    ranks = [jnp.zeros((1, tm), _F32) for _ in picks]
    for sub in range(tm // TOK_TILE):
        in_sub = jnp.where((tok >= sub * TOK_TILE) & (tok < (sub + 1) * TOK_TILE), real, 0.0)
        for k, (idx, pick) in enumerate(picks):
            onehot = jnp.where(pick, in_sub, 0.0)
            pref = _dot(onehot.astype(_BF16), earlier) + base
            ranks[k] = ranks[k] + jnp.sum(onehot * pref, axis=0, keepdims=True)
            base = base + jnp.sum(onehot, axis=1, keepdims=True)
    carry_scr[...] = base
    cnt_ref[...] = base
    record = ([idx.astype(_F32) for idx, _ in picks] + [e / den for e in es] + ranks
              + [jnp.zeros((route_ref.shape[0] - 3 * TOP_K, tm), _F32)])
    route_ref[...] = jnp.concatenate(record, axis=0)


def _post(xp, odp, omp, xs, ods, oms, wo1, wo2, ln_w, wr_hi, wr_lo, b_r, *, tm, n_tokens):
    Tp, D = xp.shape
    npt = Tp // tm
    T = Tp + tm
    prow = lambda t: (jnp.minimum(t, npt - 1), 0)
    srow = lambda t: (0, 0)
    row = lambda t: (t, 0)
    const = lambda t: (0, 0)
    return pl.pallas_call(
        functools.partial(_post_kernel, tm=tm, n_prompt_tiles=npt, n_tokens=n_tokens),
        grid=(npt + 1,),
        in_specs=[pl.BlockSpec((tm, D), prow), pl.BlockSpec((tm, DN_WIDTH), prow),
                  pl.BlockSpec((tm, ATT_WIDTH), prow),
                  pl.BlockSpec((tm, D), srow), pl.BlockSpec((tm, DN_WIDTH), srow),
                  pl.BlockSpec((tm, ATT_WIDTH), srow),
                  pl.BlockSpec((DN_WIDTH, D), const), pl.BlockSpec((ATT_WIDTH, D), const),
                  pl.BlockSpec((1, D), const), pl.BlockSpec((N_EXPERTS, D), const),
                  pl.BlockSpec((N_EXPERTS, D), const), pl.BlockSpec((N_EXPERTS, 1), const)],
        out_specs=[pl.BlockSpec((tm * SUBLANES, LANES), row), pl.BlockSpec((tm * SUBLANES, LANES), row),
                   pl.BlockSpec((ROUTE_ROWS, tm), lambda t: (0, t)), pl.BlockSpec((N_EXPERTS, 1), const)],
        out_shape=[jax.ShapeDtypeStruct((T * SUBLANES, LANES), _F32),
                   jax.ShapeDtypeStruct((T * SUBLANES, LANES), _F32),
                   jax.ShapeDtypeStruct((ROUTE_ROWS, T), _F32), jax.ShapeDtypeStruct((N_EXPERTS, 1), _F32)],
        scratch_shapes=[pltpu.VMEM((N_EXPERTS, 1), _F32)],
        compiler_params=_params("arbitrary"),
        name="post",
    )(xp, odp, omp, xs, ods, oms, wo1, wo2, ln_w, wr_hi, wr_lo, b_r)


RUN_PIECES = tuple(1 << b for b in reversed(range(TOK_TILE.bit_length())))
RUN_RARE = 64


def _all_run_copies(make_copy, start_ref, len_ref, tile):
    runs = []
    pool_start = jnp.int32(0)
    for e in range(N_EXPERTS):
        n = len_ref[tile * N_EXPERTS + e]
        first = start_ref[tile * N_EXPERTS + e]
        pieces = []
        for piece in RUN_PIECES:
            done = n & (-2 * piece)
            pieces.append((piece, make_copy(first + done, pool_start + done, piece)))
        runs.append((n, pieces))
        pool_start = pool_start + n
    return runs


def _for_pieces(runs, act):
    for n, pieces in runs:
        def some(which):
            for piece, cp in pieces:
                if which(piece):
                    @pl.when((n & piece) != 0)
                    def _():
                        act(cp)

        @pl.when(n >= RUN_RARE)
        def _():
            some(lambda piece: piece >= RUN_RARE)

        some(lambda piece: piece < RUN_RARE)


def _start_all(runs):
    _for_pieces(runs, lambda cp: cp.start())


def _wait_all(runs):
    _for_pieces(runs, lambda cp: cp.wait())


def _rows(ref, first_row, n_rows):
    return ref.at[pl.ds(pl.multiple_of(first_row * SUBLANES, SUBLANES), n_rows * SUBLANES)]


def _dispatch_kernel(start_ref, len_ref, tail_ref, nu_ref, slot_hbm, xn_ref, xs_hbm, idx_smem, pool, zero_scr,
                     sem_idx, sem_zero, sem_rows, *, tm, n_blocks, n_tiles):
    i = pl.program_id(0)
    n_asg = tm * TOP_K
    idx_copy = pltpu.make_async_copy(slot_hbm.at[pl.ds(i * n_asg, n_asg)], idx_smem, sem_idx)
    idx_copy.start()

    @pl.when(i == 0)
    def _():
        zero_scr[...] = jnp.zeros(zero_scr.shape, _F32)

        def clear(first_row):
            first = pl.multiple_of(first_row * SUBLANES, MOE_ROWS * SUBLANES)
            cp = pltpu.make_async_copy(zero_scr, xs_hbm.at[pl.ds(first, MOE_ROWS * SUBLANES)], sem_zero)
            cp.start()
            cp.wait()

        for e in range(N_EXPERTS):
            @pl.when(tail_ref[e] >= 0)
            def _():
                clear(tail_ref[e])

        def clear_block(b, carry):
            clear(b * MOE_ROWS)
            return carry

        lax.fori_loop(nu_ref[0], n_blocks, clear_block, 0)

    idx_copy.wait()

    def run_copies(tile, slot):
        return _all_run_copies(
            lambda hbm_row, pool_row, n: pltpu.make_async_copy(
                _rows(pool.at[slot], pool_row, n), _rows(xs_hbm, hbm_row, n), sem_rows.at[slot]),
            start_ref, len_ref, tile)

    def step(slot):
        @pl.when(i >= 2)
        def _():
            _wait_all(run_copies(i - 2, slot))

        def pack(t, carry):
            tile = xn_ref[pl.ds(pl.multiple_of(t * SUBLANES, SUBLANES), SUBLANES), :]
            for k in range(TOP_K):
                row = pl.multiple_of(idx_smem[k * tm + t] * SUBLANES, SUBLANES)
                pool[slot, pl.ds(row, SUBLANES), :] = tile
            return carry

        lax.fori_loop(0, tm, pack, 0, unroll=4)
        _start_all(run_copies(i, slot))

        @pl.when(i == n_tiles - 1)
        def _():
            _wait_all(run_copies(i, slot))

            @pl.when(i >= 1)
            def _():
                _wait_all(run_copies(i - 1, 1 - slot))

    for slot in range(2):
        @pl.when(i % 2 == slot)
        def _():
            step(slot)


def _dispatch(run_start, run_len, tail, n_used, slot, xn_tiles, n_tiles, n_blocks, *, tm):
    pre = lambda i, *_: (i, 0)
    return pl.pallas_call(
        functools.partial(_dispatch_kernel, tm=tm, n_blocks=n_blocks, n_tiles=n_tiles),
        grid_spec=pltpu.PrefetchScalarGridSpec(
            num_scalar_prefetch=4,
            grid=(n_tiles,),
            in_specs=[pl.BlockSpec(memory_space=pl.ANY), pl.BlockSpec((tm * SUBLANES, LANES), pre)],
            out_specs=pl.BlockSpec(memory_space=pl.ANY),
            scratch_shapes=[pltpu.SMEM((tm * TOP_K,), jnp.int32),
                            pltpu.VMEM((2, (tm * TOP_K + 1) * SUBLANES, LANES), _F32),
                            pltpu.VMEM((MOE_ROWS * SUBLANES, LANES), _F32),
                            pltpu.SemaphoreType.DMA, pltpu.SemaphoreType.DMA, pltpu.SemaphoreType.DMA((2,))]),
        out_shape=jax.ShapeDtypeStruct((n_blocks * MOE_ROWS * SUBLANES, LANES), _F32),
        compiler_params=_params("arbitrary"),
        name="dispatch",
    )(run_start, run_len, tail, n_used, slot, xn_tiles)


def _moe_kernel(be_ref, nu_ref, x_ref, wg_ref, bg_ref, wu_ref, bu_ref, wd_ref, bd_ref, o_ref,
                wg_scr, wu_scr, wd_scr, x_scr):
    i = pl.program_id(0)

    @pl.when(i >= nu_ref[0])
    def _():
        o_ref[...] = jnp.zeros(o_ref.shape, _F32)

    @pl.when(i < nu_ref[0])
    def _():
        prev = be_ref[jnp.maximum(i - 1, 0)]

        @pl.when((i == 0) | (be_ref[i] != prev))
        def _():
            wg_scr[...] = wg_ref[0].astype(_BF16)
            wu_scr[...] = wu_ref[0].astype(_BF16)
            wd_scr[...] = wd_ref[0].astype(_BF16)

        for j in range(SUBLANES):
            x_scr[:, j * LANES:(j + 1) * LANES] = _load_token_tiles(x_ref, MOE_ROWS, j).astype(_BF16)
        x = x_scr[...]
        gt = jnp.minimum(_dot(x, wg_scr[...]) + bg_ref[0], SWIGLU_LIMIT)
        up = jnp.clip(_dot(x, wu_scr[...]) + bu_ref[0], -SWIGLU_LIMIT, SWIGLU_LIMIT)
        act = ((up + 1.0) * (gt * jax.nn.sigmoid(SWIGLU_ALPHA * gt))).astype(_BF16)
        _store_token_tiles(o_ref, _dot(act, wd_scr[...]) + bd_ref[0])


def _moe_ffn(block_e, n_used, xs, n_blocks, w_gate, b_gate, w_up, b_up, w_down, b_down):
    E, D, F = w_gate.shape
    assert D == SUBLANES * LANES
    rows = MOE_ROWS * SUBLANES
    xblk = lambda i, be, nu: (jnp.minimum(i, nu[0] - 1), 0)
    wsel = lambda i, be, nu: (be[jnp.minimum(i, nu[0] - 1)], 0, 0)
    return pl.pallas_call(
        _moe_kernel,
        grid_spec=pltpu.PrefetchScalarGridSpec(
            num_scalar_prefetch=2,
            grid=(n_blocks,),
            in_specs=[pl.BlockSpec((rows, LANES), xblk),
                      pl.BlockSpec((1, D, F), wsel), pl.BlockSpec((1, 1, F), wsel),
                      pl.BlockSpec((1, D, F), wsel), pl.BlockSpec((1, 1, F), wsel),
                      pl.BlockSpec((1, F, D), wsel), pl.BlockSpec((1, 1, D), wsel)],
            out_specs=pl.BlockSpec((rows, LANES), lambda i, be, nu: (i, 0)),
            scratch_shapes=[pltpu.VMEM((D, F), _BF16), pltpu.VMEM((D, F), _BF16),
                            pltpu.VMEM((F, D), _BF16), pltpu.VMEM((MOE_ROWS, D), _BF16)]),
        out_shape=jax.ShapeDtypeStruct((n_blocks * rows, LANES), _F32),
        compiler_params=_params("arbitrary"),
        name="moe_ffn",
    )(block_e, n_used, xs, w_gate, b_gate.reshape(E, 1, F), w_up, b_up.reshape(E, 1, F),
      w_down, b_down.reshape(E, 1, D))


def _combine_kernel(start_ref, len_ref, slot_hbm, gate_hbm, ys_hbm, x1_ref, yp_ref, yr_ref,
                    idx_smem0, idx_smem1, gate_smem0, gate_smem1, pool, out_scr, sem_idx, sem_rows,
                    *, tm, n_prompt_tiles, n_tiles):
    i = pl.program_id(0)
    n_asg = tm * TOP_K
    idx_smem = (idx_smem0, idx_smem1)
    gate_smem = (gate_smem0, gate_smem1)

    def run_copies(tile, slot):
        return _all_run_copies(
            lambda hbm_row, pool_row, n: pltpu.make_async_copy(
                _rows(ys_hbm, hbm_row, n), _rows(pool.at[slot], pool_row, n), sem_rows.at[slot]),
            start_ref, len_ref, tile)

    def table_copies(tile, slot):
        return [pltpu.make_async_copy(slot_hbm.at[pl.ds(tile * n_asg, n_asg)], idx_smem[slot], sem_idx.at[slot]),
                pltpu.make_async_copy(gate_hbm.at[pl.ds(tile * n_asg, n_asg)], gate_smem[slot], sem_idx.at[slot])]

    def fetch(tile, slot):
        for cp in table_copies(tile, slot):
            cp.start()
        _start_all(run_copies(tile, slot))

    def step(slot):
        if slot == 0:
            @pl.when(i == 0)
            def _():
                fetch(0, 0)

        @pl.when(i + 1 < n_tiles)
        def _():
            fetch(i + 1, 1 - slot)

        for cp in table_copies(i, slot):
            cp.wait()
        _wait_all(run_copies(i, slot))

        def token(t, carry):
            at = pl.ds(pl.multiple_of(t * SUBLANES, SUBLANES), SUBLANES)
            acc = x1_ref[at, :]
            for k in range(TOP_K):
                row = pl.multiple_of(idx_smem[slot][k * tm + t] * SUBLANES, SUBLANES)
                acc = acc + gate_smem[slot][k * tm + t] * pool[slot, pl.ds(row, SUBLANES), :]
            out_scr[at, :] = acc
            return carry

        lax.fori_loop(0, tm, token, 0, unroll=4)

        def emit(o_ref):
            for j in range(SUBLANES):
                o_ref[:, j * LANES:(j + 1) * LANES] = _load_token_tiles(out_scr, tm, j)

        @pl.when(i < n_prompt_tiles)
        def _():
            emit(yp_ref)

        @pl.when(i >= n_prompt_tiles)
        def _():
            emit(yr_ref)

    for slot in range(2):
        @pl.when(i % 2 == slot)
        def _():
            step(slot)


def _combine(run_start, run_len, slot, gates, ys, x1_tiles, n_prompt, n_tiles, *, tm):
    D = SUBLANES * LANES
    npt = n_prompt // tm
    pre = lambda i, *_: (i, 0)
    return pl.pallas_call(
        functools.partial(_combine_kernel, tm=tm, n_prompt_tiles=npt, n_tiles=n_tiles),
        grid_spec=pltpu.PrefetchScalarGridSpec(
            num_scalar_prefetch=2,
            grid=(n_tiles,),
            in_specs=[pl.BlockSpec(memory_space=pl.ANY), pl.BlockSpec(memory_space=pl.ANY),
                      pl.BlockSpec(memory_space=pl.ANY), pl.BlockSpec((tm * SUBLANES, LANES), pre)],
            out_specs=[pl.BlockSpec((tm, D), lambda i, *_: (jnp.minimum(i, npt - 1), 0)),
                       pl.BlockSpec((tm, D), lambda i, *_: (jnp.maximum(i - npt, 0), 0))],
            scratch_shapes=[pltpu.SMEM((tm * TOP_K,), jnp.int32), pltpu.SMEM((tm * TOP_K,), jnp.int32),
                            pltpu.SMEM((tm * TOP_K,), _F32), pltpu.SMEM((tm * TOP_K,), _F32),
                            pltpu.VMEM((2, tm * TOP_K * SUBLANES, LANES), _F32),
                            pltpu.VMEM((tm * SUBLANES, LANES), _F32),
                            pltpu.SemaphoreType.DMA((2,)), pltpu.SemaphoreType.DMA((2,))]),
        out_shape=[jax.ShapeDtypeStruct((n_prompt, D), _F32),
                   jax.ShapeDtypeStruct(((n_tiles - npt) * tm, D), _F32)],
        compiler_params=_params("arbitrary"),
        name="combine",
    )(run_start, run_len, slot, gates, ys, x1_tiles)


def _pad_lanes(v, fill=0.0):
    v = v.reshape(1, -1).astype(_F32)
    return jnp.pad(v, ((0, 0), (0, LANES - v.shape[1])), constant_values=fill)


def _pad_rows(a, rows):
    return jnp.pad(a, ((0, rows - a.shape[0]), (0, 0)))


def _prep_w_in(w_in):
    off_a = CONV_DIM + DN_WIDTH
    off_mq = off_a + 2 * DN_HEADS
    ab = jnp.pad(w_in[:, off_a:off_mq], ((0, 0), (0, LANES - 2 * DN_HEADS)))
    return jnp.concatenate([w_in[:, :off_a], w_in[:, off_mq:], ab], axis=1).astype(_BF16)


def _moe(xn_tiles, x1_tiles, route_t, counts, n_prompt, n_tokens, w_gate, b_gate, w_up, b_up, w_down, b_down):
    tm = TOK_TILE
    n_tiles = -(-n_tokens // tm)
    tc = n_tiles * tm
    idx = route_t[0:TOP_K, :tc].astype(jnp.int32)
    gates = route_t[TOP_K:2 * TOP_K, :tc]
    rank = route_t[2 * TOP_K:3 * TOP_K, :tc].astype(jnp.int32)
    cnt = counts[:, 0].astype(jnp.int32)
    padded = (cnt + MOE_ROWS - 1) // MOE_ROWS * MOE_ROWS
    pends = jnp.cumsum(padded)
    pstarts = pends - padded
    n_blocks = -(-n_tokens * TOP_K // MOE_ROWS) + N_EXPERTS
    real = (jnp.arange(tc, dtype=jnp.int32) < n_tokens)[None, :]
    experts = jnp.arange(N_EXPERTS, dtype=jnp.int32)[:, None, None]
    onehot = (idx[None] == experts) & real[None]
    hist = jnp.sum(onehot.reshape(N_EXPERTS, TOP_K, n_tiles, tm), axis=(1, 3), dtype=jnp.int32).T
    before_tile = jnp.cumsum(hist, axis=0) - hist
    before_expert = jnp.cumsum(hist, axis=1) - hist
    run_start = (pstarts[None, :] + before_tile).reshape(-1)
    shift = jnp.repeat((before_tile - before_expert).T, tm, axis=1)
    slot = rank - jnp.sum(jnp.where(onehot, shift[:, None, :], 0), axis=0)
    per_tile = lambda d: d.reshape(TOP_K, n_tiles, tm).transpose(1, 0, 2).reshape(-1)
    slot_pack = per_tile(jnp.where(real, slot, tm * TOP_K))
    slot_read = per_tile(jnp.where(real, slot, 0))
    block_start = jnp.arange(n_blocks, dtype=jnp.int32) * MOE_ROWS
    block_e = jnp.minimum(jnp.sum((pends[None, :] <= block_start[:, None]).astype(jnp.int32), axis=1),
                          N_EXPERTS - 1)
    n_used = (pends[-1:] // MOE_ROWS).astype(jnp.int32)
    tail = jnp.where(cnt > 0, pends - MOE_ROWS, -1).astype(jnp.int32)
    run_len = hist.reshape(-1)
    xs = _dispatch(run_start, run_len, tail, n_used, slot_pack, xn_tiles, n_tiles, n_blocks, tm=tm)
    ys = _moe_ffn(block_e, n_used, xs, n_blocks, w_gate, b_gate, w_up, b_up, w_down, b_down)
    return _combine(run_start, run_len, slot_read, per_tile(gates), ys, x1_tiles, n_prompt, n_tiles, tm=tm)


def _layer(xp, xs_, cache_k, cache_v, conv_s, ssm_s, page_table, lw):
    (ln1_w, w_in, w_conv, a_log, dt_bias, dn_norm_w, q_norm_w, k_norm_w, w_o, ln2_w, w_router,
     b_router, w_gate, b_gate, w_up, b_up, w_down, b_down) = lw
    B, S, D = xp.shape
    DB, L, _ = xs_.shape
    n_phys, H, PS, dh = cache_k.shape
    n_pages = page_table.shape[1]
    ppb = MOBA_BLOCK // PS
    Tp, Ts = B * S, DB * L
    assert (n_pages * PS) % MOBA_BLOCK == 0, "paged past must end on a MoBA block boundary"
    assert CONV_W - 1 <= L <= SUBLANES and Ts <= POST_ROWS and POST_ROWS % TOK_TILE == 0
    assert Tp % PROJ_ROWS == 0 and S % GDN_ROWS == 0 and Tp % POST_ROWS == 0
    cur = n_pages // ppb
    n_sel = min(MOBA_TOPK, cur)
    assert n_sel > 0

    w_all = _prep_w_in(w_in)
    ln1 = ln1_w.reshape(1, D)
    qn = q_norm_w.reshape(1, dh)
    kn = k_norm_w.reshape(1, dh)
    alog_row = _pad_lanes(a_log)
    dtb_row = _pad_lanes(dt_bias)
    dnw = dn_norm_w.reshape(1, dh)

    u_p, z_p, ab_p, mq_p, mk_p, mv_p = _proj(xp.reshape(Tp, D), ln1, w_all, qn, kn, tm=PROJ_ROWS, seq=(B, S))
    od_p, ssm_p = _gdn(u_p.reshape(B, S, CONV_DIM), z_p.reshape(B, S, DN_WIDTH), ab_p.reshape(B, S, LANES),
                       jnp.zeros((B, SUBLANES, CONV_DIM), _F32), jnp.zeros((B, DN_HEADS, dh, dh), _F32),
                       w_conv, alog_row, dtb_row, dnw, lt=GDN_ROWS, valid_len=GDN_ROWS)
    om_p, psum = _moba_prompt(mq_p, mk_p, mv_p, cache_k.reshape(n_phys * H, PS, dh))
    conv_p = u_p.reshape(B, S, CONV_DIM)[:, S - (CONV_W - 1):]

    u_s, z_s, ab_s, mq_s, mk_s, mv_s = _proj(xs_.reshape(Ts, D), ln1, w_all, qn, kn, tm=Ts)
    padl = lambda a: jnp.pad(a.reshape(DB, L, -1), ((0, 0), (0, DN_CHUNK - L), (0, 0)))
    conv0 = jnp.pad(conv_s, ((0, 0), (SUBLANES - (CONV_W - 1), 0), (0, 0)))
    od_s, ssm_s_new = _gdn(padl(u_s), padl(z_s), padl(ab_s), conv0, ssm_s, w_conv, alog_row, dtb_row, dnw,
                           lt=DN_CHUNK, valid_len=L)
    od_s = od_s[:, :L].reshape(Ts, DN_WIDTH)
    conv_s_new = jnp.concatenate([conv_s, u_s.reshape(DB, L, CONV_DIM)], axis=1)[:, L:]

    heads = lambda a: a.reshape(DB, L, H, dh).transpose(0, 2, 1, 3)
    pad8 = lambda a: jnp.pad(a, ((0, 0), (0, 0), (0, SUBLANES - L), (0, 0)))
    q8, k8, v8 = pad8(heads(mq_s)), pad8(heads(mk_s)), pad8(heads(mv_s))
    psum = psum.reshape(n_phys, H * dh)
    sel = _moba_sel(page_table, psum, q8, ppb=ppb, n_sel=n_sel, rows_per_blk=MOBA_BLOCK)
    sel = sel[:, :, :L, :n_sel]
    logical = sel[..., None] * ppb + jnp.arange(ppb, dtype=jnp.int32)
    phys = page_table[jnp.arange(DB)[:, None, None, None, None], logical]
    om_s = _moba_sample(phys.reshape(-1).astype(jnp.int32), q8, k8, v8, cache_k, cache_v, n_new=L)
    om_s = om_s[:, :, :L].transpose(0, 2, 1, 3).reshape(Ts, H * dh).astype(_BF16)

    wo = w_o.astype(_BF16)
    wr = w_router.T
    wr_hi = wr.astype(_BF16)
    wr_lo = (wr - wr_hi.astype(_F32)).astype(_BF16)
    x1_tiles, xn_tiles, route_t, cnt = _post(
        xp.reshape(Tp, D), od_p.reshape(Tp, DN_WIDTH), om_p.reshape(Tp, ATT_WIDTH),
        _pad_rows(xs_.reshape(Ts, D), POST_ROWS), _pad_rows(od_s, POST_ROWS), _pad_rows(om_s, POST_ROWS),
        wo[:DN_WIDTH], wo[DN_WIDTH:], ln2_w.reshape(1, D), wr_hi, wr_lo, b_router.reshape(N_EXPERTS, 1),
        tm=POST_ROWS, n_tokens=Tp + Ts)
    y_p, y_s = _moe(xn_tiles, x1_tiles, route_t, cnt, Tp, Tp + Ts, w_gate, b_gate, w_up, b_up, w_down, b_down)
    return (y_p.reshape(B, S, D), y_s[:Ts].reshape(DB, L, D), mk_p, mv_p, conv_p, ssm_p,
            heads(mk_s), heads(mv_s), conv_s_new, ssm_s_new)


def kernel(x_prompt, x_sample, cache_k, cache_v, state_conv, state_ssm, page_table, ln1_w, w_in, w_conv,
           a_log, dt_bias, dn_norm_w, q_norm_w, k_norm_w, w_o, ln2_w, w_router, b_router, w_gate, b_gate,
           w_up, b_up, w_down, b_down):
    weights = (ln1_w, w_in, w_conv, a_log, dt_bias, dn_norm_w, q_norm_w, k_norm_w, w_o, ln2_w, w_router,
               b_router, w_gate, b_gate, w_up, b_up, w_down, b_down)
    depth = w_in.shape[0]
    yp, ys = x_prompt, x_sample
    outs = [[] for _ in range(8)]
    for l in range(depth):
        res = _layer(yp, ys, cache_k[l], cache_v[l], state_conv[l], state_ssm[l], page_table,
                     tuple(w[l] for w in weights))
        yp, ys = res[0], res[1]
        for acc, r in zip(outs, res[2:]):
            acc.append(r)
    return (yp, ys) + tuple(jnp.stack(o) for o in outs)
```

```python
import functools
import math

import jax
import jax.numpy as jnp
from jax import lax
from jax.experimental import pallas as pl
from jax.experimental.pallas import tpu as pltpu

HEAD_DIM = 128
DN_HEADS = 4
MOBA_HEADS = 4
DN_WIDTH = DN_HEADS * HEAD_DIM
ATT_WIDTH = MOBA_HEADS * HEAD_DIM
CONV_W = 4
CONV_DIM = 3 * DN_WIDTH
DN_CHUNK = 64
MOBA_BLOCK = 256
MOBA_TOPK = 3
N_EXPERTS = 32
TOP_K = 4
SWIGLU_LIMIT = 7.0
SWIGLU_ALPHA = 1.702
EPS = 1e-6

LANES = 128
SUBLANES = 8
VMEM_LIMIT = 56 * 1024 * 1024
NEG_BIG = -1e30

PROJ_ROWS = 512
GDN_ROWS = 256
POST_ROWS = 512
TOK_TILE = 256
MOE_ROWS = 512
ROUTE_ROWS = 16

_HI = lax.Precision.HIGHEST
_F32 = jnp.float32
_BF16 = jnp.bfloat16


def _dot(a, b, precision=None):
    return jnp.dot(a, b, preferred_element_type=_F32, precision=precision)


def _dot_nt(a, b, precision=None):
    return lax.dot_general(a, b, (((1,), (1,)), ((), ())),
                           preferred_element_type=_F32, precision=precision)


def _dot_tn(a, b, precision=None):
    return lax.dot_general(a, b, (((0,), (0,)), ((), ())),
                           preferred_element_type=_F32, precision=precision)


def _split2(x):
    hi = x.astype(_BF16)
    return hi, (x - hi.astype(_F32)).astype(_BF16)


def _split3(x):
    hi = x.astype(_BF16)
    r = x - hi.astype(_F32)
    mid = r.astype(_BF16)
    return hi, mid, (r - mid.astype(_F32)).astype(_BF16)


def _dot3(a, b, dot=_dot):
    return dot(a[0], b[0]) + dot(a[1], b[0]) + dot(a[0], b[1])


def _rms(x, w):
    return x * lax.rsqrt(jnp.mean(x * x, axis=-1, keepdims=True) + EPS) * w


def _store_token_tiles(ref, x):
    n = x.shape[0]
    for j in range(SUBLANES):
        ref[pl.ds(j, n, stride=SUBLANES), :] = x[:, j * LANES:(j + 1) * LANES]


def _load_token_tiles(ref, n, j):
    return ref[pl.ds(j, n, stride=SUBLANES), :]


def _silu(x):
    return x * jax.nn.sigmoid(x)


def _params(*sem):
    return pltpu.CompilerParams(dimension_semantics=sem, vmem_limit_bytes=VMEM_LIMIT)


def _proj_kernel(x_ref, ln_ref, w_ref, qn_ref, kn_ref,
                 u_ref, z_ref, ab_ref, mq_ref, mk_ref, mv_ref, *, heads_out):
    xn = _rms(x_ref[...], ln_ref[...]).astype(_BF16)
    h = _dot(xn, w_ref[...])
    u_ref[...] = h[:, :CONV_DIM]
    z_ref[...] = h[:, CONV_DIM:CONV_DIM + DN_WIDTH]
    o = CONV_DIM + DN_WIDTH
    ab_ref[...] = h[:, o + 3 * ATT_WIDTH:]
    for hd in range(MOBA_HEADS):
        sl = slice(o + hd * HEAD_DIM, o + (hd + 1) * HEAD_DIM)
        q = _rms(h[:, sl], qn_ref[...])
        k = _rms(h[:, sl.start + ATT_WIDTH:sl.stop + ATT_WIDTH], kn_ref[...])
        v = h[:, sl.start + 2 * ATT_WIDTH:sl.stop + 2 * ATT_WIDTH]
        if heads_out:
            mq_ref[0, hd] = q
            mk_ref[0, hd] = k
            mv_ref[0, hd] = v
        else:
            hs = slice(hd * HEAD_DIM, (hd + 1) * HEAD_DIM)
            mq_ref[:, hs] = q
            mk_ref[:, hs] = k
            mv_ref[:, hs] = v


def _proj(x2d, ln_w, w_all, qn_w, kn_w, *, tm, seq=None):
    T, D = x2d.shape
    n_all = w_all.shape[1]
    grid = (T // tm,)
    row = lambda t: (t, 0)
    const = lambda t: (0, 0)
    if seq is not None:
        B, S = seq
        per = S // tm
        hshape = jax.ShapeDtypeStruct((B, MOBA_HEADS, S, HEAD_DIM), _F32)
        hspec = pl.BlockSpec((1, MOBA_HEADS, tm, HEAD_DIM), lambda t: (t // per, 0, t % per, 0))
    else:
        hshape = jax.ShapeDtypeStruct((T, ATT_WIDTH), _F32)
        hspec = pl.BlockSpec((tm, ATT_WIDTH), row)
    return pl.pallas_call(
        functools.partial(_proj_kernel, heads_out=seq is not None),
        grid=grid,
        in_specs=[pl.BlockSpec((tm, D), row), pl.BlockSpec((1, D), const),
                  pl.BlockSpec((D, n_all), const), pl.BlockSpec((1, HEAD_DIM), const),
                  pl.BlockSpec((1, HEAD_DIM), const)],
        out_specs=[pl.BlockSpec((tm, CONV_DIM), row), pl.BlockSpec((tm, DN_WIDTH), row),
                   pl.BlockSpec((tm, LANES), row), hspec, hspec, hspec],
        out_shape=[jax.ShapeDtypeStruct((T, CONV_DIM), _F32),
                   jax.ShapeDtypeStruct((T, DN_WIDTH), _F32),
                   jax.ShapeDtypeStruct((T, LANES), _F32), hshape, hshape, hshape],
        compiler_params=_params("parallel"),
        name="proj",
    )(x2d, ln_w, w_all, qn_w, kn_w)


def _gdn_kernel(u_ref, z_ref, ab_ref, conv0_ref, ssm0_ref, wconv_ref, alog_ref, dtb_ref, dnw_ref,
                od_ref, ssm_ref, up_scr, s_scr, *, lt, valid_len):
    i = pl.program_id(1)
    c = DN_CHUNK
    nc = lt // c
    heads = range(DN_HEADS)

    @pl.when(i == 0)
    def _():
        s_scr[...] = ssm0_ref[0]
        up_scr[0:SUBLANES, :] = conv0_ref[0]

    up_scr[SUBLANES:SUBLANES + lt, :] = u_ref[0]
    w = wconv_ref[...]
    base = SUBLANES - (CONV_W - 1)
    y = up_scr[base:base + lt, :] * w[0:1, :]
    for t in range(1, CONV_W):
        y = y + up_scr[base + t:base + t + lt, :] * w[t:t + 1, :]
    up_scr[0:SUBLANES, :] = up_scr[lt:lt + SUBLANES, :]
    qkv = _silu(y)

    masked = valid_len < lt
    if masked:
        rowv = lax.broadcasted_iota(jnp.int32, (lt, 1), 0) < valid_len
        qkv = jnp.where(rowv, qkv, 0.0)

    ab = ab_ref[0]
    lane = lax.broadcasted_iota(jnp.int32, (lt, LANES), 1)
    xg = ab + dtb_ref[...]
    softplus = jnp.maximum(xg, 0.0) + jnp.log(1.0 + jnp.exp(-jnp.abs(xg)))
    gb = jnp.where(lane < DN_HEADS, -jnp.exp(alog_ref[...]) * softplus, jax.nn.sigmoid(ab))
    if masked:
        gb = jnp.where(rowv, gb, 0.0)

    rows = lax.broadcasted_iota(jnp.int32, (c, c), 0)
    cols = lax.broadcasted_iota(jnp.int32, (c, c), 1)
    causal = cols <= rows
    strict = cols < rows
    eye = jnp.where(rows == cols, 1.0, 0.0)
    tril16 = jnp.where(causal, 1.0, 0.0).astype(_BF16)
    prow = lax.broadcasted_iota(jnp.int32, (DN_HEADS * c, LANES), 0) // c
    plane = lax.broadcasted_iota(jnp.int32, (DN_HEADS * c, LANES), 1)
    pick16 = jnp.where(prow == plane, 1.0, 0.0).astype(_BF16)

    cum, cum_t = [], []
    for ci in range(nc):
        g3 = _split3(gb[ci * c:(ci + 1) * c, :])
        cm = _dot(tril16, g3[0]) + _dot(tril16, g3[1]) + _dot(tril16, g3[2])
        c3 = _split3(cm)
        cum.append(cm)
        cum_t.append(_dot_nt(pick16, c3[0]) + _dot_nt(pick16, c3[1]) + _dot_nt(pick16, c3[2]))
    probs = [(ci, hd) for ci in range(nc) for hd in heads]

    def rows_of(ci):
        return slice(ci * c, (ci + 1) * c)

    q_l, k_l, v_l = [], [], []
    for ci, hd in probs:
        q = qkv[rows_of(ci), hd * HEAD_DIM:(hd + 1) * HEAD_DIM]
        k = qkv[rows_of(ci), DN_WIDTH + hd * HEAD_DIM:DN_WIDTH + (hd + 1) * HEAD_DIM]
        q_l.append(q * lax.rsqrt(jnp.sum(q * q, axis=-1, keepdims=True) + EPS) * (HEAD_DIM ** -0.5))
        k_l.append(k * lax.rsqrt(jnp.sum(k * k, axis=-1, keepdims=True) + EPS))
        v_l.append(qkv[rows_of(ci), 2 * DN_WIDTH + hd * HEAD_DIM:2 * DN_WIDTH + (hd + 1) * HEAD_DIM])
    gcum_l = [cum[ci][:, hd:hd + 1] for ci, hd in probs]
    beta_l = [gb[rows_of(ci), DN_HEADS + hd:DN_HEADS + hd + 1] for ci, hd in probs]
    decay_l = [jnp.exp(jnp.where(causal, g - cum_t[ci][hd * c:(hd + 1) * c, :], -jnp.inf))
               for (ci, hd), g in zip(probs, gcum_l)]
    k16_l = [k.astype(_BF16) for k in k_l]
    kb_l = [k * b for k, b in zip(k_l, beta_l)]
    low_l = [jnp.where(strict, _dot_nt(kb.astype(_BF16), k16) * d, 0.0)
             for kb, k16, d in zip(kb_l, k16_l, decay_l)]
    attn16_l = [(_dot_nt(q.astype(_BF16), k16) * d).astype(_BF16)
                for q, k16, d in zip(q_l, k16_l, decay_l)]

    low2_l = [_split2(l) for l in low_l]
    x_l = [eye - l for l in low_l]
    m_l = [_dot3(l2, l2) for l2 in low2_l]
    n_fac = max(1, int(math.ceil(math.log2(c)))) - 1
    for f in range(n_fac):
        m2_l = [_split2(m) for m in m_l]
        x_l = [x + _dot3(_split2(x), m2) for x, m2 in zip(x_l, m2_l)]
        if f + 1 < n_fac:
            m_l = [_dot3(m2, m2) for m2 in m2_l]
    eg_l = [jnp.exp(g) for g in gcum_l]
    uw_l = [_dot3(_split2(x), _split2(jnp.concatenate([v * b, kb * eg], axis=1)))
            for x, v, b, kb, eg in zip(x_l, v_l, beta_l, kb_l, eg_l)]
    u_l = [uw[:, :HEAD_DIM] for uw in uw_l]
    w16_l = [uw[:, HEAD_DIM:].astype(_BF16) for uw in uw_l]
    qe16_l = [(q * eg).astype(_BF16) for q, eg in zip(q_l, eg_l)]
    glast_l = [g[c - 1:c, :] for g in gcum_l]
    kdec16_l = [(k * jnp.exp(gl - g)).astype(_BF16) for k, gl, g in zip(k_l, glast_l, gcum_l)]
    eglast_l = [jnp.exp(gl) for gl in glast_l]

    state = [s_scr[hd] for hd in heads]
    for ci in range(nc):
        p0 = ci * DN_HEADS
        s16 = [s.astype(_BF16) for s in state]
        vnew = [u_l[p0 + hd] - _dot(w16_l[p0 + hd], s16[hd]) for hd in heads]
        vnew16 = [v.astype(_BF16) for v in vnew]
        o = [_dot(qe16_l[p0 + hd], s16[hd]) + _dot(attn16_l[p0 + hd], vnew16[hd]) for hd in heads]
        state = [state[hd] * eglast_l[p0 + hd] + _dot_tn(kdec16_l[p0 + hd], vnew16[hd]) for hd in heads]
        for hd in heads:
            zz = z_ref[0, rows_of(ci), hd * HEAD_DIM:(hd + 1) * HEAD_DIM]
            od_ref[0, rows_of(ci), hd * HEAD_DIM:(hd + 1) * HEAD_DIM] = (
                _rms(o[hd], dnw_ref[...]) * _silu(zz)).astype(_BF16)
    for hd in heads:
        s_scr[hd] = state[hd]
    ssm_ref[0] = s_scr[...]


def _gdn(u, z, ab, conv0, ssm0, w_conv, alog_row, dtb_row, dn_w, *, lt, valid_len):
    B, L, _ = u.shape
    grid = (B, L // lt)
    tile = lambda b, i: (b, i, 0)
    perb3 = lambda b, i: (b, 0, 0)
    const = lambda b, i: (0, 0)
    return pl.pallas_call(
        functools.partial(_gdn_kernel, lt=lt, valid_len=valid_len),
        grid=grid,
        in_specs=[pl.BlockSpec((1, lt, CONV_DIM), tile), pl.BlockSpec((1, lt, DN_WIDTH), tile),
                  pl.BlockSpec((1, lt, LANES), tile), pl.BlockSpec((1, SUBLANES, CONV_DIM), perb3),
                  pl.BlockSpec((1, DN_HEADS, HEAD_DIM, HEAD_DIM), lambda b, i: (b, 0, 0, 0)),
                  pl.BlockSpec((CONV_W, CONV_DIM), const), pl.BlockSpec((1, LANES), const),
                  pl.BlockSpec((1, LANES), const), pl.BlockSpec((1, HEAD_DIM), const)],
        out_specs=[pl.BlockSpec((1, lt, DN_WIDTH), tile),
                   pl.BlockSpec((1, DN_HEADS, HEAD_DIM, HEAD_DIM), lambda b, i: (b, 0, 0, 0))],
        out_shape=[jax.ShapeDtypeStruct((B, L, DN_WIDTH), _BF16),
                   jax.ShapeDtypeStruct((B, DN_HEADS, HEAD_DIM, HEAD_DIM), _F32)],
        scratch_shapes=[pltpu.VMEM((lt + 2 * SUBLANES, CONV_DIM), _F32),
                        pltpu.VMEM((DN_HEADS, HEAD_DIM, HEAD_DIM), _F32)],
        compiler_params=_params("parallel", "arbitrary"),
        name="gdn",
    )(u, z, ab, conv0, ssm0, w_conv, alog_row, dtb_row, dn_w)


def _topk_rows(g, n_sel):
    r = g.shape[0]
    row = lax.broadcasted_iota(jnp.int32, g.shape, 0)
    sel = jnp.zeros(g.shape, _F32)
    for _ in range(n_sel):
        m = jnp.max(g, axis=0, keepdims=True)
        idx = jnp.min(jnp.where(g == m, row, r), axis=0, keepdims=True)
        pick = row == idx
        sel = jnp.where(pick, 1.0, sel)
        g = jnp.where(pick, -jnp.inf, g)
    return sel


def _moba_prompt_kernel(q_ref, k_ref, v_ref, c_ref, o_ref, ps_ref, kb_scr, vt_scr, kmean_scr, sel_scr,
                        *, nblk):
    qt = pl.program_id(2)
    blk = MOBA_BLOCK
    scale = HEAD_DIM ** -0.5

    ps_ref[...] = jnp.sum(c_ref[...], axis=1)

    @pl.when(qt == 0)
    def _():
        kmean_scr[...] = jnp.zeros(kmean_scr.shape, _F32)
        for j in range(nblk):
            kj = k_ref[0, 0, j * blk:(j + 1) * blk, :]
            kb_scr[j] = kj.astype(_BF16)
            kmean_scr[j:j + 1, :] = jnp.mean(kj, axis=0, keepdims=True)
            vt_scr[j] = v_ref[0, 0, j * blk:(j + 1) * blk, :].T.astype(_BF16)

    q = q_ref[0, 0]
    gate = _dot_nt(kmean_scr[...], q, _HI)
    brow = lax.broadcasted_iota(jnp.int32, gate.shape, 0)
    valid = brow < qt
    sel = _topk_rows(jnp.where(valid, gate, -jnp.inf), min(MOBA_TOPK, nblk))
    sel_scr[...] = jnp.where(valid, sel, 0.0)

    q16 = (q * scale).astype(_BF16)
    kpos = lax.broadcasted_iota(jnp.int32, (blk, blk), 0)
    qpos = lax.broadcasted_iota(jnp.int32, (blk, blk), 1)
    def scores(j):
        return jnp.where(sel_scr[pl.ds(j, 1), :] > 0.0, _dot_nt(kb_scr[j], q16), NEG_BIG)

    s = jnp.where(kpos <= qpos, _dot_nt(kb_scr[qt], q16), NEG_BIG)
    m = jnp.max(s, axis=0, keepdims=True)
    p = jnp.exp(s - m)
    l = jnp.sum(p, axis=0, keepdims=True)

    def body(j, carry):
        s_cur, p_prev, j_prev, m, l, acc = carry
        s_next = scores(j + 1)
        pv = _dot(vt_scr[j_prev], p_prev)
        m_new = jnp.maximum(m, jnp.max(s_cur, axis=0, keepdims=True))
        alpha = jnp.exp(m - m_new)
        p = jnp.exp(s_cur - m_new)
        l = alpha * l + jnp.sum(p, axis=0, keepdims=True)
        return s_next, p.astype(_BF16), j, m_new, l, alpha * (acc + pv)

    init = (scores(0), p.astype(_BF16), qt, m, l, jnp.zeros((HEAD_DIM, blk), _F32))
    _, p_prev, j_prev, m, l, acc = lax.fori_loop(0, qt, body, init)
    acc = acc + _dot(vt_scr[j_prev], p_prev)
    o_ref[0] = (acc / l).T.astype(_BF16)


def _moba_prompt(q, k, v, slabs):
    B, H, S, dh = q.shape
    nblk = S // MOBA_BLOCK
    nb8 = -(-nblk // SUBLANES) * SUBLANES
    assert S % MOBA_BLOCK == 0
    n_slabs, ps, _ = slabs.shape
    steps = B * H * nblk
    share = n_slabs // steps
    assert share * steps == n_slabs and share % SUBLANES == 0, "key cache does not split evenly over the grid"
    full = lambda b, h, t: (b, h, 0, 0)
    flat = lambda b, h, t: (b * H + h) * nblk + t
    return pl.pallas_call(
        functools.partial(_moba_prompt_kernel, nblk=nblk),
        grid=(B, H, nblk),
        in_specs=[pl.BlockSpec((1, 1, MOBA_BLOCK, dh), lambda b, h, t: (b, h, t, 0)),
                  pl.BlockSpec((1, 1, S, dh), full), pl.BlockSpec((1, 1, S, dh), full),
                  pl.BlockSpec((share, ps, dh), lambda b, h, t: (flat(b, h, t), 0, 0))],
        out_specs=[pl.BlockSpec((1, MOBA_BLOCK, dh), lambda b, h, t: (b, t, h)),
                   pl.BlockSpec((share, dh), lambda b, h, t: (flat(b, h, t), 0))],
        out_shape=[jax.ShapeDtypeStruct((B, S, H * dh), _BF16),
                   jax.ShapeDtypeStruct((n_slabs, dh), _F32)],
        scratch_shapes=[pltpu.VMEM((nblk, MOBA_BLOCK, dh), _BF16),
                        pltpu.VMEM((nblk, dh, MOBA_BLOCK), _BF16),
                        pltpu.VMEM((nb8, dh), _F32),
                        pltpu.VMEM((nb8, MOBA_BLOCK), _F32)],
        compiler_params=_params("parallel", "parallel", "arbitrary"),
        name="moba_prompt",
    )(q, k, v, slabs)


def _moba_sel_kernel(pt_ref, psum_ref, q_ref, sel_ref, km_scr, *, n_pages, ppb, n_blk, n_sel, rows_per_blk):
    b = pl.program_id(0)
    km_scr[...] = jnp.zeros(km_scr.shape, _F32)

    def fill(j, carry):
        acc = psum_ref[pl.ds(pt_ref[b * n_pages + j * ppb], 1), :]
        for r in range(1, ppb):
            acc = acc + psum_ref[pl.ds(pt_ref[b * n_pages + j * ppb + r], 1), :]
        km_scr[pl.ds(j, 1), :] = acc * (1.0 / rows_per_blk)
        return carry

    lax.fori_loop(0, n_blk, fill, 0)
    lane = lax.broadcasted_iota(jnp.int32, (SUBLANES, LANES), 1)
    for hd in range(MOBA_HEADS):
        g = _dot_nt(q_ref[0, hd], km_scr[:, hd * HEAD_DIM:(hd + 1) * HEAD_DIM], _HI)
        g = jnp.where(lane < n_blk, g, -jnp.inf)
        out = jnp.zeros((SUBLANES, LANES), jnp.int32)
        for r in range(n_sel):
            m = jnp.max(g, axis=1, keepdims=True)
            idx = jnp.min(jnp.where(g == m, lane, LANES), axis=1, keepdims=True)
            out = jnp.where(lane == r, idx, out)
            g = jnp.where(lane == idx, -jnp.inf, g)
        sel_ref[0, hd] = out


def _moba_sel(page_table, psum, q8, *, ppb, n_sel, rows_per_blk):
    DB, n_pages = page_table.shape
    n_blk = n_pages // ppb
    assert n_blk <= LANES
    n_phys, width = psum.shape
    return pl.pallas_call(
        functools.partial(_moba_sel_kernel, n_pages=n_pages, ppb=ppb, n_blk=n_blk, n_sel=n_sel,
                          rows_per_blk=rows_per_blk),
        grid_spec=pltpu.PrefetchScalarGridSpec(
            num_scalar_prefetch=1,
            grid=(DB,),
            in_specs=[pl.BlockSpec((n_phys, width), lambda b, pt: (0, 0)),
                      pl.BlockSpec((1, MOBA_HEADS, SUBLANES, HEAD_DIM), lambda b, pt: (b, 0, 0, 0))],
            out_specs=pl.BlockSpec((1, MOBA_HEADS, SUBLANES, LANES), lambda b, pt: (b, 0, 0, 0)),
            scratch_shapes=[pltpu.VMEM((LANES, width), _F32)]),
        out_shape=jax.ShapeDtypeStruct((DB, MOBA_HEADS, SUBLANES, LANES), jnp.int32),
        compiler_params=_params("arbitrary"),
        name="moba_sel",
    )(page_table.reshape(-1), psum, q8)


def _moba_sample_kernel(phys_ref, q_ref, kn_ref, vn_ref, ck_hbm, cv_hbm, o_ref, kbuf, vbuf, sem,
                        *, n_pg, n_new, n_heads, n_steps):
    step = pl.program_id(0)
    per = n_new * n_pg
    scale = HEAD_DIM ** -0.5

    def page_copies(st, slot):
        hd = st % n_heads
        cps = []
        for j in range(per):
            page = phys_ref[st * per + j]
            cps.append(pltpu.make_async_copy(ck_hbm.at[page, hd], kbuf.at[slot, j], sem.at[slot]))
            cps.append(pltpu.make_async_copy(cv_hbm.at[page, hd], vbuf.at[slot, j], sem.at[slot]))
        return cps

    def attend(slot):
        q = q_ref[0, 0]
        q16 = q.astype(_BF16)
        kn = kn_ref[0, 0]
        vn = vn_ref[0, 0]
        rowq = lax.broadcasted_iota(jnp.int32, (SUBLANES, 1), 0)
        s_own = []
        for c in range(n_new):
            sc = jnp.sum(q * kn[c:c + 1, :], axis=-1, keepdims=True) * scale
            s_own.append(jnp.where(rowq >= c, sc, NEG_BIG))
        m_own = s_own[0]
        for sc in s_own[1:]:
            m_own = jnp.maximum(m_own, sc)
        out = jnp.zeros((SUBLANES, HEAD_DIM), _F32)
        for t in range(n_new):
            pages = range(t * n_pg, (t + 1) * n_pg)
            s_past = [_dot_nt(q16, kbuf[slot, j].astype(_BF16)) * scale for j in pages]
            m = m_own
            for sp in s_past:
                m = jnp.maximum(m, jnp.max(sp, axis=-1, keepdims=True))
            l = jnp.zeros((SUBLANES, 1), _F32)
            acc = jnp.zeros((SUBLANES, HEAD_DIM), _F32)
            for sp, j in zip(s_past, pages):
                p = jnp.exp(sp - m)
                l = l + jnp.sum(p, axis=-1, keepdims=True)
                acc = acc + _dot(p.astype(_BF16), vbuf[slot, j].astype(_BF16))
            for c, sc in enumerate(s_own):
                p = jnp.exp(sc - m)
                l = l + p
                acc = acc + p * vn[c:c + 1, :]
            out = jnp.where(rowq == t, acc / l, out)
        o_ref[0, 0] = out

    def run(slot):
        if slot == 0:
            @pl.when(step == 0)
            def _():
                for cp in page_copies(0, 0):
                    cp.start()

        @pl.when(step + 1 < n_steps)
        def _():
            for cp in page_copies(step + 1, 1 - slot):
                cp.start()

        for cp in page_copies(step, slot):
            cp.wait()
        attend(slot)

    for slot in range(2):
        @pl.when(step % 2 == slot)
        def _():
            run(slot)


def _moba_sample(phys, q8, kn8, vn8, cache_k, cache_v, *, n_new):
    DB, H, _, dh = q8.shape
    ps = cache_k.shape[2]
    n_pg = phys.shape[0] // (DB * H * n_new)
    steps = DB * H
    small = pl.BlockSpec((1, 1, SUBLANES, dh), lambda s, ph: (s // H, s % H, 0, 0))
    return pl.pallas_call(
        functools.partial(_moba_sample_kernel, n_pg=n_pg, n_new=n_new, n_heads=H, n_steps=steps),
        grid_spec=pltpu.PrefetchScalarGridSpec(
            num_scalar_prefetch=1,
            grid=(steps,),
            in_specs=[small, small, small, pl.BlockSpec(memory_space=pl.ANY),
                      pl.BlockSpec(memory_space=pl.ANY)],
            out_specs=small,
            scratch_shapes=[pltpu.VMEM((2, n_new * n_pg, ps, dh), _F32),
                            pltpu.VMEM((2, n_new * n_pg, ps, dh), _F32),
                            pltpu.SemaphoreType.DMA((2,))]),
        out_shape=jax.ShapeDtypeStruct((DB, H, SUBLANES, dh), _F32),
        compiler_params=_params("arbitrary"),
        name="moba_sample",
    )(phys, q8, kn8, vn8, cache_k, cache_v)


def _post_kernel(xp_ref, odp_ref, omp_ref, xs_ref, ods_ref, oms_ref, wo1_ref, wo2_ref, ln_ref,
                 wrh_ref, wrl_ref, br_ref, x1_ref, xn_ref, route_ref, cnt_ref, carry_scr,
                 *, tm, n_prompt_tiles, n_tokens):
    i = pl.program_id(0)

    @pl.when(i == 0)
    def _():
        carry_scr[...] = jnp.zeros(carry_scr.shape, _F32)

    is_p = i < n_prompt_tiles
    x = jnp.where(is_p, xp_ref[...], xs_ref[...])
    od = jnp.where(is_p, odp_ref[...], ods_ref[...])
    om = jnp.where(is_p, omp_ref[...], oms_ref[...])
    x1 = x + _dot(od, wo1_ref[...]) + _dot(om, wo2_ref[...])
    _store_token_tiles(x1_ref, x1)
    xn = _rms(x1, ln_ref[...])
    _store_token_tiles(xn_ref, xn)
    xh, xl = _split2(xn)
    logits = (_dot_nt(wrh_ref[...], xh) + _dot_nt(wrh_ref[...], xl) + _dot_nt(wrl_ref[...], xh)) + br_ref[...]

    erow = lax.broadcasted_iota(jnp.int32, (N_EXPERTS, tm), 0)
    g = logits
    vals, picks = [], []
    for _ in range(TOP_K):
        m = jnp.max(g, axis=0, keepdims=True)
        idx = jnp.min(jnp.where(g == m, erow, N_EXPERTS), axis=0, keepdims=True)
        pick = erow == idx
        vals.append(m)
        picks.append((idx, pick))
        g = jnp.where(pick, -jnp.inf, g)
    es = [jnp.exp(v - vals[0]) for v in vals]
    den = es[0]
    for e in es[1:]:
        den = den + e

    rows = lax.broadcasted_iota(jnp.int32, (tm, tm), 0)
    cols = lax.broadcasted_iota(jnp.int32, (tm, tm), 1)
    earlier = jnp.where(rows < cols, 1.0, 0.0).astype(_BF16)
    real = jnp.where((i * tm + lax.broadcasted_iota(jnp.int32, (1, tm), 1)) < n_tokens, 1.0, 0.0)
    base = carry_scr[...]
    tok = lax.broadcasted_iota(jnp.int32, (1, tm), 1)
    ranks = [jnp.zeros((1, tm), _F32) for _ in picks]
    for sub in range(tm // TOK_TILE):
        in_sub = jnp.where((tok >= sub * TOK_TILE) & (tok < (sub + 1) * TOK_TILE), real, 0.0)
        for k, (idx, pick) in enumerate(picks):
            onehot = jnp.where(pick, in_sub, 0.0)
            pref = _dot(onehot.astype(_BF16), earlier) + base
            ranks[k] = ranks[k] + jnp.sum(onehot * pref, axis=0, keepdims=True)
            base = base + jnp.sum(onehot, axis=1, keepdims=True)
    carry_scr[...] = base
    cnt_ref[...] = base
    record = ([idx.astype(_F32) for idx, _ in picks] + [e / den for e in es] + ranks
              + [jnp.zeros((route_ref.shape[0] - 3 * TOP_K, tm), _F32)])
    route_ref[...] = jnp.concatenate(record, axis=0)


def _post(xp, odp, omp, xs, ods, oms, wo1, wo2, ln_w, wr_hi, wr_lo, b_r, *, tm, n_tokens):
    Tp, D = xp.shape
    npt = Tp // tm
    T = Tp + tm
    prow = lambda t: (jnp.minimum(t, npt - 1), 0)
    srow = lambda t: (0, 0)
    row = lambda t: (t, 0)
    const = lambda t: (0, 0)
    return pl.pallas_call(
        functools.partial(_post_kernel, tm=tm, n_prompt_tiles=npt, n_tokens=n_tokens),
        grid=(npt + 1,),
        in_specs=[pl.BlockSpec((tm, D), prow), pl.BlockSpec((tm, DN_WIDTH), prow),
                  pl.BlockSpec((tm, ATT_WIDTH), prow),
                  pl.BlockSpec((tm, D), srow), pl.BlockSpec((tm, DN_WIDTH), srow),
                  pl.BlockSpec((tm, ATT_WIDTH), srow),
                  pl.BlockSpec((DN_WIDTH, D), const), pl.BlockSpec((ATT_WIDTH, D), const),
                  pl.BlockSpec((1, D), const), pl.BlockSpec((N_EXPERTS, D), const),
                  pl.BlockSpec((N_EXPERTS, D), const), pl.BlockSpec((N_EXPERTS, 1), const)],
        out_specs=[pl.BlockSpec((tm * SUBLANES, LANES), row), pl.BlockSpec((tm * SUBLANES, LANES), row),
                   pl.BlockSpec((ROUTE_ROWS, tm), lambda t: (0, t)), pl.BlockSpec((N_EXPERTS, 1), const)],
        out_shape=[jax.ShapeDtypeStruct((T * SUBLANES, LANES), _F32),
                   jax.ShapeDtypeStruct((T * SUBLANES, LANES), _F32),
                   jax.ShapeDtypeStruct((ROUTE_ROWS, T), _F32), jax.ShapeDtypeStruct((N_EXPERTS, 1), _F32)],
        scratch_shapes=[pltpu.VMEM((N_EXPERTS, 1), _F32)],
        compiler_params=_params("arbitrary"),
        name="post",
    )(xp, odp, omp, xs, ods, oms, wo1, wo2, ln_w, wr_hi, wr_lo, b_r)


RUN_PIECES = tuple(1 << b for b in reversed(range(TOK_TILE.bit_length())))
RUN_RARE = 64


def _all_run_copies(make_copy, start_ref, len_ref, tile):
    runs = []
    pool_start = jnp.int32(0)
    for e in range(N_EXPERTS):
        n = len_ref[tile * N_EXPERTS + e]
        first = start_ref[tile * N_EXPERTS + e]
        pieces = []
        for piece in RUN_PIECES:
            done = n & (-2 * piece)
            pieces.append((piece, make_copy(first + done, pool_start + done, piece)))
        runs.append((n, pieces))
        pool_start = pool_start + n
    return runs


def _for_pieces(runs, act):
    for n, pieces in runs:
        def some(which):
            for piece, cp in pieces:
                if which(piece):
                    @pl.when((n & piece) != 0)
                    def _():
                        act(cp)

        @pl.when(n >= RUN_RARE)
        def _():
            some(lambda piece: piece >= RUN_RARE)

        some(lambda piece: piece < RUN_RARE)


def _start_all(runs):
    _for_pieces(runs, lambda cp: cp.start())


def _wait_all(runs):
    _for_pieces(runs, lambda cp: cp.wait())


def _rows(ref, first_row, n_rows):
    return ref.at[pl.ds(pl.multiple_of(first_row * SUBLANES, SUBLANES), n_rows * SUBLANES)]


def _dispatch_kernel(start_ref, len_ref, tail_ref, nu_ref, slot_hbm, xn_ref, xs_hbm, idx_smem0, idx_smem1, pool, zero_scr,
                     sem_idx, sem_zero, sem_rows, *, tm, n_blocks, n_tiles):
    i = pl.program_id(0)
    n_asg = tm * TOP_K
    idx_smem = (idx_smem0, idx_smem1)

    def idx_copy(tile, slot):
        return pltpu.make_async_copy(slot_hbm.at[pl.ds(tile * n_asg, n_asg)], idx_smem[slot], sem_idx.at[slot])

    @pl.when(i == 0)
    def _():
        zero_scr[...] = jnp.zeros(zero_scr.shape, _F32)

        def clear(first_row):
            first = pl.multiple_of(first_row * SUBLANES, MOE_ROWS * SUBLANES)
            cp = pltpu.make_async_copy(zero_scr, xs_hbm.at[pl.ds(first, MOE_ROWS * SUBLANES)], sem_zero)
            cp.start()
            cp.wait()

        for e in range(N_EXPERTS):
            @pl.when(tail_ref[e] >= 0)
            def _():
                clear(tail_ref[e])

        def clear_block(b, carry):
            clear(b * MOE_ROWS)
            return carry

        lax.fori_loop(nu_ref[0], n_blocks, clear_block, 0)

    def run_copies(tile, slot):
        return _all_run_copies(
            lambda hbm_row, pool_row, n: pltpu.make_async_copy(
                _rows(pool.at[slot], pool_row, n), _rows(xs_hbm, hbm_row, n), sem_rows.at[slot]),
            start_ref, len_ref, tile)

    def step(slot):
        if slot == 0:
            @pl.when(i == 0)
            def _():
                idx_copy(0, 0).start()

        @pl.when(i + 1 < n_tiles)
        def _():
            idx_copy(i + 1, 1 - slot).start()

        idx_copy(i, slot).wait()

        @pl.when(i >= 2)
        def _():
            _wait_all(run_copies(i - 2, slot))

        def pack(t, carry):
            tile = xn_ref[pl.ds(pl.multiple_of(t * SUBLANES, SUBLANES), SUBLANES), :]
            for k in range(TOP_K):
                row = pl.multiple_of(idx_smem[slot][k * tm + t] * SUBLANES, SUBLANES)
                pool[slot, pl.ds(row, SUBLANES), :] = tile
            return carry

        lax.fori_loop(0, tm, pack, 0, unroll=4)
        _start_all(run_copies(i, slot))

        @pl.when(i == n_tiles - 1)
        def _():
            _wait_all(run_copies(i, slot))

            @pl.when(i >= 1)
            def _():
                _wait_all(run_copies(i - 1, 1 - slot))

    for slot in range(2):
        @pl.when(i % 2 == slot)
        def _():
            step(slot)


def _dispatch(run_start, run_len, tail, n_used, slot, xn_tiles, n_tiles, n_blocks, *, tm):
    pre = lambda i, *_: (i, 0)
    return pl.pallas_call(
        functools.partial(_dispatch_kernel, tm=tm, n_blocks=n_blocks, n_tiles=n_tiles),
        grid_spec=pltpu.PrefetchScalarGridSpec(
            num_scalar_prefetch=4,
            grid=(n_tiles,),
            in_specs=[pl.BlockSpec(memory_space=pl.ANY), pl.BlockSpec((tm * SUBLANES, LANES), pre)],
            out_specs=pl.BlockSpec(memory_space=pl.ANY),
            scratch_shapes=[pltpu.SMEM((tm * TOP_K,), jnp.int32), pltpu.SMEM((tm * TOP_K,), jnp.int32),
                            pltpu.VMEM((2, (tm * TOP_K + 1) * SUBLANES, LANES), _F32),
                            pltpu.VMEM((MOE_ROWS * SUBLANES, LANES), _F32),
                            pltpu.SemaphoreType.DMA((2,)), pltpu.SemaphoreType.DMA, pltpu.SemaphoreType.DMA((2,))]),
        out_shape=jax.ShapeDtypeStruct((n_blocks * MOE_ROWS * SUBLANES, LANES), _F32),
        compiler_params=_params("arbitrary"),
        name="dispatch",
    )(run_start, run_len, tail, n_used, slot, xn_tiles)


def _moe_kernel(be_ref, nu_ref, x_ref, wg_ref, bg_ref, wu_ref, bu_ref, wd_ref, bd_ref, o_ref,
                wg_scr, wu_scr, wd_scr, x_scr):
    i = pl.program_id(0)

    @pl.when(i >= nu_ref[0])
    def _():
        o_ref[...] = jnp.zeros(o_ref.shape, _F32)

    @pl.when(i < nu_ref[0])
    def _():
        prev = be_ref[jnp.maximum(i - 1, 0)]

        @pl.when((i == 0) | (be_ref[i] != prev))
        def _():
            wg_scr[...] = wg_ref[0].astype(_BF16)
            wu_scr[...] = wu_ref[0].astype(_BF16)
            wd_scr[...] = wd_ref[0].astype(_BF16)

        for j in range(SUBLANES):
            x_scr[:, j * LANES:(j + 1) * LANES] = _load_token_tiles(x_ref, MOE_ROWS, j).astype(_BF16)
        x = x_scr[...]
        gt = jnp.minimum(_dot(x, wg_scr[...]) + bg_ref[0], SWIGLU_LIMIT)
        up = jnp.clip(_dot(x, wu_scr[...]) + bu_ref[0], -SWIGLU_LIMIT, SWIGLU_LIMIT)
        act = ((up + 1.0) * (gt * jax.nn.sigmoid(SWIGLU_ALPHA * gt))).astype(_BF16)
        _store_token_tiles(o_ref, _dot(act, wd_scr[...]) + bd_ref[0])


def _moe_ffn(block_e, n_used, xs, n_blocks, w_gate, b_gate, w_up, b_up, w_down, b_down):
    E, D, F = w_gate.shape
    assert D == SUBLANES * LANES
    rows = MOE_ROWS * SUBLANES
    xblk = lambda i, be, nu: (jnp.minimum(i, nu[0] - 1), 0)
    wsel = lambda i, be, nu: (be[jnp.minimum(i, nu[0] - 1)], 0, 0)
    return pl.pallas_call(
        _moe_kernel,
        grid_spec=pltpu.PrefetchScalarGridSpec(
            num_scalar_prefetch=2,
            grid=(n_blocks,),
            in_specs=[pl.BlockSpec((rows, LANES), xblk),
                      pl.BlockSpec((1, D, F), wsel), pl.BlockSpec((1, 1, F), wsel),
                      pl.BlockSpec((1, D, F), wsel), pl.BlockSpec((1, 1, F), wsel),
                      pl.BlockSpec((1, F, D), wsel), pl.BlockSpec((1, 1, D), wsel)],
            out_specs=pl.BlockSpec((rows, LANES), lambda i, be, nu: (i, 0)),
            scratch_shapes=[pltpu.VMEM((D, F), _BF16), pltpu.VMEM((D, F), _BF16),
                            pltpu.VMEM((F, D), _BF16), pltpu.VMEM((MOE_ROWS, D), _BF16)]),
        out_shape=jax.ShapeDtypeStruct((n_blocks * rows, LANES), _F32),
        compiler_params=_params("arbitrary"),
        name="moe_ffn",
    )(block_e, n_used, xs, w_gate, b_gate.reshape(E, 1, F), w_up, b_up.reshape(E, 1, F),
      w_down, b_down.reshape(E, 1, D))


def _combine_kernel(start_ref, len_ref, slot_hbm, gate_hbm, ys_hbm, x1_ref, yp_ref, yr_ref,
                    idx_smem0, idx_smem1, gate_smem0, gate_smem1, pool, out_scr, sem_idx, sem_rows,
                    *, tm, n_prompt_tiles, n_tiles):
    i = pl.program_id(0)
    n_asg = tm * TOP_K
    idx_smem = (idx_smem0, idx_smem1)
    gate_smem = (gate_smem0, gate_smem1)

    def run_copies(tile, slot):
        return _all_run_copies(
            lambda hbm_row, pool_row, n: pltpu.make_async_copy(
                _rows(ys_hbm, hbm_row, n), _rows(pool.at[slot], pool_row, n), sem_rows.at[slot]),
            start_ref, len_ref, tile)

    def table_copies(tile, slot):
        return [pltpu.make_async_copy(slot_hbm.at[pl.ds(tile * n_asg, n_asg)], idx_smem[slot], sem_idx.at[slot]),
                pltpu.make_async_copy(gate_hbm.at[pl.ds(tile * n_asg, n_asg)], gate_smem[slot], sem_idx.at[slot])]

    def fetch(tile, slot):
        for cp in table_copies(tile, slot):
            cp.start()
        _start_all(run_copies(tile, slot))

    def step(slot):
        if slot == 0:
            @pl.when(i == 0)
            def _():
                fetch(0, 0)

        @pl.when(i + 1 < n_tiles)
        def _():
            fetch(i + 1, 1 - slot)

        for cp in table_copies(i, slot):
            cp.wait()
        _wait_all(run_copies(i, slot))

        def token(t, carry):
            at = pl.ds(pl.multiple_of(t * SUBLANES, SUBLANES), SUBLANES)
            acc = x1_ref[at, :]
            for k in range(TOP_K):
                row = pl.multiple_of(idx_smem[slot][k * tm + t] * SUBLANES, SUBLANES)
                acc = acc + gate_smem[slot][k * tm + t] * pool[slot, pl.ds(row, SUBLANES), :]
            out_scr[at, :] = acc
            return carry

        lax.fori_loop(0, tm, token, 0, unroll=4)

        def emit(o_ref):
            for j in range(SUBLANES):
                o_ref[:, j * LANES:(j + 1) * LANES] = _load_token_tiles(out_scr, tm, j)

        @pl.when(i < n_prompt_tiles)
        def _():
            emit(yp_ref)

        @pl.when(i >= n_prompt_tiles)
        def _():
            emit(yr_ref)

    for slot in range(2):
        @pl.when(i % 2 == slot)
        def _():
            step(slot)


def _combine(run_start, run_len, slot, gates, ys, x1_tiles, n_prompt, n_tiles, *, tm):
    D = SUBLANES * LANES
    npt = n_prompt // tm
    pre = lambda i, *_: (i, 0)
    return pl.pallas_call(
        functools.partial(_combine_kernel, tm=tm, n_prompt_tiles=npt, n_tiles=n_tiles),
        grid_spec=pltpu.PrefetchScalarGridSpec(
            num_scalar_prefetch=2,
            grid=(n_tiles,),
            in_specs=[pl.BlockSpec(memory_space=pl.ANY), pl.BlockSpec(memory_space=pl.ANY),
                      pl.BlockSpec(memory_space=pl.ANY), pl.BlockSpec((tm * SUBLANES, LANES), pre)],
            out_specs=[pl.BlockSpec((tm, D), lambda i, *_: (jnp.minimum(i, npt - 1), 0)),
                       pl.BlockSpec((tm, D), lambda i, *_: (jnp.maximum(i - npt, 0), 0))],
            scratch_shapes=[pltpu.SMEM((tm * TOP_K,), jnp.int32), pltpu.SMEM((tm * TOP_K,), jnp.int32),
                            pltpu.SMEM((tm * TOP_K,), _F32), pltpu.SMEM((tm * TOP_K,), _F32),
                            pltpu.VMEM((2, tm * TOP_K * SUBLANES, LANES), _F32),
                            pltpu.VMEM((tm * SUBLANES, LANES), _F32),
                            pltpu.SemaphoreType.DMA((2,)), pltpu.SemaphoreType.DMA((2,))]),
        out_shape=[jax.ShapeDtypeStruct((n_prompt, D), _F32),
                   jax.ShapeDtypeStruct(((n_tiles - npt) * tm, D), _F32)],
        compiler_params=_params("arbitrary"),
        name="combine",
    )(run_start, run_len, slot, gates, ys, x1_tiles)


def _pad_lanes(v, fill=0.0):
    v = v.reshape(1, -1).astype(_F32)
    return jnp.pad(v, ((0, 0), (0, LANES - v.shape[1])), constant_values=fill)


def _pad_rows(a, rows):
    return jnp.pad(a, ((0, rows - a.shape[0]), (0, 0)))


def _prep_w_in(w_in):
    off_a = CONV_DIM + DN_WIDTH
    off_mq = off_a + 2 * DN_HEADS
    ab = jnp.pad(w_in[:, off_a:off_mq], ((0, 0), (0, LANES - 2 * DN_HEADS)))
    return jnp.concatenate([w_in[:, :off_a], w_in[:, off_mq:], ab], axis=1).astype(_BF16)


def _moe(xn_tiles, x1_tiles, route_t, counts, n_prompt, n_tokens, w_gate, b_gate, w_up, b_up, w_down, b_down):
    tm = TOK_TILE
    n_tiles = -(-n_tokens // tm)
    tc = n_tiles * tm
    idx = route_t[0:TOP_K, :tc].astype(jnp.int32)
    gates = route_t[TOP_K:2 * TOP_K, :tc]
    rank = route_t[2 * TOP_K:3 * TOP_K, :tc].astype(jnp.int32)
    cnt = counts[:, 0].astype(jnp.int32)
    padded = (cnt + MOE_ROWS - 1) // MOE_ROWS * MOE_ROWS
    pends = jnp.cumsum(padded)
    pstarts = pends - padded
    n_blocks = -(-n_tokens * TOP_K // MOE_ROWS) + N_EXPERTS
    real = (jnp.arange(tc, dtype=jnp.int32) < n_tokens)[None, :]
    experts = jnp.arange(N_EXPERTS, dtype=jnp.int32)[:, None, None]
    onehot = (idx[None] == experts) & real[None]
    hist = jnp.sum(onehot.reshape(N_EXPERTS, TOP_K, n_tiles, tm), axis=(1, 3), dtype=jnp.int32).T
    before_tile = jnp.cumsum(hist, axis=0) - hist
    before_expert = jnp.cumsum(hist, axis=1) - hist
    run_start = (pstarts[None, :] + before_tile).reshape(-1)
    shift = jnp.repeat((before_tile - before_expert).T, tm, axis=1)
    slot = rank - jnp.sum(jnp.where(onehot, shift[:, None, :], 0), axis=0)
    per_tile = lambda d: d.reshape(TOP_K, n_tiles, tm).transpose(1, 0, 2).reshape(-1)
    slot_pack = per_tile(jnp.where(real, slot, tm * TOP_K))
    slot_read = per_tile(jnp.where(real, slot, 0))
    block_start = jnp.arange(n_blocks, dtype=jnp.int32) * MOE_ROWS
    block_e = jnp.minimum(jnp.sum((pends[None, :] <= block_start[:, None]).astype(jnp.int32), axis=1),
                          N_EXPERTS - 1)
    n_used = (pends[-1:] // MOE_ROWS).astype(jnp.int32)
    tail = jnp.where(cnt > 0, pends - MOE_ROWS, -1).astype(jnp.int32)
    run_len = hist.reshape(-1)
    xs = _dispatch(run_start, run_len, tail, n_used, slot_pack, xn_tiles, n_tiles, n_blocks, tm=tm)
    ys = _moe_ffn(block_e, n_used, xs, n_blocks, w_gate, b_gate, w_up, b_up, w_down, b_down)
    return _combine(run_start, run_len, slot_read, per_tile(gates), ys, x1_tiles, n_prompt, n_tiles, tm=tm)


def _layer(xp, xs_, cache_k, cache_v, conv_s, ssm_s, page_table, lw):
    (ln1_w, w_in, w_conv, a_log, dt_bias, dn_norm_w, q_norm_w, k_norm_w, w_o, ln2_w, w_router,
     b_router, w_gate, b_gate, w_up, b_up, w_down, b_down) = lw
    B, S, D = xp.shape
    DB, L, _ = xs_.shape
    n_phys, H, PS, dh = cache_k.shape
    n_pages = page_table.shape[1]
    ppb = MOBA_BLOCK // PS
    Tp, Ts = B * S, DB * L
    assert (n_pages * PS) % MOBA_BLOCK == 0, "paged past must end on a MoBA block boundary"
    assert CONV_W - 1 <= L <= SUBLANES and Ts <= POST_ROWS and POST_ROWS % TOK_TILE == 0
    assert Tp % PROJ_ROWS == 0 and S % GDN_ROWS == 0 and Tp % POST_ROWS == 0
    cur = n_pages // ppb
    n_sel = min(MOBA_TOPK, cur)
    assert n_sel > 0

    w_all = _prep_w_in(w_in)
    ln1 = ln1_w.reshape(1, D)
    qn = q_norm_w.reshape(1, dh)
    kn = k_norm_w.reshape(1, dh)
    alog_row = _pad_lanes(a_log)
    dtb_row = _pad_lanes(dt_bias)
    dnw = dn_norm_w.reshape(1, dh)

    u_p, z_p, ab_p, mq_p, mk_p, mv_p = _proj(xp.reshape(Tp, D), ln1, w_all, qn, kn, tm=PROJ_ROWS, seq=(B, S))
    od_p, ssm_p = _gdn(u_p.reshape(B, S, CONV_DIM), z_p.reshape(B, S, DN_WIDTH), ab_p.reshape(B, S, LANES),
                       jnp.zeros((B, SUBLANES, CONV_DIM), _F32), jnp.zeros((B, DN_HEADS, dh, dh), _F32),
                       w_conv, alog_row, dtb_row, dnw, lt=GDN_ROWS, valid_len=GDN_ROWS)
    om_p, psum = _moba_prompt(mq_p, mk_p, mv_p, cache_k.reshape(n_phys * H, PS, dh))
    conv_p = u_p.reshape(B, S, CONV_DIM)[:, S - (CONV_W - 1):]

    u_s, z_s, ab_s, mq_s, mk_s, mv_s = _proj(xs_.reshape(Ts, D), ln1, w_all, qn, kn, tm=Ts)
    padl = lambda a: jnp.pad(a.reshape(DB, L, -1), ((0, 0), (0, DN_CHUNK - L), (0, 0)))
    conv0 = jnp.pad(conv_s, ((0, 0), (SUBLANES - (CONV_W - 1), 0), (0, 0)))
    od_s, ssm_s_new = _gdn(padl(u_s), padl(z_s), padl(ab_s), conv0, ssm_s, w_conv, alog_row, dtb_row, dnw,
                           lt=DN_CHUNK, valid_len=L)
    od_s = od_s[:, :L].reshape(Ts, DN_WIDTH)
    conv_s_new = jnp.concatenate([conv_s, u_s.reshape(DB, L, CONV_DIM)], axis=1)[:, L:]

    heads = lambda a: a.reshape(DB, L, H, dh).transpose(0, 2, 1, 3)
    pad8 = lambda a: jnp.pad(a, ((0, 0), (0, 0), (0, SUBLANES - L), (0, 0)))
    q8, k8, v8 = pad8(heads(mq_s)), pad8(heads(mk_s)), pad8(heads(mv_s))
    psum = psum.reshape(n_phys, H * dh)
    sel = _moba_sel(page_table, psum, q8, ppb=ppb, n_sel=n_sel, rows_per_blk=MOBA_BLOCK)
    sel = sel[:, :, :L, :n_sel]
    logical = sel[..., None] * ppb + jnp.arange(ppb, dtype=jnp.int32)
    phys = page_table[jnp.arange(DB)[:, None, None, None, None], logical]
    om_s = _moba_sample(phys.reshape(-1).astype(jnp.int32), q8, k8, v8, cache_k, cache_v, n_new=L)
    om_s = om_s[:, :, :L].transpose(0, 2, 1, 3).reshape(Ts, H * dh).astype(_BF16)

    wo = w_o.astype(_BF16)
    wr = w_router.T
    wr_hi = wr.astype(_BF16)
    wr_lo = (wr - wr_hi.astype(_F32)).astype(_BF16)
    x1_tiles, xn_tiles, route_t, cnt = _post(
        xp.reshape(Tp, D), od_p.reshape(Tp, DN_WIDTH), om_p.reshape(Tp, ATT_WIDTH),
        _pad_rows(xs_.reshape(Ts, D), POST_ROWS), _pad_rows(od_s, POST_ROWS), _pad_rows(om_s, POST_ROWS),
        wo[:DN_WIDTH], wo[DN_WIDTH:], ln2_w.reshape(1, D), wr_hi, wr_lo, b_router.reshape(N_EXPERTS, 1),
        tm=POST_ROWS, n_tokens=Tp + Ts)
    y_p, y_s = _moe(xn_tiles, x1_tiles, route_t, cnt, Tp, Tp + Ts, w_gate, b_gate, w_up, b_up, w_down, b_down)
    return (y_p.reshape(B, S, D), y_s[:Ts].reshape(DB, L, D), mk_p, mv_p, conv_p, ssm_p,
            heads(mk_s), heads(mv_s), conv_s_new, ssm_s_new)


def kernel(x_prompt, x_sample, cache_k, cache_v, state_conv, state_ssm, page_table, ln1_w, w_in, w_conv,
           a_log, dt_bias, dn_norm_w, q_norm_w, k_norm_w, w_o, ln2_w, w_router, b_router, w_gate, b_gate,
           w_up, b_up, w_down, b_down):
    weights = (ln1_w, w_in, w_conv, a_log, dt_bias, dn_norm_w, q_norm_w, k_norm_w, w_o, ln2_w, w_router,
               b_router, w_gate, b_gate, w_up, b_up, w_down, b_down)
    depth = w_in.shape[0]
    yp, ys = x_prompt, x_sample
    outs = [[] for _ in range(8)]
    for l in range(depth):
        res = _layer(yp, ys, cache_k[l], cache_v[l], state_conv[l], state_ssm[l], page_table,
                     tuple(w[l] for w in weights))
        yp, ys = res[0], res[1]
        for acc, r in zip(outs, res[2:]):
            acc.append(r)
    return (yp, ys) + tuple(jnp.stack(o) for o in outs)
```

```python
import functools
import math

import jax
import jax.numpy as jnp
from jax import lax
from jax.experimental import pallas as pl
from jax.experimental.pallas import tpu as pltpu

HEAD_DIM = 128
DN_HEADS = 4
MOBA_HEADS = 4
DN_WIDTH = DN_HEADS * HEAD_DIM
ATT_WIDTH = MOBA_HEADS * HEAD_DIM
CONV_W = 4
CONV_DIM = 3 * DN_WIDTH
DN_CHUNK = 64
MOBA_BLOCK = 256
MOBA_TOPK = 3
N_EXPERTS = 32
TOP_K = 4
SWIGLU_LIMIT = 7.0
SWIGLU_ALPHA = 1.702
EPS = 1e-6

LANES = 128
SUBLANES = 8
VMEM_LIMIT = 56 * 1024 * 1024
NEG_BIG = -1e30

PROJ_ROWS = 512
GDN_ROWS = 256
POST_ROWS = 512
TOK_TILE = 256
MOE_ROWS = 512
ROUTE_ROWS = 16

_HI = lax.Precision.HIGHEST
_F32 = jnp.float32
_BF16 = jnp.bfloat16


def _dot(a, b, precision=None):
    return jnp.dot(a, b, preferred_element_type=_F32, precision=precision)


def _dot_nt(a, b, precision=None):
    return lax.dot_general(a, b, (((1,), (1,)), ((), ())),
                           preferred_element_type=_F32, precision=precision)


def _dot_tn(a, b, precision=None):
    return lax.dot_general(a, b, (((0,), (0,)), ((), ())),
                           preferred_element_type=_F32, precision=precision)


def _split2(x):
    hi = x.astype(_BF16)
    return hi, (x - hi.astype(_F32)).astype(_BF16)


def _split3(x):
    hi = x.astype(_BF16)
    r = x - hi.astype(_F32)
    mid = r.astype(_BF16)
    return hi, mid, (r - mid.astype(_F32)).astype(_BF16)


def _dot3(a, b, dot=_dot):
    return dot(a[0], b[0]) + dot(a[1], b[0]) + dot(a[0], b[1])


def _rms(x, w):
    return x * lax.rsqrt(jnp.mean(x * x, axis=-1, keepdims=True) + EPS) * w


def _store_token_tiles(ref, x):
    n = x.shape[0]
    for j in range(SUBLANES):
        ref[pl.ds(j, n, stride=SUBLANES), :] = x[:, j * LANES:(j + 1) * LANES]


def _load_token_tiles(ref, n, j):
    return ref[pl.ds(j, n, stride=SUBLANES), :]


def _silu(x):
    return x * jax.nn.sigmoid(x)


def _params(*sem):
    return pltpu.CompilerParams(dimension_semantics=sem, vmem_limit_bytes=VMEM_LIMIT)


def _proj_kernel(x_ref, ln_ref, w_ref, qn_ref, kn_ref,
                 u_ref, z_ref, ab_ref, mq_ref, mk_ref, mv_ref, *, heads_out):
    xn = _rms(x_ref[...], ln_ref[...]).astype(_BF16)
    h = _dot(xn, w_ref[...])
    u_ref[...] = h[:, :CONV_DIM]
    z_ref[...] = h[:, CONV_DIM:CONV_DIM + DN_WIDTH]
    o = CONV_DIM + DN_WIDTH
    ab_ref[...] = h[:, o + 3 * ATT_WIDTH:]
    for hd in range(MOBA_HEADS):
        sl = slice(o + hd * HEAD_DIM, o + (hd + 1) * HEAD_DIM)
        q = _rms(h[:, sl], qn_ref[...])
        k = _rms(h[:, sl.start + ATT_WIDTH:sl.stop + ATT_WIDTH], kn_ref[...])
        v = h[:, sl.start + 2 * ATT_WIDTH:sl.stop + 2 * ATT_WIDTH]
        if heads_out:
            mq_ref[0, hd] = q
            mk_ref[0, hd] = k
            mv_ref[0, hd] = v
        else:
            hs = slice(hd * HEAD_DIM, (hd + 1) * HEAD_DIM)
            mq_ref[:, hs] = q
            mk_ref[:, hs] = k
            mv_ref[:, hs] = v


def _proj(x2d, ln_w, w_all, qn_w, kn_w, *, tm, seq=None):
    T, D = x2d.shape
    n_all = w_all.shape[1]
    grid = (T // tm,)
    row = lambda t: (t, 0)
    const = lambda t: (0, 0)
    if seq is not None:
        B, S = seq
        per = S // tm
        hshape = jax.ShapeDtypeStruct((B, MOBA_HEADS, S, HEAD_DIM), _F32)
        hspec = pl.BlockSpec((1, MOBA_HEADS, tm, HEAD_DIM), lambda t: (t // per, 0, t % per, 0))
    else:
        hshape = jax.ShapeDtypeStruct((T, ATT_WIDTH), _F32)
        hspec = pl.BlockSpec((tm, ATT_WIDTH), row)
    return pl.pallas_call(
        functools.partial(_proj_kernel, heads_out=seq is not None),
        grid=grid,
        in_specs=[pl.BlockSpec((tm, D), row), pl.BlockSpec((1, D), const),
                  pl.BlockSpec((D, n_all), const), pl.BlockSpec((1, HEAD_DIM), const),
                  pl.BlockSpec((1, HEAD_DIM), const)],
        out_specs=[pl.BlockSpec((tm, CONV_DIM), row), pl.BlockSpec((tm, DN_WIDTH), row),
                   pl.BlockSpec((tm, LANES), row), hspec, hspec, hspec],
        out_shape=[jax.ShapeDtypeStruct((T, CONV_DIM), _F32),
                   jax.ShapeDtypeStruct((T, DN_WIDTH), _F32),
                   jax.ShapeDtypeStruct((T, LANES), _F32), hshape, hshape, hshape],
        compiler_params=_params("parallel"),
        name="proj",
    )(x2d, ln_w, w_all, qn_w, kn_w)


def _gdn_kernel(u_ref, z_ref, ab_ref, conv0_ref, ssm0_ref, wconv_ref, alog_ref, dtb_ref, dnw_ref,
                od_ref, ssm_ref, up_scr, s_scr, *, lt, valid_len):
    i = pl.program_id(1)
    c = DN_CHUNK
    nc = lt // c
    heads = range(DN_HEADS)

    @pl.when(i == 0)
    def _():
        s_scr[...] = ssm0_ref[0]
        up_scr[0:SUBLANES, :] = conv0_ref[0]

    up_scr[SUBLANES:SUBLANES + lt, :] = u_ref[0]
    w = wconv_ref[...]
    base = SUBLANES - (CONV_W - 1)
    y = up_scr[base:base + lt, :] * w[0:1, :]
    for t in range(1, CONV_W):
        y = y + up_scr[base + t:base + t + lt, :] * w[t:t + 1, :]
    up_scr[0:SUBLANES, :] = up_scr[lt:lt + SUBLANES, :]
    qkv = _silu(y)

    masked = valid_len < lt
    if masked:
        rowv = lax.broadcasted_iota(jnp.int32, (lt, 1), 0) < valid_len
        qkv = jnp.where(rowv, qkv, 0.0)

    ab = ab_ref[0]
    lane = lax.broadcasted_iota(jnp.int32, (lt, LANES), 1)
    xg = ab + dtb_ref[...]
    softplus = jnp.maximum(xg, 0.0) + jnp.log(1.0 + jnp.exp(-jnp.abs(xg)))
    gb = jnp.where(lane < DN_HEADS, -jnp.exp(alog_ref[...]) * softplus, jax.nn.sigmoid(ab))
    if masked:
        gb = jnp.where(rowv, gb, 0.0)

    rows = lax.broadcasted_iota(jnp.int32, (c, c), 0)
    cols = lax.broadcasted_iota(jnp.int32, (c, c), 1)
    causal = cols <= rows
    strict = cols < rows
    eye = jnp.where(rows == cols, 1.0, 0.0)
    tril16 = jnp.where(causal, 1.0, 0.0).astype(_BF16)
    prow = lax.broadcasted_iota(jnp.int32, (DN_HEADS * c, LANES), 0) // c
    plane = lax.broadcasted_iota(jnp.int32, (DN_HEADS * c, LANES), 1)
    pick16 = jnp.where(prow == plane, 1.0, 0.0).astype(_BF16)

    cum, cum_t = [], []
    for ci in range(nc):
        g3 = _split3(gb[ci * c:(ci + 1) * c, :])
        cm = _dot(tril16, g3[0]) + _dot(tril16, g3[1]) + _dot(tril16, g3[2])
        c3 = _split3(cm)
        cum.append(cm)
        cum_t.append(_dot_nt(pick16, c3[0]) + _dot_nt(pick16, c3[1]) + _dot_nt(pick16, c3[2]))
    probs = [(ci, hd) for ci in range(nc) for hd in heads]

    def rows_of(ci):
        return slice(ci * c, (ci + 1) * c)

    q_l, k_l, v_l = [], [], []
    for ci, hd in probs:
        q = qkv[rows_of(ci), hd * HEAD_DIM:(hd + 1) * HEAD_DIM]
        k = qkv[rows_of(ci), DN_WIDTH + hd * HEAD_DIM:DN_WIDTH + (hd + 1) * HEAD_DIM]
        q_l.append(q * lax.rsqrt(jnp.sum(q * q, axis=-1, keepdims=True) + EPS) * (HEAD_DIM ** -0.5))
        k_l.append(k * lax.rsqrt(jnp.sum(k * k, axis=-1, keepdims=True) + EPS))
        v_l.append(qkv[rows_of(ci), 2 * DN_WIDTH + hd * HEAD_DIM:2 * DN_WIDTH + (hd + 1) * HEAD_DIM])
    gcum_l = [cum[ci][:, hd:hd + 1] for ci, hd in probs]
    beta_l = [gb[rows_of(ci), DN_HEADS + hd:DN_HEADS + hd + 1] for ci, hd in probs]
    decay_l = [jnp.exp(jnp.where(causal, g - cum_t[ci][hd * c:(hd + 1) * c, :], -jnp.inf))
               for (ci, hd), g in zip(probs, gcum_l)]
    k16_l = [k.astype(_BF16) for k in k_l]
    kb_l = [k * b for k, b in zip(k_l, beta_l)]
    low_l = [jnp.where(strict, _dot_nt(kb.astype(_BF16), k16) * d, 0.0)
             for kb, k16, d in zip(kb_l, k16_l, decay_l)]
    attn16_l = [(_dot_nt(q.astype(_BF16), k16) * d).astype(_BF16)
                for q, k16, d in zip(q_l, k16_l, decay_l)]

    low2_l = [_split2(l) for l in low_l]
    x_l = [eye - l for l in low_l]
    m_l = [_dot3(l2, l2) for l2 in low2_l]
    n_fac = max(1, int(math.ceil(math.log2(c)))) - 1
    for f in range(n_fac):
        m2_l = [_split2(m) for m in m_l]
        x_l = [x + _dot3(_split2(x), m2) for x, m2 in zip(x_l, m2_l)]
        if f + 1 < n_fac:
            m_l = [_dot3(m2, m2) for m2 in m2_l]
    eg_l = [jnp.exp(g) for g in gcum_l]
    uw_l = [_dot3(_split2(x), _split2(jnp.concatenate([v * b, kb * eg], axis=1)))
            for x, v, b, kb, eg in zip(x_l, v_l, beta_l, kb_l, eg_l)]
    u_l = [uw[:, :HEAD_DIM] for uw in uw_l]
    w16_l = [uw[:, HEAD_DIM:].astype(_BF16) for uw in uw_l]
    qe16_l = [(q * eg).astype(_BF16) for q, eg in zip(q_l, eg_l)]
    glast_l = [g[c - 1:c, :] for g in gcum_l]
    kdec16_l = [(k * jnp.exp(gl - g)).astype(_BF16) for k, gl, g in zip(k_l, glast_l, gcum_l)]
    eglast_l = [jnp.exp(gl) for gl in glast_l]

    state = [s_scr[hd] for hd in heads]
    for ci in range(nc):
        p0 = ci * DN_HEADS
        s16 = [s.astype(_BF16) for s in state]
        vnew = [u_l[p0 + hd] - _dot(w16_l[p0 + hd], s16[hd]) for hd in heads]
        vnew16 = [v.astype(_BF16) for v in vnew]
        o = [_dot(qe16_l[p0 + hd], s16[hd]) + _dot(attn16_l[p0 + hd], vnew16[hd]) for hd in heads]
        state = [state[hd] * eglast_l[p0 + hd] + _dot_tn(kdec16_l[p0 + hd], vnew16[hd]) for hd in heads]
        for hd in heads:
            zz = z_ref[0, rows_of(ci), hd * HEAD_DIM:(hd + 1) * HEAD_DIM]
            od_ref[0, rows_of(ci), hd * HEAD_DIM:(hd + 1) * HEAD_DIM] = (
                _rms(o[hd], dnw_ref[...]) * _silu(zz)).astype(_BF16)
    for hd in heads:
        s_scr[hd] = state[hd]
    ssm_ref[0] = s_scr[...]


def _gdn(u, z, ab, conv0, ssm0, w_conv, alog_row, dtb_row, dn_w, *, lt, valid_len):
    B, L, _ = u.shape
    grid = (B, L // lt)
    tile = lambda b, i: (b, i, 0)
    perb3 = lambda b, i: (b, 0, 0)
    const = lambda b, i: (0, 0)
    return pl.pallas_call(
        functools.partial(_gdn_kernel, lt=lt, valid_len=valid_len),
        grid=grid,
        in_specs=[pl.BlockSpec((1, lt, CONV_DIM), tile), pl.BlockSpec((1, lt, DN_WIDTH), tile),
                  pl.BlockSpec((1, lt, LANES), tile), pl.BlockSpec((1, SUBLANES, CONV_DIM), perb3),
                  pl.BlockSpec((1, DN_HEADS, HEAD_DIM, HEAD_DIM), lambda b, i: (b, 0, 0, 0)),
                  pl.BlockSpec((CONV_W, CONV_DIM), const), pl.BlockSpec((1, LANES), const),
                  pl.BlockSpec((1, LANES), const), pl.BlockSpec((1, HEAD_DIM), const)],
        out_specs=[pl.BlockSpec((1, lt, DN_WIDTH), tile),
                   pl.BlockSpec((1, DN_HEADS, HEAD_DIM, HEAD_DIM), lambda b, i: (b, 0, 0, 0))],
        out_shape=[jax.ShapeDtypeStruct((B, L, DN_WIDTH), _BF16),
                   jax.ShapeDtypeStruct((B, DN_HEADS, HEAD_DIM, HEAD_DIM), _F32)],
        scratch_shapes=[pltpu.VMEM((lt + 2 * SUBLANES, CONV_DIM), _F32),
                        pltpu.VMEM((DN_HEADS, HEAD_DIM, HEAD_DIM), _F32)],
        compiler_params=_params("parallel", "arbitrary"),
        name="gdn",
    )(u, z, ab, conv0, ssm0, w_conv, alog_row, dtb_row, dn_w)


def _topk_rows(g, n_sel):
    r = g.shape[0]
    row = lax.broadcasted_iota(jnp.int32, g.shape, 0)
    sel = jnp.zeros(g.shape, _F32)
    for _ in range(n_sel):
        m = jnp.max(g, axis=0, keepdims=True)
        idx = jnp.min(jnp.where(g == m, row, r), axis=0, keepdims=True)
        pick = row == idx
        sel = jnp.where(pick, 1.0, sel)
        g = jnp.where(pick, -jnp.inf, g)
    return sel


def _moba_prompt_kernel(q_ref, k_ref, v_ref, c_ref, o_ref, ps_ref, kb_scr, vt_scr, kmean_scr, sel_scr,
                        *, nblk):
    qt = pl.program_id(2)
    blk = MOBA_BLOCK
    scale = HEAD_DIM ** -0.5

    ps_ref[...] = jnp.sum(c_ref[...], axis=1)

    @pl.when(qt == 0)
    def _():
        kmean_scr[...] = jnp.zeros(kmean_scr.shape, _F32)
        for j in range(nblk):
            kj = k_ref[0, 0, j * blk:(j + 1) * blk, :]
            kb_scr[j] = kj.astype(_BF16)
            kmean_scr[j:j + 1, :] = jnp.mean(kj, axis=0, keepdims=True)
            vt_scr[j] = v_ref[0, 0, j * blk:(j + 1) * blk, :].T.astype(_BF16)

    q = q_ref[0, 0]
    gate = _dot_nt(kmean_scr[...], q, _HI)
    brow = lax.broadcasted_iota(jnp.int32, gate.shape, 0)
    valid = brow < qt
    sel = _topk_rows(jnp.where(valid, gate, -jnp.inf), min(MOBA_TOPK, nblk))
    sel_scr[...] = jnp.where(valid, sel, 0.0)

    q16 = (q * scale).astype(_BF16)
    kpos = lax.broadcasted_iota(jnp.int32, (blk, blk), 0)
    qpos = lax.broadcasted_iota(jnp.int32, (blk, blk), 1)
    def scores(j):
        return jnp.where(sel_scr[pl.ds(j, 1), :] > 0.0, _dot_nt(kb_scr[j], q16), NEG_BIG)

    s = jnp.where(kpos <= qpos, _dot_nt(kb_scr[qt], q16), NEG_BIG)
    m = jnp.max(s, axis=0, keepdims=True)
    p = jnp.exp(s - m)
    l = jnp.sum(p, axis=0, keepdims=True)

    def body(j, carry):
        s_cur, p_prev, j_prev, m, l, acc = carry
        s_next = scores(j + 1)
        pv = _dot(vt_scr[j_prev], p_prev)
        m_new = jnp.maximum(m, jnp.max(s_cur, axis=0, keepdims=True))
        alpha = jnp.exp(m - m_new)
        p = jnp.exp(s_cur - m_new)
        l = alpha * l + jnp.sum(p, axis=0, keepdims=True)
        return s_next, p.astype(_BF16), j, m_new, l, alpha * (acc + pv)

    init = (scores(0), p.astype(_BF16), qt, m, l, jnp.zeros((HEAD_DIM, blk), _F32))
    _, p_prev, j_prev, m, l, acc = lax.fori_loop(0, qt, body, init)
    acc = acc + _dot(vt_scr[j_prev], p_prev)
    o_ref[0] = (acc / l).T.astype(_BF16)


def _moba_prompt(q, k, v, slabs):
    B, H, S, dh = q.shape
    nblk = S // MOBA_BLOCK
    nb8 = -(-nblk // SUBLANES) * SUBLANES
    assert S % MOBA_BLOCK == 0
    n_slabs, ps, _ = slabs.shape
    steps = B * H * nblk
    share = n_slabs // steps
    assert share * steps == n_slabs and share % SUBLANES == 0, "key cache does not split evenly over the grid"
    full = lambda b, h, t: (b, h, 0, 0)
    flat = lambda b, h, t: (b * H + h) * nblk + t
    return pl.pallas_call(
        functools.partial(_moba_prompt_kernel, nblk=nblk),
        grid=(B, H, nblk),
        in_specs=[pl.BlockSpec((1, 1, MOBA_BLOCK, dh), lambda b, h, t: (b, h, t, 0)),
                  pl.BlockSpec((1, 1, S, dh), full), pl.BlockSpec((1, 1, S, dh), full),
                  pl.BlockSpec((share, ps, dh), lambda b, h, t: (flat(b, h, t), 0, 0))],
        out_specs=[pl.BlockSpec((1, MOBA_BLOCK, dh), lambda b, h, t: (b, t, h)),
                   pl.BlockSpec((share, dh), lambda b, h, t: (flat(b, h, t), 0))],
        out_shape=[jax.ShapeDtypeStruct((B, S, H * dh), _BF16),
                   jax.ShapeDtypeStruct((n_slabs, dh), _F32)],
        scratch_shapes=[pltpu.VMEM((nblk, MOBA_BLOCK, dh), _BF16),
                        pltpu.VMEM((nblk, dh, MOBA_BLOCK), _BF16),
                        pltpu.VMEM((nb8, dh), _F32),
                        pltpu.VMEM((nb8, MOBA_BLOCK), _F32)],
        compiler_params=_params("parallel", "parallel", "arbitrary"),
        name="moba_prompt",
    )(q, k, v, slabs)


def _moba_sel_kernel(pt_ref, psum_ref, q_ref, sel_ref, km_scr, *, n_pages, ppb, n_blk, n_sel, rows_per_blk):
    b = pl.program_id(0)
    km_scr[...] = jnp.zeros(km_scr.shape, _F32)

    def fill(j, carry):
        acc = psum_ref[pl.ds(pt_ref[b * n_pages + j * ppb], 1), :]
        for r in range(1, ppb):
            acc = acc + psum_ref[pl.ds(pt_ref[b * n_pages + j * ppb + r], 1), :]
        km_scr[pl.ds(j, 1), :] = acc * (1.0 / rows_per_blk)
        return carry

    lax.fori_loop(0, n_blk, fill, 0)
    lane = lax.broadcasted_iota(jnp.int32, (SUBLANES, LANES), 1)
    for hd in range(MOBA_HEADS):
        g = _dot_nt(q_ref[0, hd], km_scr[:, hd * HEAD_DIM:(hd + 1) * HEAD_DIM], _HI)
        g = jnp.where(lane < n_blk, g, -jnp.inf)
        out = jnp.zeros((SUBLANES, LANES), jnp.int32)
        for r in range(n_sel):
            m = jnp.max(g, axis=1, keepdims=True)
            idx = jnp.min(jnp.where(g == m, lane, LANES), axis=1, keepdims=True)
            out = jnp.where(lane == r, idx, out)
            g = jnp.where(lane == idx, -jnp.inf, g)
        sel_ref[0, hd] = out


def _moba_sel(page_table, psum, q8, *, ppb, n_sel, rows_per_blk):
    DB, n_pages = page_table.shape
    n_blk = n_pages // ppb
    assert n_blk <= LANES
    n_phys, width = psum.shape
    return pl.pallas_call(
        functools.partial(_moba_sel_kernel, n_pages=n_pages, ppb=ppb, n_blk=n_blk, n_sel=n_sel,
                          rows_per_blk=rows_per_blk),
        grid_spec=pltpu.PrefetchScalarGridSpec(
            num_scalar_prefetch=1,
            grid=(DB,),
            in_specs=[pl.BlockSpec((n_phys, width), lambda b, pt: (0, 0)),
                      pl.BlockSpec((1, MOBA_HEADS, SUBLANES, HEAD_DIM), lambda b, pt: (b, 0, 0, 0))],
            out_specs=pl.BlockSpec((1, MOBA_HEADS, SUBLANES, LANES), lambda b, pt: (b, 0, 0, 0)),
            scratch_shapes=[pltpu.VMEM((LANES, width), _F32)]),
        out_shape=jax.ShapeDtypeStruct((DB, MOBA_HEADS, SUBLANES, LANES), jnp.int32),
        compiler_params=_params("arbitrary"),
        name="moba_sel",
    )(page_table.reshape(-1), psum, q8)


def _moba_sample_kernel(phys_ref, q_ref, kn_ref, vn_ref, ck_hbm, cv_hbm, o_ref, kbuf, vbuf, sem,
                        *, n_pg, n_new, n_heads, n_steps):
    step = pl.program_id(0)
    per = n_new * n_pg
    scale = HEAD_DIM ** -0.5

    def page_copies(st, slot):
        hd = st % n_heads
        cps = []
        for j in range(per):
            page = phys_ref[st * per + j]
            cps.append(pltpu.make_async_copy(ck_hbm.at[page, hd], kbuf.at[slot, j], sem.at[slot]))
            cps.append(pltpu.make_async_copy(cv_hbm.at[page, hd], vbuf.at[slot, j], sem.at[slot]))
        return cps

    def attend(slot):
        q = q_ref[0, 0]
        q16 = q.astype(_BF16)
        kn = kn_ref[0, 0]
        vn = vn_ref[0, 0]
        rowq = lax.broadcasted_iota(jnp.int32, (SUBLANES, 1), 0)
        s_own = []
        for c in range(n_new):
            sc = jnp.sum(q * kn[c:c + 1, :], axis=-1, keepdims=True) * scale
            s_own.append(jnp.where(rowq >= c, sc, NEG_BIG))
        m_own = s_own[0]
        for sc in s_own[1:]:
            m_own = jnp.maximum(m_own, sc)
        out = jnp.zeros((SUBLANES, HEAD_DIM), _F32)
        for t in range(n_new):
            pages = range(t * n_pg, (t + 1) * n_pg)
            s_past = [_dot_nt(q16, kbuf[slot, j].astype(_BF16)) * scale for j in pages]
            m = m_own
            for sp in s_past:
                m = jnp.maximum(m, jnp.max(sp, axis=-1, keepdims=True))
            l = jnp.zeros((SUBLANES, 1), _F32)
            acc = jnp.zeros((SUBLANES, HEAD_DIM), _F32)
            for sp, j in zip(s_past, pages):
                p = jnp.exp(sp - m)
                l = l + jnp.sum(p, axis=-1, keepdims=True)
                acc = acc + _dot(p.astype(_BF16), vbuf[slot, j].astype(_BF16))
            for c, sc in enumerate(s_own):
                p = jnp.exp(sc - m)
                l = l + p
                acc = acc + p * vn[c:c + 1, :]
            out = jnp.where(rowq == t, acc / l, out)
        o_ref[0, 0] = out

    def run(slot):
        if slot == 0:
            @pl.when(step == 0)
            def _():
                for cp in page_copies(0, 0):
                    cp.start()

        @pl.when(step + 1 < n_steps)
        def _():
            for cp in page_copies(step + 1, 1 - slot):
                cp.start()

        for cp in page_copies(step, slot):
            cp.wait()
        attend(slot)

    for slot in range(2):
        @pl.when(step % 2 == slot)
        def _():
            run(slot)


def _moba_sample(phys, q8, kn8, vn8, cache_k, cache_v, *, n_new):
    DB, H, _, dh = q8.shape
    ps = cache_k.shape[2]
    n_pg = phys.shape[0] // (DB * H * n_new)
    steps = DB * H
    small = pl.BlockSpec((1, 1, SUBLANES, dh), lambda s, ph: (s // H, s % H, 0, 0))
    return pl.pallas_call(
        functools.partial(_moba_sample_kernel, n_pg=n_pg, n_new=n_new, n_heads=H, n_steps=steps),
        grid_spec=pltpu.PrefetchScalarGridSpec(
            num_scalar_prefetch=1,
            grid=(steps,),
            in_specs=[small, small, small, pl.BlockSpec(memory_space=pl.ANY),
                      pl.BlockSpec(memory_space=pl.ANY)],
            out_specs=small,
            scratch_shapes=[pltpu.VMEM((2, n_new * n_pg, ps, dh), _F32),
                            pltpu.VMEM((2, n_new * n_pg, ps, dh), _F32),
                            pltpu.SemaphoreType.DMA((2,))]),
        out_shape=jax.ShapeDtypeStruct((DB, H, SUBLANES, dh), _F32),
        compiler_params=_params("arbitrary"),
        name="moba_sample",
    )(phys, q8, kn8, vn8, cache_k, cache_v)


def _post_kernel(xp_ref, odp_ref, omp_ref, xs_ref, ods_ref, oms_ref, wo1_ref, wo2_ref, ln_ref,
                 wrh_ref, wrl_ref, br_ref, x1_ref, xn_ref, route_ref, cnt_ref, carry_scr,
                 *, tm, n_prompt_tiles, n_tokens):
    i = pl.program_id(0)

    @pl.when(i == 0)
    def _():
        carry_scr[...] = jnp.zeros(carry_scr.shape, _F32)

    is_p = i < n_prompt_tiles
    x = jnp.where(is_p, xp_ref[...], xs_ref[...])
    od = jnp.where(is_p, odp_ref[...], ods_ref[...])
    om = jnp.where(is_p, omp_ref[...], oms_ref[...])
    x1 = x + _dot(od, wo1_ref[...]) + _dot(om, wo2_ref[...])
    _store_token_tiles(x1_ref, x1)
    xn = _rms(x1, ln_ref[...])
    _store_token_tiles(xn_ref, xn)
    xh, xl = _split2(xn)
    logits = (_dot_nt(wrh_ref[...], xh) + _dot_nt(wrh_ref[...], xl) + _dot_nt(wrl_ref[...], xh)) + br_ref[...]

    erow = lax.broadcasted_iota(jnp.int32, (N_EXPERTS, tm), 0)
    g = logits
    vals, picks = [], []
    for _ in range(TOP_K):
        m = jnp.max(g, axis=0, keepdims=True)
        idx = jnp.min(jnp.where(g == m, erow, N_EXPERTS), axis=0, keepdims=True)
        pick = erow == idx
        vals.append(m)
        picks.append((idx, pick))
        g = jnp.where(pick, -jnp.inf, g)
    es = [jnp.exp(v - vals[0]) for v in vals]
    den = es[0]
    for e in es[1:]:
        den = den + e

    rows = lax.broadcasted_iota(jnp.int32, (tm, tm), 0)
    cols = lax.broadcasted_iota(jnp.int32, (tm, tm), 1)
    earlier = jnp.where(rows < cols, 1.0, 0.0).astype(_BF16)
    real = jnp.where((i * tm + lax.broadcasted_iota(jnp.int32, (1, tm), 1)) < n_tokens, 1.0, 0.0)
    base = carry_scr[...]
    tok = lax.broadcasted_iota(jnp.int32, (1, tm), 1)
    ranks = [jnp.zeros((1, tm), _F32) for _ in picks]
    for sub in range(tm // TOK_TILE):
        in_sub = jnp.where((tok >= sub * TOK_TILE) & (tok < (sub + 1) * TOK_TILE), real, 0.0)
        for k, (idx, pick) in enumerate(picks):
            onehot = jnp.where(pick, in_sub, 0.0)
            pref = _dot(onehot.astype(_BF16), earlier) + base
            ranks[k] = ranks[k] + jnp.sum(onehot * pref, axis=0, keepdims=True)
            base = base + jnp.sum(onehot, axis=1, keepdims=True)
    carry_scr[...] = base
    cnt_ref[...] = base
    record = ([idx.astype(_F32) for idx, _ in picks] + [e / den for e in es] + ranks
              + [jnp.zeros((route_ref.shape[0] - 3 * TOP_K, tm), _F32)])
    route_ref[...] = jnp.concatenate(record, axis=0)


def _post(xp, odp, omp, xs, ods, oms, wo1, wo2, ln_w, wr_hi, wr_lo, b_r, *, tm, n_tokens):
    Tp, D = xp.shape
    npt = Tp // tm
    T = Tp + tm
    prow = lambda t: (jnp.minimum(t, npt - 1), 0)
    srow = lambda t: (0, 0)
    row = lambda t: (t, 0)
    const = lambda t: (0, 0)
    return pl.pallas_call(
        functools.partial(_post_kernel, tm=tm, n_prompt_tiles=npt, n_tokens=n_tokens),
        grid=(npt + 1,),
        in_specs=[pl.BlockSpec((tm, D), prow), pl.BlockSpec((tm, DN_WIDTH), prow),
                  pl.BlockSpec((tm, ATT_WIDTH), prow),
                  pl.BlockSpec((tm, D), srow), pl.BlockSpec((tm, DN_WIDTH), srow),
                  pl.BlockSpec((tm, ATT_WIDTH), srow),
                  pl.BlockSpec((DN_WIDTH, D), const), pl.BlockSpec((ATT_WIDTH, D), const),
                  pl.BlockSpec((1, D), const), pl.BlockSpec((N_EXPERTS, D), const),
                  pl.BlockSpec((N_EXPERTS, D), const), pl.BlockSpec((N_EXPERTS, 1), const)],
        out_specs=[pl.BlockSpec((tm * SUBLANES, LANES), row), pl.BlockSpec((tm * SUBLANES, LANES), row),
                   pl.BlockSpec((ROUTE_ROWS, tm), lambda t: (0, t)), pl.BlockSpec((N_EXPERTS, 1), const)],
        out_shape=[jax.ShapeDtypeStruct((T * SUBLANES, LANES), _F32),
                   jax.ShapeDtypeStruct((T * SUBLANES, LANES), _F32),
                   jax.ShapeDtypeStruct((ROUTE_ROWS, T), _F32), jax.ShapeDtypeStruct((N_EXPERTS, 1), _F32)],
        scratch_shapes=[pltpu.VMEM((N_EXPERTS, 1), _F32)],
        compiler_params=_params("arbitrary"),
        name="post",
    )(xp, odp, omp, xs, ods, oms, wo1, wo2, ln_w, wr_hi, wr_lo, b_r)


RUN_PIECES = tuple(1 << b for b in reversed(range(TOK_TILE.bit_length())))
RUN_RARE = 64


def _all_run_copies(make_copy, start_ref, len_ref, tile):
    runs = []
    pool_start = jnp.int32(0)
    for e in range(N_EXPERTS):
        n = len_ref[tile * N_EXPERTS + e]
        first = start_ref[tile * N_EXPERTS + e]
        pieces = []
        for piece in RUN_PIECES:
            done = n & (-2 * piece)
            pieces.append((piece, make_copy(first + done, pool_start + done, piece)))
        runs.append((n, pieces))
        pool_start = pool_start + n
    return runs


def _for_pieces(runs, act):
    for e, (n, pieces) in enumerate(runs):
        def some(which):
            for piece, cp in pieces:
                if which(piece):
                    @pl.when((n & piece) != 0)
                    def _():
                        act(cp, e)

        @pl.when(n >= RUN_RARE)
        def _():
            some(lambda piece: piece >= RUN_RARE)

        some(lambda piece: piece < RUN_RARE)


def _start_all(runs):
    _for_pieces(runs, lambda cp, e: cp.start(priority=e % 2))


def _wait_all(runs):
    _for_pieces(runs, lambda cp, e: cp.wait())


def _rows(ref, first_row, n_rows):
    return ref.at[pl.ds(pl.multiple_of(first_row * SUBLANES, SUBLANES), n_rows * SUBLANES)]


def _dispatch_kernel(start_ref, len_ref, tail_ref, nu_ref, slot_hbm, xn_ref, xs_hbm, idx_smem0, idx_smem1, pool, zero_scr,
                     sem_idx, sem_zero, sem_rows, *, tm, n_blocks, n_tiles):
    i = pl.program_id(0)
    n_asg = tm * TOP_K
    idx_smem = (idx_smem0, idx_smem1)

    def idx_copy(tile, slot):
        return pltpu.make_async_copy(slot_hbm.at[pl.ds(tile * n_asg, n_asg)], idx_smem[slot], sem_idx.at[slot])

    @pl.when(i == 0)
    def _():
        zero_scr[...] = jnp.zeros(zero_scr.shape, _F32)

        def clear(first_row):
            first = pl.multiple_of(first_row * SUBLANES, MOE_ROWS * SUBLANES)
            cp = pltpu.make_async_copy(zero_scr, xs_hbm.at[pl.ds(first, MOE_ROWS * SUBLANES)], sem_zero)
            cp.start()
            cp.wait()

        for e in range(N_EXPERTS):
            @pl.when(tail_ref[e] >= 0)
            def _():
                clear(tail_ref[e])

        def clear_block(b, carry):
            clear(b * MOE_ROWS)
            return carry

        lax.fori_loop(nu_ref[0], n_blocks, clear_block, 0)

    def run_copies(tile, slot):
        return _all_run_copies(
            lambda hbm_row, pool_row, n: pltpu.make_async_copy(
                _rows(pool.at[slot], pool_row, n), _rows(xs_hbm, hbm_row, n), sem_rows.at[slot]),
            start_ref, len_ref, tile)

    def step(slot):
        if slot == 0:
            @pl.when(i == 0)
            def _():
                idx_copy(0, 0).start()

        @pl.when(i + 1 < n_tiles)
        def _():
            idx_copy(i + 1, 1 - slot).start()

        idx_copy(i, slot).wait()

        @pl.when(i >= 2)
        def _():
            _wait_all(run_copies(i - 2, slot))

        def pack(t, carry):
            tile = xn_ref[pl.ds(pl.multiple_of(t * SUBLANES, SUBLANES), SUBLANES), :]
            for k in range(TOP_K):
                row = pl.multiple_of(idx_smem[slot][k * tm + t] * SUBLANES, SUBLANES)
                pool[slot, pl.ds(row, SUBLANES), :] = tile
            return carry

        lax.fori_loop(0, tm, pack, 0, unroll=4)
        _start_all(run_copies(i, slot))

        @pl.when(i == n_tiles - 1)
        def _():
            _wait_all(run_copies(i, slot))

            @pl.when(i >= 1)
            def _():
                _wait_all(run_copies(i - 1, 1 - slot))

    for slot in range(2):
        @pl.when(i % 2 == slot)
        def _():
            step(slot)


def _dispatch(run_start, run_len, tail, n_used, slot, xn_tiles, n_tiles, n_blocks, *, tm):
    pre = lambda i, *_: (i, 0)
    return pl.pallas_call(
        functools.partial(_dispatch_kernel, tm=tm, n_blocks=n_blocks, n_tiles=n_tiles),
        grid_spec=pltpu.PrefetchScalarGridSpec(
            num_scalar_prefetch=4,
            grid=(n_tiles,),
            in_specs=[pl.BlockSpec(memory_space=pl.ANY), pl.BlockSpec((tm * SUBLANES, LANES), pre)],
            out_specs=pl.BlockSpec(memory_space=pl.ANY),
            scratch_shapes=[pltpu.SMEM((tm * TOP_K,), jnp.int32), pltpu.SMEM((tm * TOP_K,), jnp.int32),
                            pltpu.VMEM((2, (tm * TOP_K + 1) * SUBLANES, LANES), _F32),
                            pltpu.VMEM((MOE_ROWS * SUBLANES, LANES), _F32),
                            pltpu.SemaphoreType.DMA((2,)), pltpu.SemaphoreType.DMA, pltpu.SemaphoreType.DMA((2,))]),
        out_shape=jax.ShapeDtypeStruct((n_blocks * MOE_ROWS * SUBLANES, LANES), _F32),
        compiler_params=_params("arbitrary"),
        name="dispatch",
    )(run_start, run_len, tail, n_used, slot, xn_tiles)


def _moe_kernel(be_ref, nu_ref, x_ref, wg_ref, bg_ref, wu_ref, bu_ref, wd_ref, bd_ref, o_ref,
                wg_scr, wu_scr, wd_scr, x_scr):
    i = pl.program_id(0)

    @pl.when(i >= nu_ref[0])
    def _():
        o_ref[...] = jnp.zeros(o_ref.shape, _F32)

    @pl.when(i < nu_ref[0])
    def _():
        prev = be_ref[jnp.maximum(i - 1, 0)]

        @pl.when((i == 0) | (be_ref[i] != prev))
        def _():
            wg_scr[...] = wg_ref[0].astype(_BF16)
            wu_scr[...] = wu_ref[0].astype(_BF16)
            wd_scr[...] = wd_ref[0].astype(_BF16)

        for j in range(SUBLANES):
            x_scr[:, j * LANES:(j + 1) * LANES] = _load_token_tiles(x_ref, MOE_ROWS, j).astype(_BF16)
        x = x_scr[...]
        gt = jnp.minimum(_dot(x, wg_scr[...]) + bg_ref[0], SWIGLU_LIMIT)
        up = jnp.clip(_dot(x, wu_scr[...]) + bu_ref[0], -SWIGLU_LIMIT, SWIGLU_LIMIT)
        act = ((up + 1.0) * (gt * jax.nn.sigmoid(SWIGLU_ALPHA * gt))).astype(_BF16)
        _store_token_tiles(o_ref, _dot(act, wd_scr[...]) + bd_ref[0])


def _moe_ffn(block_e, n_used, xs, n_blocks, w_gate, b_gate, w_up, b_up, w_down, b_down):
    E, D, F = w_gate.shape
    assert D == SUBLANES * LANES
    rows = MOE_ROWS * SUBLANES
    xblk = lambda i, be, nu: (jnp.minimum(i, nu[0] - 1), 0)
    wsel = lambda i, be, nu: (be[jnp.minimum(i, nu[0] - 1)], 0, 0)
    return pl.pallas_call(
        _moe_kernel,
        grid_spec=pltpu.PrefetchScalarGridSpec(
            num_scalar_prefetch=2,
            grid=(n_blocks,),
            in_specs=[pl.BlockSpec((rows, LANES), xblk),
                      pl.BlockSpec((1, D, F), wsel), pl.BlockSpec((1, 1, F), wsel),
                      pl.BlockSpec((1, D, F), wsel), pl.BlockSpec((1, 1, F), wsel),
                      pl.BlockSpec((1, F, D), wsel), pl.BlockSpec((1, 1, D), wsel)],
            out_specs=pl.BlockSpec((rows, LANES), lambda i, be, nu: (i, 0)),
            scratch_shapes=[pltpu.VMEM((D, F), _BF16), pltpu.VMEM((D, F), _BF16),
                            pltpu.VMEM((F, D), _BF16), pltpu.VMEM((MOE_ROWS, D), _BF16)]),
        out_shape=jax.ShapeDtypeStruct((n_blocks * rows, LANES), _F32),
        compiler_params=_params("arbitrary"),
        name="moe_ffn",
    )(block_e, n_used, xs, w_gate, b_gate.reshape(E, 1, F), w_up, b_up.reshape(E, 1, F),
      w_down, b_down.reshape(E, 1, D))


def _combine_kernel(start_ref, len_ref, slot_hbm, gate_hbm, ys_hbm, x1_ref, yp_ref, yr_ref,
                    idx_smem0, idx_smem1, gate_smem0, gate_smem1, pool, out_scr, sem_idx, sem_rows,
                    *, tm, n_prompt_tiles, n_tiles):
    i = pl.program_id(0)
    n_asg = tm * TOP_K
    idx_smem = (idx_smem0, idx_smem1)
    gate_smem = (gate_smem0, gate_smem1)

    def run_copies(tile, slot):
        return _all_run_copies(
            lambda hbm_row, pool_row, n: pltpu.make_async_copy(
                _rows(ys_hbm, hbm_row, n), _rows(pool.at[slot], pool_row, n), sem_rows.at[slot]),
            start_ref, len_ref, tile)

    def table_copies(tile, slot):
        return [pltpu.make_async_copy(slot_hbm.at[pl.ds(tile * n_asg, n_asg)], idx_smem[slot], sem_idx.at[slot]),
                pltpu.make_async_copy(gate_hbm.at[pl.ds(tile * n_asg, n_asg)], gate_smem[slot], sem_idx.at[slot])]

    def fetch(tile, slot):
        for cp in table_copies(tile, slot):
            cp.start()
        _start_all(run_copies(tile, slot))

    def step(slot):
        if slot == 0:
            @pl.when(i == 0)
            def _():
                fetch(0, 0)

        @pl.when(i + 1 < n_tiles)
        def _():
            fetch(i + 1, 1 - slot)

        for cp in table_copies(i, slot):
            cp.wait()
        _wait_all(run_copies(i, slot))

        def token(t, carry):
            at = pl.ds(pl.multiple_of(t * SUBLANES, SUBLANES), SUBLANES)
            acc = x1_ref[at, :]
            for k in range(TOP_K):
                row = pl.multiple_of(idx_smem[slot][k * tm + t] * SUBLANES, SUBLANES)
                acc = acc + gate_smem[slot][k * tm + t] * pool[slot, pl.ds(row, SUBLANES), :]
            out_scr[at, :] = acc
            return carry

        lax.fori_loop(0, tm, token, 0, unroll=4)

        def emit(o_ref):
            for j in range(SUBLANES):
                o_ref[:, j * LANES:(j + 1) * LANES] = _load_token_tiles(out_scr, tm, j)

        @pl.when(i < n_prompt_tiles)
        def _():
            emit(yp_ref)

        @pl.when(i >= n_prompt_tiles)
        def _():
            emit(yr_ref)

    for slot in range(2):
        @pl.when(i % 2 == slot)
        def _():
            step(slot)


def _combine(run_start, run_len, slot, gates, ys, x1_tiles, n_prompt, n_tiles, *, tm):
    D = SUBLANES * LANES
    npt = n_prompt // tm
    pre = lambda i, *_: (i, 0)
    return pl.pallas_call(
        functools.partial(_combine_kernel, tm=tm, n_prompt_tiles=npt, n_tiles=n_tiles),
        grid_spec=pltpu.PrefetchScalarGridSpec(
            num_scalar_prefetch=2,
            grid=(n_tiles,),
            in_specs=[pl.BlockSpec(memory_space=pl.ANY), pl.BlockSpec(memory_space=pl.ANY),
                      pl.BlockSpec(memory_space=pl.ANY), pl.BlockSpec((tm * SUBLANES, LANES), pre)],
            out_specs=[pl.BlockSpec((tm, D), lambda i, *_: (jnp.minimum(i, npt - 1), 0)),
                       pl.BlockSpec((tm, D), lambda i, *_: (jnp.maximum(i - npt, 0), 0))],
            scratch_shapes=[pltpu.SMEM((tm * TOP_K,), jnp.int32), pltpu.SMEM((tm * TOP_K,), jnp.int32),
                            pltpu.SMEM((tm * TOP_K,), _F32), pltpu.SMEM((tm * TOP_K,), _F32),
                            pltpu.VMEM((2, tm * TOP_K * SUBLANES, LANES), _F32),
                            pltpu.VMEM((tm * SUBLANES, LANES), _F32),
                            pltpu.SemaphoreType.DMA((2,)), pltpu.SemaphoreType.DMA((2,))]),
        out_shape=[jax.ShapeDtypeStruct((n_prompt, D), _F32),
                   jax.ShapeDtypeStruct(((n_tiles - npt) * tm, D), _F32)],
        compiler_params=_params("arbitrary"),
        name="combine",
    )(run_start, run_len, slot, gates, ys, x1_tiles)


def _pad_lanes(v, fill=0.0):
    v = v.reshape(1, -1).astype(_F32)
    return jnp.pad(v, ((0, 0), (0, LANES - v.shape[1])), constant_values=fill)


def _pad_rows(a, rows):
    return jnp.pad(a, ((0, rows - a.shape[0]), (0, 0)))


def _prep_w_in(w_in):
    off_a = CONV_DIM + DN_WIDTH
    off_mq = off_a + 2 * DN_HEADS
    ab = jnp.pad(w_in[:, off_a:off_mq], ((0, 0), (0, LANES - 2 * DN_HEADS)))
    return jnp.concatenate([w_in[:, :off_a], w_in[:, off_mq:], ab], axis=1).astype(_BF16)


def _moe(xn_tiles, x1_tiles, route_t, counts, n_prompt, n_tokens, w_gate, b_gate, w_up, b_up, w_down, b_down):
    tm = TOK_TILE
    n_tiles = -(-n_tokens // tm)
    tc = n_tiles * tm
    idx = route_t[0:TOP_K, :tc].astype(jnp.int32)
    gates = route_t[TOP_K:2 * TOP_K, :tc]
    rank = route_t[2 * TOP_K:3 * TOP_K, :tc].astype(jnp.int32)
    cnt = counts[:, 0].astype(jnp.int32)
    padded = (cnt + MOE_ROWS - 1) // MOE_ROWS * MOE_ROWS
    pends = jnp.cumsum(padded)
    pstarts = pends - padded
    n_blocks = -(-n_tokens * TOP_K // MOE_ROWS) + N_EXPERTS
    real = (jnp.arange(tc, dtype=jnp.int32) < n_tokens)[None, :]
    experts = jnp.arange(N_EXPERTS, dtype=jnp.int32)[:, None, None]
    onehot = (idx[None] == experts) & real[None]
    hist = jnp.sum(onehot.reshape(N_EXPERTS, TOP_K, n_tiles, tm), axis=(1, 3), dtype=jnp.int32).T
    before_tile = jnp.cumsum(hist, axis=0) - hist
    before_expert = jnp.cumsum(hist, axis=1) - hist
    run_start = (pstarts[None, :] + before_tile).reshape(-1)
    shift = jnp.repeat((before_tile - before_expert).T, tm, axis=1)
    slot = rank - jnp.sum(jnp.where(onehot, shift[:, None, :], 0), axis=0)
    per_tile = lambda d: d.reshape(TOP_K, n_tiles, tm).transpose(1, 0, 2).reshape(-1)
    slot_pack = per_tile(jnp.where(real, slot, tm * TOP_K))
    slot_read = per_tile(jnp.where(real, slot, 0))
    block_start = jnp.arange(n_blocks, dtype=jnp.int32) * MOE_ROWS
    block_e = jnp.minimum(jnp.sum((pends[None, :] <= block_start[:, None]).astype(jnp.int32), axis=1),
                          N_EXPERTS - 1)
    n_used = (pends[-1:] // MOE_ROWS).astype(jnp.int32)
    tail = jnp.where(cnt > 0, pends - MOE_ROWS, -1).astype(jnp.int32)
    run_len = hist.reshape(-1)
    xs = _dispatch(run_start, run_len, tail, n_used, slot_pack, xn_tiles, n_tiles, n_blocks, tm=tm)
    ys = _moe_ffn(block_e, n_used, xs, n_blocks, w_gate, b_gate, w_up, b_up, w_down, b_down)
    return _combine(run_start, run_len, slot_read, per_tile(gates), ys, x1_tiles, n_prompt, n_tiles, tm=tm)


def _layer(xp, xs_, cache_k, cache_v, conv_s, ssm_s, page_table, lw):
    (ln1_w, w_in, w_conv, a_log, dt_bias, dn_norm_w, q_norm_w, k_norm_w, w_o, ln2_w, w_router,
     b_router, w_gate, b_gate, w_up, b_up, w_down, b_down) = lw
    B, S, D = xp.shape
    DB, L, _ = xs_.shape
    n_phys, H, PS, dh = cache_k.shape
    n_pages = page_table.shape[1]
    ppb = MOBA_BLOCK // PS
    Tp, Ts = B * S, DB * L
    assert (n_pages * PS) % MOBA_BLOCK == 0, "paged past must end on a MoBA block boundary"
    assert CONV_W - 1 <= L <= SUBLANES and Ts <= POST_ROWS and POST_ROWS % TOK_TILE == 0
    assert Tp % PROJ_ROWS == 0 and S % GDN_ROWS == 0 and Tp % POST_ROWS == 0
    cur = n_pages // ppb
    n_sel = min(MOBA_TOPK, cur)
    assert n_sel > 0

    w_all = _prep_w_in(w_in)
    ln1 = ln1_w.reshape(1, D)
    qn = q_norm_w.reshape(1, dh)
    kn = k_norm_w.reshape(1, dh)
    alog_row = _pad_lanes(a_log)
    dtb_row = _pad_lanes(dt_bias)
    dnw = dn_norm_w.reshape(1, dh)

    u_p, z_p, ab_p, mq_p, mk_p, mv_p = _proj(xp.reshape(Tp, D), ln1, w_all, qn, kn, tm=PROJ_ROWS, seq=(B, S))
    od_p, ssm_p = _gdn(u_p.reshape(B, S, CONV_DIM), z_p.reshape(B, S, DN_WIDTH), ab_p.reshape(B, S, LANES),
                       jnp.zeros((B, SUBLANES, CONV_DIM), _F32), jnp.zeros((B, DN_HEADS, dh, dh), _F32),
                       w_conv, alog_row, dtb_row, dnw, lt=GDN_ROWS, valid_len=GDN_ROWS)
    om_p, psum = _moba_prompt(mq_p, mk_p, mv_p, cache_k.reshape(n_phys * H, PS, dh))
    conv_p = u_p.reshape(B, S, CONV_DIM)[:, S - (CONV_W - 1):]

    u_s, z_s, ab_s, mq_s, mk_s, mv_s = _proj(xs_.reshape(Ts, D), ln1, w_all, qn, kn, tm=Ts)
    padl = lambda a: jnp.pad(a.reshape(DB, L, -1), ((0, 0), (0, DN_CHUNK - L), (0, 0)))
    conv0 = jnp.pad(conv_s, ((0, 0), (SUBLANES - (CONV_W - 1), 0), (0, 0)))
    od_s, ssm_s_new = _gdn(padl(u_s), padl(z_s), padl(ab_s), conv0, ssm_s, w_conv, alog_row, dtb_row, dnw,
                           lt=DN_CHUNK, valid_len=L)
    od_s = od_s[:, :L].reshape(Ts, DN_WIDTH)
    conv_s_new = jnp.concatenate([conv_s, u_s.reshape(DB, L, CONV_DIM)], axis=1)[:, L:]

    heads = lambda a: a.reshape(DB, L, H, dh).transpose(0, 2, 1, 3)
    pad8 = lambda a: jnp.pad(a, ((0, 0), (0, 0), (0, SUBLANES - L), (0, 0)))
    q8, k8, v8 = pad8(heads(mq_s)), pad8(heads(mk_s)), pad8(heads(mv_s))
    psum = psum.reshape(n_phys, H * dh)
    sel = _moba_sel(page_table, psum, q8, ppb=ppb, n_sel=n_sel, rows_per_blk=MOBA_BLOCK)
    sel = sel[:, :, :L, :n_sel]
    logical = sel[..., None] * ppb + jnp.arange(ppb, dtype=jnp.int32)
    phys = page_table[jnp.arange(DB)[:, None, None, None, None], logical]
    om_s = _moba_sample(phys.reshape(-1).astype(jnp.int32), q8, k8, v8, cache_k, cache_v, n_new=L)
    om_s = om_s[:, :, :L].transpose(0, 2, 1, 3).reshape(Ts, H * dh).astype(_BF16)

    wo = w_o.astype(_BF16)
    wr = w_router.T
    wr_hi = wr.astype(_BF16)
    wr_lo = (wr - wr_hi.astype(_F32)).astype(_BF16)
    x1_tiles, xn_tiles, route_t, cnt = _post(
        xp.reshape(Tp, D), od_p.reshape(Tp, DN_WIDTH), om_p.reshape(Tp, ATT_WIDTH),
        _pad_rows(xs_.reshape(Ts, D), POST_ROWS), _pad_rows(od_s, POST_ROWS), _pad_rows(om_s, POST_ROWS),
        wo[:DN_WIDTH], wo[DN_WIDTH:], ln2_w.reshape(1, D), wr_hi, wr_lo, b_router.reshape(N_EXPERTS, 1),
        tm=POST_ROWS, n_tokens=Tp + Ts)
    y_p, y_s = _moe(xn_tiles, x1_tiles, route_t, cnt, Tp, Tp + Ts, w_gate, b_gate, w_up, b_up, w_down, b_down)
    return (y_p.reshape(B, S, D), y_s[:Ts].reshape(DB, L, D), mk_p, mv_p, conv_p, ssm_p,
            heads(mk_s), heads(mv_s), conv_s_new, ssm_s_new)


def kernel(x_prompt, x_sample, cache_k, cache_v, state_conv, state_ssm, page_table, ln1_w, w_in, w_conv,
           a_log, dt_bias, dn_norm_w, q_norm_w, k_norm_w, w_o, ln2_w, w_router, b_router, w_gate, b_gate,
           w_up, b_up, w_down, b_down):
    weights = (ln1_w, w_in, w_conv, a_log, dt_bias, dn_norm_w, q_norm_w, k_norm_w, w_o, ln2_w, w_router,
               b_router, w_gate, b_gate, w_up, b_up, w_down, b_down)
    depth = w_in.shape[0]
    yp, ys = x_prompt, x_sample
    outs = [[] for _ in range(8)]
    for l in range(depth):
        res = _layer(yp, ys, cache_k[l], cache_v[l], state_conv[l], state_ssm[l], page_table,
                     tuple(w[l] for w in weights))
        yp, ys = res[0], res[1]
        for acc, r in zip(outs, res[2:]):
            acc.append(r)
    return (yp, ys) + tuple(jnp.stack(o) for o in outs)
```

```python
import functools
import math

import jax
import jax.numpy as jnp
from jax import lax
from jax.experimental import pallas as pl
from jax.experimental.pallas import tpu as pltpu

HEAD_DIM = 128
DN_HEADS = 4
MOBA_HEADS = 4
DN_WIDTH = DN_HEADS * HEAD_DIM
ATT_WIDTH = MOBA_HEADS * HEAD_DIM
CONV_W = 4
CONV_DIM = 3 * DN_WIDTH
DN_CHUNK = 64
MOBA_BLOCK = 256
MOBA_TOPK = 3
N_EXPERTS = 32
TOP_K = 4
SWIGLU_LIMIT = 7.0
SWIGLU_ALPHA = 1.702
EPS = 1e-6

LANES = 128
SUBLANES = 8
VMEM_LIMIT = 56 * 1024 * 1024
NEG_BIG = -1e30

PROJ_ROWS = 512
GDN_ROWS = 256
POST_ROWS = 512
TOK_TILE = 256
MOE_ROWS = 1024
ROUTE_ROWS = 16

_HI = lax.Precision.HIGHEST
_F32 = jnp.float32
_BF16 = jnp.bfloat16


def _dot(a, b, precision=None):
    return jnp.dot(a, b, preferred_element_type=_F32, precision=precision)


def _dot_nt(a, b, precision=None):
    return lax.dot_general(a, b, (((1,), (1,)), ((), ())),
                           preferred_element_type=_F32, precision=precision)


def _dot_tn(a, b, precision=None):
    return lax.dot_general(a, b, (((0,), (0,)), ((), ())),
                           preferred_element_type=_F32, precision=precision)


def _split2(x):
    hi = x.astype(_BF16)
    return hi, (x - hi.astype(_F32)).astype(_BF16)


def _split3(x):
    hi = x.astype(_BF16)
    r = x - hi.astype(_F32)
    mid = r.astype(_BF16)
    return hi, mid, (r - mid.astype(_F32)).astype(_BF16)


def _dot3(a, b, dot=_dot):
    return dot(a[0], b[0]) + dot(a[1], b[0]) + dot(a[0], b[1])


def _rms(x, w):
    return x * lax.rsqrt(jnp.mean(x * x, axis=-1, keepdims=True) + EPS) * w


def _store_token_tiles(ref, x):
    n = x.shape[0]
    for j in range(SUBLANES):
        ref[pl.ds(j, n, stride=SUBLANES), :] = x[:, j * LANES:(j + 1) * LANES]


def _load_token_tiles(ref, n, j):
    return ref[pl.ds(j, n, stride=SUBLANES), :]


def _silu(x):
    return x * jax.nn.sigmoid(x)


def _params(*sem):
    return pltpu.CompilerParams(dimension_semantics=sem, vmem_limit_bytes=VMEM_LIMIT)


def _proj_kernel(x_ref, ln_ref, w_ref, qn_ref, kn_ref,
                 u_ref, z_ref, ab_ref, mq_ref, mk_ref, mv_ref, *, heads_out):
    xn = _rms(x_ref[...], ln_ref[...]).astype(_BF16)
    h = _dot(xn, w_ref[...])
    u_ref[...] = h[:, :CONV_DIM]
    z_ref[...] = h[:, CONV_DIM:CONV_DIM + DN_WIDTH]
    o = CONV_DIM + DN_WIDTH
    ab_ref[...] = h[:, o + 3 * ATT_WIDTH:]
    for hd in range(MOBA_HEADS):
        sl = slice(o + hd * HEAD_DIM, o + (hd + 1) * HEAD_DIM)
        q = _rms(h[:, sl], qn_ref[...])
        k = _rms(h[:, sl.start + ATT_WIDTH:sl.stop + ATT_WIDTH], kn_ref[...])
        v = h[:, sl.start + 2 * ATT_WIDTH:sl.stop + 2 * ATT_WIDTH]
        if heads_out:
            mq_ref[0, hd] = q
            mk_ref[0, hd] = k
            mv_ref[0, hd] = v
        else:
            hs = slice(hd * HEAD_DIM, (hd + 1) * HEAD_DIM)
            mq_ref[:, hs] = q
            mk_ref[:, hs] = k
            mv_ref[:, hs] = v


def _proj(x2d, ln_w, w_all, qn_w, kn_w, *, tm, seq=None):
    T, D = x2d.shape
    n_all = w_all.shape[1]
    grid = (T // tm,)
    row = lambda t: (t, 0)
    const = lambda t: (0, 0)
    if seq is not None:
        B, S = seq
        per = S // tm
        hshape = jax.ShapeDtypeStruct((B, MOBA_HEADS, S, HEAD_DIM), _F32)
        hspec = pl.BlockSpec((1, MOBA_HEADS, tm, HEAD_DIM), lambda t: (t // per, 0, t % per, 0))
    else:
        hshape = jax.ShapeDtypeStruct((T, ATT_WIDTH), _F32)
        hspec = pl.BlockSpec((tm, ATT_WIDTH), row)
    return pl.pallas_call(
        functools.partial(_proj_kernel, heads_out=seq is not None),
        grid=grid,
        in_specs=[pl.BlockSpec((tm, D), row), pl.BlockSpec((1, D), const),
                  pl.BlockSpec((D, n_all), const), pl.BlockSpec((1, HEAD_DIM), const),
                  pl.BlockSpec((1, HEAD_DIM), const)],
        out_specs=[pl.BlockSpec((tm, CONV_DIM), row), pl.BlockSpec((tm, DN_WIDTH), row),
                   pl.BlockSpec((tm, LANES), row), hspec, hspec, hspec],
        out_shape=[jax.ShapeDtypeStruct((T, CONV_DIM), _F32),
                   jax.ShapeDtypeStruct((T, DN_WIDTH), _F32),
                   jax.ShapeDtypeStruct((T, LANES), _F32), hshape, hshape, hshape],
        compiler_params=_params("parallel"),
        name="proj",
    )(x2d, ln_w, w_all, qn_w, kn_w)


def _gdn_kernel(u_ref, z_ref, ab_ref, conv0_ref, ssm0_ref, wconv_ref, alog_ref, dtb_ref, dnw_ref,
                od_ref, ssm_ref, up_scr, s_scr, *, lt, valid_len):
    i = pl.program_id(1)
    c = DN_CHUNK
    nc = lt // c
    heads = range(DN_HEADS)

    @pl.when(i == 0)
    def _():
        s_scr[...] = ssm0_ref[0]
        up_scr[0:SUBLANES, :] = conv0_ref[0]

    up_scr[SUBLANES:SUBLANES + lt, :] = u_ref[0]
    w = wconv_ref[...]
    base = SUBLANES - (CONV_W - 1)
    y = up_scr[base:base + lt, :] * w[0:1, :]
    for t in range(1, CONV_W):
        y = y + up_scr[base + t:base + t + lt, :] * w[t:t + 1, :]
    up_scr[0:SUBLANES, :] = up_scr[lt:lt + SUBLANES, :]
    qkv = _silu(y)

    masked = valid_len < lt
    if masked:
        rowv = lax.broadcasted_iota(jnp.int32, (lt, 1), 0) < valid_len
        qkv = jnp.where(rowv, qkv, 0.0)

    ab = ab_ref[0]
    lane = lax.broadcasted_iota(jnp.int32, (lt, LANES), 1)
    xg = ab + dtb_ref[...]
    softplus = jnp.maximum(xg, 0.0) + jnp.log(1.0 + jnp.exp(-jnp.abs(xg)))
    gb = jnp.where(lane < DN_HEADS, -jnp.exp(alog_ref[...]) * softplus, jax.nn.sigmoid(ab))
    if masked:
        gb = jnp.where(rowv, gb, 0.0)

    rows = lax.broadcasted_iota(jnp.int32, (c, c), 0)
    cols = lax.broadcasted_iota(jnp.int32, (c, c), 1)
    causal = cols <= rows
    strict = cols < rows
    eye = jnp.where(rows == cols, 1.0, 0.0)
    tril16 = jnp.where(causal, 1.0, 0.0).astype(_BF16)
    prow = lax.broadcasted_iota(jnp.int32, (DN_HEADS * c, LANES), 0) // c
    plane = lax.broadcasted_iota(jnp.int32, (DN_HEADS * c, LANES), 1)
    pick16 = jnp.where(prow == plane, 1.0, 0.0).astype(_BF16)

    cum, cum_t = [], []
    for ci in range(nc):
        g3 = _split3(gb[ci * c:(ci + 1) * c, :])
        cm = _dot(tril16, g3[0]) + _dot(tril16, g3[1]) + _dot(tril16, g3[2])
        c3 = _split3(cm)
        cum.append(cm)
        cum_t.append(_dot_nt(pick16, c3[0]) + _dot_nt(pick16, c3[1]) + _dot_nt(pick16, c3[2]))
    probs = [(ci, hd) for ci in range(nc) for hd in heads]

    def rows_of(ci):
        return slice(ci * c, (ci + 1) * c)

    q_l, k_l, v_l = [], [], []
    for ci, hd in probs:
        q = qkv[rows_of(ci), hd * HEAD_DIM:(hd + 1) * HEAD_DIM]
        k = qkv[rows_of(ci), DN_WIDTH + hd * HEAD_DIM:DN_WIDTH + (hd + 1) * HEAD_DIM]
        q_l.append(q * lax.rsqrt(jnp.sum(q * q, axis=-1, keepdims=True) + EPS) * (HEAD_DIM ** -0.5))
        k_l.append(k * lax.rsqrt(jnp.sum(k * k, axis=-1, keepdims=True) + EPS))
        v_l.append(qkv[rows_of(ci), 2 * DN_WIDTH + hd * HEAD_DIM:2 * DN_WIDTH + (hd + 1) * HEAD_DIM])
    gcum_l = [cum[ci][:, hd:hd + 1] for ci, hd in probs]
    beta_l = [gb[rows_of(ci), DN_HEADS + hd:DN_HEADS + hd + 1] for ci, hd in probs]
    decay_l = [jnp.exp(jnp.where(causal, g - cum_t[ci][hd * c:(hd + 1) * c, :], -jnp.inf))
               for (ci, hd), g in zip(probs, gcum_l)]
    k16_l = [k.astype(_BF16) for k in k_l]
    kb_l = [k * b for k, b in zip(k_l, beta_l)]
    low_l = [jnp.where(strict, _dot_nt(kb.astype(_BF16), k16) * d, 0.0)
             for kb, k16, d in zip(kb_l, k16_l, decay_l)]
    attn16_l = [(_dot_nt(q.astype(_BF16), k16) * d).astype(_BF16)
                for q, k16, d in zip(q_l, k16_l, decay_l)]

    low2_l = [_split2(l) for l in low_l]
    x_l = [eye - l for l in low_l]
    m_l = [_dot3(l2, l2) for l2 in low2_l]
    n_fac = max(1, int(math.ceil(math.log2(c)))) - 1
    for f in range(n_fac):
        m2_l = [_split2(m) for m in m_l]
        x_l = [x + _dot3(_split2(x), m2) for x, m2 in zip(x_l, m2_l)]
        if f + 1 < n_fac:
            m_l = [_dot3(m2, m2) for m2 in m2_l]
    eg_l = [jnp.exp(g) for g in gcum_l]
    uw_l = [_dot3(_split2(x), _split2(jnp.concatenate([v * b, kb * eg], axis=1)))
            for x, v, b, kb, eg in zip(x_l, v_l, beta_l, kb_l, eg_l)]
    u_l = [uw[:, :HEAD_DIM] for uw in uw_l]
    w16_l = [uw[:, HEAD_DIM:].astype(_BF16) for uw in uw_l]
    qe16_l = [(q * eg).astype(_BF16) for q, eg in zip(q_l, eg_l)]
    glast_l = [g[c - 1:c, :] for g in gcum_l]
    kdec16_l = [(k * jnp.exp(gl - g)).astype(_BF16) for k, gl, g in zip(k_l, glast_l, gcum_l)]
    eglast_l = [jnp.exp(gl) for gl in glast_l]

    state = [s_scr[hd] for hd in heads]
    for ci in range(nc):
        p0 = ci * DN_HEADS
        s16 = [s.astype(_BF16) for s in state]
        vnew = [u_l[p0 + hd] - _dot(w16_l[p0 + hd], s16[hd]) for hd in heads]
        vnew16 = [v.astype(_BF16) for v in vnew]
        o = [_dot(qe16_l[p0 + hd], s16[hd]) + _dot(attn16_l[p0 + hd], vnew16[hd]) for hd in heads]
        state = [state[hd] * eglast_l[p0 + hd] + _dot_tn(kdec16_l[p0 + hd], vnew16[hd]) for hd in heads]
        for hd in heads:
            zz = z_ref[0, rows_of(ci), hd * HEAD_DIM:(hd + 1) * HEAD_DIM]
            od_ref[0, rows_of(ci), hd * HEAD_DIM:(hd + 1) * HEAD_DIM] = (
                _rms(o[hd], dnw_ref[...]) * _silu(zz)).astype(_BF16)
    for hd in heads:
        s_scr[hd] = state[hd]
    ssm_ref[0] = s_scr[...]


def _gdn(u, z, ab, conv0, ssm0, w_conv, alog_row, dtb_row, dn_w, *, lt, valid_len):
    B, L, _ = u.shape
    grid = (B, L // lt)
    tile = lambda b, i: (b, i, 0)
    perb3 = lambda b, i: (b, 0, 0)
    const = lambda b, i: (0, 0)
    return pl.pallas_call(
        functools.partial(_gdn_kernel, lt=lt, valid_len=valid_len),
        grid=grid,
        in_specs=[pl.BlockSpec((1, lt, CONV_DIM), tile), pl.BlockSpec((1, lt, DN_WIDTH), tile),
                  pl.BlockSpec((1, lt, LANES), tile), pl.BlockSpec((1, SUBLANES, CONV_DIM), perb3),
                  pl.BlockSpec((1, DN_HEADS, HEAD_DIM, HEAD_DIM), lambda b, i: (b, 0, 0, 0)),
                  pl.BlockSpec((CONV_W, CONV_DIM), const), pl.BlockSpec((1, LANES), const),
                  pl.BlockSpec((1, LANES), const), pl.BlockSpec((1, HEAD_DIM), const)],
        out_specs=[pl.BlockSpec((1, lt, DN_WIDTH), tile),
                   pl.BlockSpec((1, DN_HEADS, HEAD_DIM, HEAD_DIM), lambda b, i: (b, 0, 0, 0))],
        out_shape=[jax.ShapeDtypeStruct((B, L, DN_WIDTH), _BF16),
                   jax.ShapeDtypeStruct((B, DN_HEADS, HEAD_DIM, HEAD_DIM), _F32)],
        scratch_shapes=[pltpu.VMEM((lt + 2 * SUBLANES, CONV_DIM), _F32),
                        pltpu.VMEM((DN_HEADS, HEAD_DIM, HEAD_DIM), _F32)],
        compiler_params=_params("parallel", "arbitrary"),
        name="gdn",
    )(u, z, ab, conv0, ssm0, w_conv, alog_row, dtb_row, dn_w)


def _topk_rows(g, n_sel):
    r = g.shape[0]
    row = lax.broadcasted_iota(jnp.int32, g.shape, 0)
    sel = jnp.zeros(g.shape, _F32)
    for _ in range(n_sel):
        m = jnp.max(g, axis=0, keepdims=True)
        idx = jnp.min(jnp.where(g == m, row, r), axis=0, keepdims=True)
        pick = row == idx
        sel = jnp.where(pick, 1.0, sel)
        g = jnp.where(pick, -jnp.inf, g)
    return sel


def _moba_prompt_kernel(q_ref, k_ref, v_ref, c_ref, o_ref, ps_ref, kb_scr, vt_scr, kmean_scr, sel_scr,
                        *, nblk):
    qt = pl.program_id(2)
    blk = MOBA_BLOCK
    scale = HEAD_DIM ** -0.5

    ps_ref[...] = jnp.sum(c_ref[...], axis=1)

    @pl.when(qt == 0)
    def _():
        kmean_scr[...] = jnp.zeros(kmean_scr.shape, _F32)
        for j in range(nblk):
            kj = k_ref[0, 0, j * blk:(j + 1) * blk, :]
            kb_scr[j] = kj.astype(_BF16)
            kmean_scr[j:j + 1, :] = jnp.mean(kj, axis=0, keepdims=True)
            vt_scr[j] = v_ref[0, 0, j * blk:(j + 1) * blk, :].T.astype(_BF16)

    q = q_ref[0, 0]
    gate = _dot_nt(kmean_scr[...], q, _HI)
    brow = lax.broadcasted_iota(jnp.int32, gate.shape, 0)
    valid = brow < qt
    sel = _topk_rows(jnp.where(valid, gate, -jnp.inf), min(MOBA_TOPK, nblk))
    sel_scr[...] = jnp.where(valid, sel, 0.0)

    q16 = (q * scale).astype(_BF16)
    kpos = lax.broadcasted_iota(jnp.int32, (blk, blk), 0)
    qpos = lax.broadcasted_iota(jnp.int32, (blk, blk), 1)
    def scores(j):
        return jnp.where(sel_scr[pl.ds(j, 1), :] > 0.0, _dot_nt(kb_scr[j], q16), NEG_BIG)

    s = jnp.where(kpos <= qpos, _dot_nt(kb_scr[qt], q16), NEG_BIG)
    m = jnp.max(s, axis=0, keepdims=True)
    p = jnp.exp(s - m)
    l = jnp.sum(p, axis=0, keepdims=True)

    def body(j, carry):
        s_cur, p_prev, j_prev, m, l, acc = carry
        s_next = scores(j + 1)
        pv = _dot(vt_scr[j_prev], p_prev)
        m_new = jnp.maximum(m, jnp.max(s_cur, axis=0, keepdims=True))
        alpha = jnp.exp(m - m_new)
        p = jnp.exp(s_cur - m_new)
        l = alpha * l + jnp.sum(p, axis=0, keepdims=True)
        return s_next, p.astype(_BF16), j, m_new, l, alpha * (acc + pv)

    init = (scores(0), p.astype(_BF16), qt, m, l, jnp.zeros((HEAD_DIM, blk), _F32))
    _, p_prev, j_prev, m, l, acc = lax.fori_loop(0, qt, body, init)
    acc = acc + _dot(vt_scr[j_prev], p_prev)
    o_ref[0] = (acc / l).T.astype(_BF16)


def _moba_prompt(q, k, v, slabs):
    B, H, S, dh = q.shape
    nblk = S // MOBA_BLOCK
    nb8 = -(-nblk // SUBLANES) * SUBLANES
    assert S % MOBA_BLOCK == 0
    n_slabs, ps, _ = slabs.shape
    steps = B * H * nblk
    share = n_slabs // steps
    assert share * steps == n_slabs and share % SUBLANES == 0, "key cache does not split evenly over the grid"
    full = lambda b, h, t: (b, h, 0, 0)
    flat = lambda b, h, t: (b * H + h) * nblk + t
    return pl.pallas_call(
        functools.partial(_moba_prompt_kernel, nblk=nblk),
        grid=(B, H, nblk),
        in_specs=[pl.BlockSpec((1, 1, MOBA_BLOCK, dh), lambda b, h, t: (b, h, t, 0)),
                  pl.BlockSpec((1, 1, S, dh), full), pl.BlockSpec((1, 1, S, dh), full),
                  pl.BlockSpec((share, ps, dh), lambda b, h, t: (flat(b, h, t), 0, 0))],
        out_specs=[pl.BlockSpec((1, MOBA_BLOCK, dh), lambda b, h, t: (b, t, h)),
                   pl.BlockSpec((share, dh), lambda b, h, t: (flat(b, h, t), 0))],
        out_shape=[jax.ShapeDtypeStruct((B, S, H * dh), _BF16),
                   jax.ShapeDtypeStruct((n_slabs, dh), _F32)],
        scratch_shapes=[pltpu.VMEM((nblk, MOBA_BLOCK, dh), _BF16),
                        pltpu.VMEM((nblk, dh, MOBA_BLOCK), _BF16),
                        pltpu.VMEM((nb8, dh), _F32),
                        pltpu.VMEM((nb8, MOBA_BLOCK), _F32)],
        compiler_params=_params("parallel", "parallel", "arbitrary"),
        name="moba_prompt",
    )(q, k, v, slabs)


def _moba_sel_kernel(pt_ref, psum_ref, q_ref, sel_ref, km_scr, *, n_pages, ppb, n_blk, n_sel, rows_per_blk):
    b = pl.program_id(0)
    km_scr[...] = jnp.zeros(km_scr.shape, _F32)

    def fill(j, carry):
        acc = psum_ref[pl.ds(pt_ref[b * n_pages + j * ppb], 1), :]
        for r in range(1, ppb):
            acc = acc + psum_ref[pl.ds(pt_ref[b * n_pages + j * ppb + r], 1), :]
        km_scr[pl.ds(j, 1), :] = acc * (1.0 / rows_per_blk)
        return carry

    lax.fori_loop(0, n_blk, fill, 0)
    lane = lax.broadcasted_iota(jnp.int32, (SUBLANES, LANES), 1)
    for hd in range(MOBA_HEADS):
        g = _dot_nt(q_ref[0, hd], km_scr[:, hd * HEAD_DIM:(hd + 1) * HEAD_DIM], _HI)
        g = jnp.where(lane < n_blk, g, -jnp.inf)
        out = jnp.zeros((SUBLANES, LANES), jnp.int32)
        for r in range(n_sel):
            m = jnp.max(g, axis=1, keepdims=True)
            idx = jnp.min(jnp.where(g == m, lane, LANES), axis=1, keepdims=True)
            out = jnp.where(lane == r, idx, out)
            g = jnp.where(lane == idx, -jnp.inf, g)
        sel_ref[0, hd] = out


def _moba_sel(page_table, psum, q8, *, ppb, n_sel, rows_per_blk):
    DB, n_pages = page_table.shape
    n_blk = n_pages // ppb
    assert n_blk <= LANES
    n_phys, width = psum.shape
    return pl.pallas_call(
        functools.partial(_moba_sel_kernel, n_pages=n_pages, ppb=ppb, n_blk=n_blk, n_sel=n_sel,
                          rows_per_blk=rows_per_blk),
        grid_spec=pltpu.PrefetchScalarGridSpec(
            num_scalar_prefetch=1,
            grid=(DB,),
            in_specs=[pl.BlockSpec((n_phys, width), lambda b, pt: (0, 0)),
                      pl.BlockSpec((1, MOBA_HEADS, SUBLANES, HEAD_DIM), lambda b, pt: (b, 0, 0, 0))],
            out_specs=pl.BlockSpec((1, MOBA_HEADS, SUBLANES, LANES), lambda b, pt: (b, 0, 0, 0)),
            scratch_shapes=[pltpu.VMEM((LANES, width), _F32)]),
        out_shape=jax.ShapeDtypeStruct((DB, MOBA_HEADS, SUBLANES, LANES), jnp.int32),
        compiler_params=_params("arbitrary"),
        name="moba_sel",
    )(page_table.reshape(-1), psum, q8)


def _moba_sample_kernel(phys_ref, q_ref, kn_ref, vn_ref, ck_hbm, cv_hbm, o_ref, kbuf, vbuf, sem,
                        *, n_pg, n_new, n_heads, n_steps):
    step = pl.program_id(0)
    per = n_new * n_pg
    scale = HEAD_DIM ** -0.5

    def page_copies(st, slot):
        hd = st % n_heads
        cps = []
        for j in range(per):
            page = phys_ref[st * per + j]
            cps.append(pltpu.make_async_copy(ck_hbm.at[page, hd], kbuf.at[slot, j], sem.at[slot]))
            cps.append(pltpu.make_async_copy(cv_hbm.at[page, hd], vbuf.at[slot, j], sem.at[slot]))
        return cps

    def attend(slot):
        q = q_ref[0, 0]
        q16 = q.astype(_BF16)
        kn = kn_ref[0, 0]
        vn = vn_ref[0, 0]
        rowq = lax.broadcasted_iota(jnp.int32, (SUBLANES, 1), 0)
        s_own = []
        for c in range(n_new):
            sc = jnp.sum(q * kn[c:c + 1, :], axis=-1, keepdims=True) * scale
            s_own.append(jnp.where(rowq >= c, sc, NEG_BIG))
        m_own = s_own[0]
        for sc in s_own[1:]:
            m_own = jnp.maximum(m_own, sc)
        out = jnp.zeros((SUBLANES, HEAD_DIM), _F32)
        for t in range(n_new):
            pages = range(t * n_pg, (t + 1) * n_pg)
            s_past = [_dot_nt(q16, kbuf[slot, j].astype(_BF16)) * scale for j in pages]
            m = m_own
            for sp in s_past:
                m = jnp.maximum(m, jnp.max(sp, axis=-1, keepdims=True))
            l = jnp.zeros((SUBLANES, 1), _F32)
            acc = jnp.zeros((SUBLANES, HEAD_DIM), _F32)
            for sp, j in zip(s_past, pages):
                p = jnp.exp(sp - m)
                l = l + jnp.sum(p, axis=-1, keepdims=True)
                acc = acc + _dot(p.astype(_BF16), vbuf[slot, j].astype(_BF16))
            for c, sc in enumerate(s_own):
                p = jnp.exp(sc - m)
                l = l + p
                acc = acc + p * vn[c:c + 1, :]
            out = jnp.where(rowq == t, acc / l, out)
        o_ref[0, 0] = out

    def run(slot):
        if slot == 0:
            @pl.when(step == 0)
            def _():
                for cp in page_copies(0, 0):
                    cp.start()

        @pl.when(step + 1 < n_steps)
        def _():
            for cp in page_copies(step + 1, 1 - slot):
                cp.start()

        for cp in page_copies(step, slot):
            cp.wait()
        attend(slot)

    for slot in range(2):
        @pl.when(step % 2 == slot)
        def _():
            run(slot)


def _moba_sample(phys, q8, kn8, vn8, cache_k, cache_v, *, n_new):
    DB, H, _, dh = q8.shape
    ps = cache_k.shape[2]
    n_pg = phys.shape[0] // (DB * H * n_new)
    steps = DB * H
    small = pl.BlockSpec((1, 1, SUBLANES, dh), lambda s, ph: (s // H, s % H, 0, 0))
    return pl.pallas_call(
        functools.partial(_moba_sample_kernel, n_pg=n_pg, n_new=n_new, n_heads=H, n_steps=steps),
        grid_spec=pltpu.PrefetchScalarGridSpec(
            num_scalar_prefetch=1,
            grid=(steps,),
            in_specs=[small, small, small, pl.BlockSpec(memory_space=pl.ANY),
                      pl.BlockSpec(memory_space=pl.ANY)],
            out_specs=small,
            scratch_shapes=[pltpu.VMEM((2, n_new * n_pg, ps, dh), _F32),
                            pltpu.VMEM((2, n_new * n_pg, ps, dh), _F32),
                            pltpu.SemaphoreType.DMA((2,))]),
        out_shape=jax.ShapeDtypeStruct((DB, H, SUBLANES, dh), _F32),
        compiler_params=_params("arbitrary"),
        name="moba_sample",
    )(phys, q8, kn8, vn8, cache_k, cache_v)


def _post_kernel(xp_ref, odp_ref, omp_ref, xs_ref, ods_ref, oms_ref, wo1_ref, wo2_ref, ln_ref,
                 wrh_ref, wrl_ref, br_ref, x1_ref, xn_ref, route_ref, cnt_ref, carry_scr,
                 *, tm, n_prompt_tiles, n_tokens):
    i = pl.program_id(0)

    @pl.when(i == 0)
    def _():
        carry_scr[...] = jnp.zeros(carry_scr.shape, _F32)

    is_p = i < n_prompt_tiles
    x = jnp.where(is_p, xp_ref[...], xs_ref[...])
    od = jnp.where(is_p, odp_ref[...], ods_ref[...])
    om = jnp.where(is_p, omp_ref[...], oms_ref[...])
    x1 = x + _dot(od, wo1_ref[...]) + _dot(om, wo2_ref[...])
    _store_token_tiles(x1_ref, x1)
    xn = _rms(x1, ln_ref[...])
    _store_token_tiles(xn_ref, xn)
    xh, xl = _split2(xn)
    logits = (_dot_nt(wrh_ref[...], xh) + _dot_nt(wrh_ref[...], xl) + _dot_nt(wrl_ref[...], xh)) + br_ref[...]

    erow = lax.broadcasted_iota(jnp.int32, (N_EXPERTS, tm), 0)
    g = logits
    vals, picks = [], []
    for _ in range(TOP_K):
        m = jnp.max(g, axis=0, keepdims=True)
        idx = jnp.min(jnp.where(g == m, erow, N_EXPERTS), axis=0, keepdims=True)
        pick = erow == idx
        vals.append(m)
        picks.append((idx, pick))
        g = jnp.where(pick, -jnp.inf, g)
    es = [jnp.exp(v - vals[0]) for v in vals]
    den = es[0]
    for e in es[1:]:
        den = den + e

    rows = lax.broadcasted_iota(jnp.int32, (tm, tm), 0)
    cols = lax.broadcasted_iota(jnp.int32, (tm, tm), 1)
    earlier = jnp.where(rows < cols, 1.0, 0.0).astype(_BF16)
    real = jnp.where((i * tm + lax.broadcasted_iota(jnp.int32, (1, tm), 1)) < n_tokens, 1.0, 0.0)
    base = carry_scr[...]
    tok = lax.broadcasted_iota(jnp.int32, (1, tm), 1)
    ranks = [jnp.zeros((1, tm), _F32) for _ in picks]
    for sub in range(tm // TOK_TILE):
        in_sub = jnp.where((tok >= sub * TOK_TILE) & (tok < (sub + 1) * TOK_TILE), real, 0.0)
        for k, (idx, pick) in enumerate(picks):
            onehot = jnp.where(pick, in_sub, 0.0)
            pref = _dot(onehot.astype(_BF16), earlier) + base
            ranks[k] = ranks[k] + jnp.sum(onehot * pref, axis=0, keepdims=True)
            base = base + jnp.sum(onehot, axis=1, keepdims=True)
    carry_scr[...] = base
    cnt_ref[...] = base
    record = ([idx.astype(_F32) for idx, _ in picks] + [e / den for e in es] + ranks
              + [jnp.zeros((route_ref.shape[0] - 3 * TOP_K, tm), _F32)])
    route_ref[...] = jnp.concatenate(record, axis=0)


def _post(xp, odp, omp, xs, ods, oms, wo1, wo2, ln_w, wr_hi, wr_lo, b_r, *, tm, n_tokens):
    Tp, D = xp.shape
    npt = Tp // tm
    T = Tp + tm
    prow = lambda t: (jnp.minimum(t, npt - 1), 0)
    srow = lambda t: (0, 0)
    row = lambda t: (t, 0)
    const = lambda t: (0, 0)
    return pl.pallas_call(
        functools.partial(_post_kernel, tm=tm, n_prompt_tiles=npt, n_tokens=n_tokens),
        grid=(npt + 1,),
        in_specs=[pl.BlockSpec((tm, D), prow), pl.BlockSpec((tm, DN_WIDTH), prow),
                  pl.BlockSpec((tm, ATT_WIDTH), prow),
                  pl.BlockSpec((tm, D), srow), pl.BlockSpec((tm, DN_WIDTH), srow),
                  pl.BlockSpec((tm, ATT_WIDTH), srow),
                  pl.BlockSpec((DN_WIDTH, D), const), pl.BlockSpec((ATT_WIDTH, D), const),
                  pl.BlockSpec((1, D), const), pl.BlockSpec((N_EXPERTS, D), const),
                  pl.BlockSpec((N_EXPERTS, D), const), pl.BlockSpec((N_EXPERTS, 1), const)],
        out_specs=[pl.BlockSpec((tm * SUBLANES, LANES), row), pl.BlockSpec((tm * SUBLANES, LANES), row),
                   pl.BlockSpec((ROUTE_ROWS, tm), lambda t: (0, t)), pl.BlockSpec((N_EXPERTS, 1), const)],
        out_shape=[jax.ShapeDtypeStruct((T * SUBLANES, LANES), _F32),
                   jax.ShapeDtypeStruct((T * SUBLANES, LANES), _F32),
                   jax.ShapeDtypeStruct((ROUTE_ROWS, T), _F32), jax.ShapeDtypeStruct((N_EXPERTS, 1), _F32)],
        scratch_shapes=[pltpu.VMEM((N_EXPERTS, 1), _F32)],
        compiler_params=_params("arbitrary"),
        name="post",
    )(xp, odp, omp, xs, ods, oms, wo1, wo2, ln_w, wr_hi, wr_lo, b_r)


RUN_PIECES = tuple(1 << b for b in reversed(range(TOK_TILE.bit_length())))
RUN_RARE = 64


def _all_run_copies(make_copy, start_ref, len_ref, tile):
    runs = []
    pool_start = jnp.int32(0)
    for e in range(N_EXPERTS):
        n = len_ref[tile * N_EXPERTS + e]
        first = start_ref[tile * N_EXPERTS + e]
        pieces = []
        for piece in RUN_PIECES:
            done = n & (-2 * piece)
            pieces.append((piece, make_copy(first + done, pool_start + done, piece)))
        runs.append((n, pieces))
        pool_start = pool_start + n
    return runs


def _for_pieces(runs, act):
    for n, pieces in runs:
        def some(which):
            for piece, cp in pieces:
                if which(piece):
                    @pl.when((n & piece) != 0)
                    def _():
                        act(cp)

        @pl.when(n >= RUN_RARE)
        def _():
            some(lambda piece: piece >= RUN_RARE)

        some(lambda piece: piece < RUN_RARE)


def _start_all(runs):
    _for_pieces(runs, lambda cp: cp.start())


def _wait_all(runs):
    _for_pieces(runs, lambda cp: cp.wait())


def _rows(ref, first_row, n_rows):
    return ref.at[pl.ds(pl.multiple_of(first_row * SUBLANES, SUBLANES), n_rows * SUBLANES)]


def _dispatch_kernel(start_ref, len_ref, tail_ref, nu_ref, slot_hbm, xn_ref, xs_hbm, idx_smem0, idx_smem1, pool, zero_scr,
                     sem_idx, sem_zero, sem_rows, *, tm, n_blocks, n_tiles):
    i = pl.program_id(0)
    n_asg = tm * TOP_K
    idx_smem = (idx_smem0, idx_smem1)

    def idx_copy(tile, slot):
        return pltpu.make_async_copy(slot_hbm.at[pl.ds(tile * n_asg, n_asg)], idx_smem[slot], sem_idx.at[slot])

    @pl.when(i == 0)
    def _():
        zero_scr[...] = jnp.zeros(zero_scr.shape, _F32)

        def clear(first_row):
            first = pl.multiple_of(first_row * SUBLANES, MOE_ROWS * SUBLANES)
            cp = pltpu.make_async_copy(zero_scr, xs_hbm.at[pl.ds(first, MOE_ROWS * SUBLANES)], sem_zero)
            cp.start()
            cp.wait()

        for e in range(N_EXPERTS):
            @pl.when(tail_ref[e] >= 0)
            def _():
                clear(tail_ref[e])

        def clear_block(b, carry):
            clear(b * MOE_ROWS)
            return carry

        lax.fori_loop(nu_ref[0], n_blocks, clear_block, 0)

    def run_copies(tile, slot):
        return _all_run_copies(
            lambda hbm_row, pool_row, n: pltpu.make_async_copy(
                _rows(pool.at[slot], pool_row, n), _rows(xs_hbm, hbm_row, n), sem_rows.at[slot]),
            start_ref, len_ref, tile)

    def step(slot):
        if slot == 0:
            @pl.when(i == 0)
            def _():
                idx_copy(0, 0).start()

        @pl.when(i + 1 < n_tiles)
        def _():
            idx_copy(i + 1, 1 - slot).start()

        idx_copy(i, slot).wait()

        @pl.when(i >= 2)
        def _():
            _wait_all(run_copies(i - 2, slot))

        def pack(t, carry):
            tile = xn_ref[pl.ds(pl.multiple_of(t * SUBLANES, SUBLANES), SUBLANES), :]
            for k in range(TOP_K):
                row = pl.multiple_of(idx_smem[slot][k * tm + t] * SUBLANES, SUBLANES)
                pool[slot, pl.ds(row, SUBLANES), :] = tile
            return carry

        lax.fori_loop(0, tm, pack, 0, unroll=4)
        _start_all(run_copies(i, slot))

        @pl.when(i == n_tiles - 1)
        def _():
            _wait_all(run_copies(i, slot))

            @pl.when(i >= 1)
            def _():
                _wait_all(run_copies(i - 1, 1 - slot))

    for slot in range(2):
        @pl.when(i % 2 == slot)
        def _():
            step(slot)


def _dispatch(run_start, run_len, tail, n_used, slot, xn_tiles, n_tiles, n_blocks, *, tm):
    pre = lambda i, *_: (i, 0)
    return pl.pallas_call(
        functools.partial(_dispatch_kernel, tm=tm, n_blocks=n_blocks, n_tiles=n_tiles),
        grid_spec=pltpu.PrefetchScalarGridSpec(
            num_scalar_prefetch=4,
            grid=(n_tiles,),
            in_specs=[pl.BlockSpec(memory_space=pl.ANY), pl.BlockSpec((tm * SUBLANES, LANES), pre)],
            out_specs=pl.BlockSpec(memory_space=pl.ANY),
            scratch_shapes=[pltpu.SMEM((tm * TOP_K,), jnp.int32), pltpu.SMEM((tm * TOP_K,), jnp.int32),
                            pltpu.VMEM((2, (tm * TOP_K + 1) * SUBLANES, LANES), _F32),
                            pltpu.VMEM((MOE_ROWS * SUBLANES, LANES), _F32),
                            pltpu.SemaphoreType.DMA((2,)), pltpu.SemaphoreType.DMA, pltpu.SemaphoreType.DMA((2,))]),
        out_shape=jax.ShapeDtypeStruct((n_blocks * MOE_ROWS * SUBLANES, LANES), _F32),
        compiler_params=_params("arbitrary"),
        name="dispatch",
    )(run_start, run_len, tail, n_used, slot, xn_tiles)


def _moe_kernel(be_ref, nu_ref, x_ref, wg_ref, bg_ref, wu_ref, bu_ref, wd_ref, bd_ref, o_ref,
                wg_scr, wu_scr, wd_scr, x_scr):
    i = pl.program_id(0)

    @pl.when(i >= nu_ref[0])
    def _():
        o_ref[...] = jnp.zeros(o_ref.shape, _F32)

    @pl.when(i < nu_ref[0])
    def _():
        prev = be_ref[jnp.maximum(i - 1, 0)]

        @pl.when((i == 0) | (be_ref[i] != prev))
        def _():
            wg_scr[...] = wg_ref[0].astype(_BF16)
            wu_scr[...] = wu_ref[0].astype(_BF16)
            wd_scr[...] = wd_ref[0].astype(_BF16)

        for j in range(SUBLANES):
            x_scr[:, j * LANES:(j + 1) * LANES] = _load_token_tiles(x_ref, MOE_ROWS, j).astype(_BF16)
        x = x_scr[...]
        gt = jnp.minimum(_dot(x, wg_scr[...]) + bg_ref[0], SWIGLU_LIMIT)
        up = jnp.clip(_dot(x, wu_scr[...]) + bu_ref[0], -SWIGLU_LIMIT, SWIGLU_LIMIT)
        act = ((up + 1.0) * (gt * jax.nn.sigmoid(SWIGLU_ALPHA * gt))).astype(_BF16)
        _store_token_tiles(o_ref, _dot(act, wd_scr[...]) + bd_ref[0])


def _moe_ffn(block_e, n_used, xs, n_blocks, w_gate, b_gate, w_up, b_up, w_down, b_down):
    E, D, F = w_gate.shape
    assert D == SUBLANES * LANES
    rows = MOE_ROWS * SUBLANES
    xblk = lambda i, be, nu: (jnp.minimum(i, nu[0] - 1), 0)
    wsel = lambda i, be, nu: (be[jnp.minimum(i, nu[0] - 1)], 0, 0)
    return pl.pallas_call(
        _moe_kernel,
        grid_spec=pltpu.PrefetchScalarGridSpec(
            num_scalar_prefetch=2,
            grid=(n_blocks,),
            in_specs=[pl.BlockSpec((rows, LANES), xblk),
                      pl.BlockSpec((1, D, F), wsel), pl.BlockSpec((1, 1, F), wsel),
                      pl.BlockSpec((1, D, F), wsel), pl.BlockSpec((1, 1, F), wsel),
                      pl.BlockSpec((1, F, D), wsel), pl.BlockSpec((1, 1, D), wsel)],
            out_specs=pl.BlockSpec((rows, LANES), lambda i, be, nu: (i, 0)),
            scratch_shapes=[pltpu.VMEM((D, F), _BF16), pltpu.VMEM((D, F), _BF16),
                            pltpu.VMEM((F, D), _BF16), pltpu.VMEM((MOE_ROWS, D), _BF16)]),
        out_shape=jax.ShapeDtypeStruct((n_blocks * rows, LANES), _F32),
        compiler_params=_params("arbitrary"),
        name="moe_ffn",
    )(block_e, n_used, xs, w_gate, b_gate.reshape(E, 1, F), w_up, b_up.reshape(E, 1, F),
      w_down, b_down.reshape(E, 1, D))


def _combine_kernel(start_ref, len_ref, slot_hbm, gate_hbm, ys_hbm, x1_ref, yp_ref, yr_ref,
                    idx_smem0, idx_smem1, gate_smem0, gate_smem1, pool, out_scr, sem_idx, sem_rows,
                    *, tm, n_prompt_tiles, n_tiles):
    i = pl.program_id(0)
    n_asg = tm * TOP_K
    idx_smem = (idx_smem0, idx_smem1)
    gate_smem = (gate_smem0, gate_smem1)

    def run_copies(tile, slot):
        return _all_run_copies(
            lambda hbm_row, pool_row, n: pltpu.make_async_copy(
                _rows(ys_hbm, hbm_row, n), _rows(pool.at[slot], pool_row, n), sem_rows.at[slot]),
            start_ref, len_ref, tile)

    def table_copies(tile, slot):
        return [pltpu.make_async_copy(slot_hbm.at[pl.ds(tile * n_asg, n_asg)], idx_smem[slot], sem_idx.at[slot]),
                pltpu.make_async_copy(gate_hbm.at[pl.ds(tile * n_asg, n_asg)], gate_smem[slot], sem_idx.at[slot])]

    def fetch(tile, slot):
        for cp in table_copies(tile, slot):
            cp.start()
        _start_all(run_copies(tile, slot))

    def step(slot):
        if slot == 0:
            @pl.when(i == 0)
            def _():
                fetch(0, 0)

        @pl.when(i + 1 < n_tiles)
        def _():
            fetch(i + 1, 1 - slot)

        for cp in table_copies(i, slot):
            cp.wait()
        _wait_all(run_copies(i, slot))

        def token(t, carry):
            at = pl.ds(pl.multiple_of(t * SUBLANES, SUBLANES), SUBLANES)
            acc = x1_ref[at, :]
            for k in range(TOP_K):
                row = pl.multiple_of(idx_smem[slot][k * tm + t] * SUBLANES, SUBLANES)
                acc = acc + gate_smem[slot][k * tm + t] * pool[slot, pl.ds(row, SUBLANES), :]
            out_scr[at, :] = acc
            return carry

        lax.fori_loop(0, tm, token, 0, unroll=4)

        def emit(o_ref):
            for j in range(SUBLANES):
                o_ref[:, j * LANES:(j + 1) * LANES] = _load_token_tiles(out_scr, tm, j)

        @pl.when(i < n_prompt_tiles)
        def _():
            emit(yp_ref)

        @pl.when(i >= n_prompt_tiles)
        def _():
            emit(yr_ref)

    for slot in range(2):
        @pl.when(i % 2 == slot)
        def _():
            step(slot)


def _combine(run_start, run_len, slot, gates, ys, x1_tiles, n_prompt, n_tiles, *, tm):
    D = SUBLANES * LANES
    npt = n_prompt // tm
    pre = lambda i, *_: (i, 0)
    return pl.pallas_call(
        functools.partial(_combine_kernel, tm=tm, n_prompt_tiles=npt, n_tiles=n_tiles),
        grid_spec=pltpu.PrefetchScalarGridSpec(
            num_scalar_prefetch=2,
            grid=(n_tiles,),
            in_specs=[pl.BlockSpec(memory_space=pl.ANY), pl.BlockSpec(memory_space=pl.ANY),
                      pl.BlockSpec(memory_space=pl.ANY), pl.BlockSpec((tm * SUBLANES, LANES), pre)],
            out_specs=[pl.BlockSpec((tm, D), lambda i, *_: (jnp.minimum(i, npt - 1), 0)),
                       pl.BlockSpec((tm, D), lambda i, *_: (jnp.maximum(i - npt, 0), 0))],
            scratch_shapes=[pltpu.SMEM((tm * TOP_K,), jnp.int32), pltpu.SMEM((tm * TOP_K,), jnp.int32),
                            pltpu.SMEM((tm * TOP_K,), _F32), pltpu.SMEM((tm * TOP_K,), _F32),
                            pltpu.VMEM((2, tm * TOP_K * SUBLANES, LANES), _F32),
                            pltpu.VMEM((tm * SUBLANES, LANES), _F32),
                            pltpu.SemaphoreType.DMA((2,)), pltpu.SemaphoreType.DMA((2,))]),
        out_shape=[jax.ShapeDtypeStruct((n_prompt, D), _F32),
                   jax.ShapeDtypeStruct(((n_tiles - npt) * tm, D), _F32)],
        compiler_params=_params("arbitrary"),
        name="combine",
    )(run_start, run_len, slot, gates, ys, x1_tiles)


def _pad_lanes(v, fill=0.0):
    v = v.reshape(1, -1).astype(_F32)
    return jnp.pad(v, ((0, 0), (0, LANES - v.shape[1])), constant_values=fill)


def _pad_rows(a, rows):
    return jnp.pad(a, ((0, rows - a.shape[0]), (0, 0)))


def _prep_w_in(w_in):
    off_a = CONV_DIM + DN_WIDTH
    off_mq = off_a + 2 * DN_HEADS
    ab = jnp.pad(w_in[:, off_a:off_mq], ((0, 0), (0, LANES - 2 * DN_HEADS)))
    return jnp.concatenate([w_in[:, :off_a], w_in[:, off_mq:], ab], axis=1).astype(_BF16)


def _moe(xn_tiles, x1_tiles, route_t, counts, n_prompt, n_tokens, w_gate, b_gate, w_up, b_up, w_down, b_down):
    tm = TOK_TILE
    n_tiles = -(-n_tokens // tm)
    tc = n_tiles * tm
    idx = route_t[0:TOP_K, :tc].astype(jnp.int32)
    gates = route_t[TOP_K:2 * TOP_K, :tc]
    rank = route_t[2 * TOP_K:3 * TOP_K, :tc].astype(jnp.int32)
    cnt = counts[:, 0].astype(jnp.int32)
    padded = (cnt + MOE_ROWS - 1) // MOE_ROWS * MOE_ROWS
    pends = jnp.cumsum(padded)
    pstarts = pends - padded
    n_blocks = -(-n_tokens * TOP_K // MOE_ROWS) + N_EXPERTS
    real = (jnp.arange(tc, dtype=jnp.int32) < n_tokens)[None, :]
    experts = jnp.arange(N_EXPERTS, dtype=jnp.int32)[:, None, None]
    onehot = (idx[None] == experts) & real[None]
    hist = jnp.sum(onehot.reshape(N_EXPERTS, TOP_K, n_tiles, tm), axis=(1, 3), dtype=jnp.int32).T
    before_tile = jnp.cumsum(hist, axis=0) - hist
    before_expert = jnp.cumsum(hist, axis=1) - hist
    run_start = (pstarts[None, :] + before_tile).reshape(-1)
    shift = jnp.repeat((before_tile - before_expert).T, tm, axis=1)
    slot = rank - jnp.sum(jnp.where(onehot, shift[:, None, :], 0), axis=0)
    per_tile = lambda d: d.reshape(TOP_K, n_tiles, tm).transpose(1, 0, 2).reshape(-1)
    slot_pack = per_tile(jnp.where(real, slot, tm * TOP_K))
    slot_read = per_tile(jnp.where(real, slot, 0))
    block_start = jnp.arange(n_blocks, dtype=jnp.int32) * MOE_ROWS
    block_e = jnp.minimum(jnp.sum((pends[None, :] <= block_start[:, None]).astype(jnp.int32), axis=1),
                          N_EXPERTS - 1)
    n_used = (pends[-1:] // MOE_ROWS).astype(jnp.int32)
    tail = jnp.where(cnt > 0, pends - MOE_ROWS, -1).astype(jnp.int32)
    run_len = hist.reshape(-1)
    xs = _dispatch(run_start, run_len, tail, n_used, slot_pack, xn_tiles, n_tiles, n_blocks, tm=tm)
    ys = _moe_ffn(block_e, n_used, xs, n_blocks, w_gate, b_gate, w_up, b_up, w_down, b_down)
    return _combine(run_start, run_len, slot_read, per_tile(gates), ys, x1_tiles, n_prompt, n_tiles, tm=tm)


def _layer(xp, xs_, cache_k, cache_v, conv_s, ssm_s, page_table, lw):
    (ln1_w, w_in, w_conv, a_log, dt_bias, dn_norm_w, q_norm_w, k_norm_w, w_o, ln2_w, w_router,
     b_router, w_gate, b_gate, w_up, b_up, w_down, b_down) = lw
    B, S, D = xp.shape
    DB, L, _ = xs_.shape
    n_phys, H, PS, dh = cache_k.shape
    n_pages = page_table.shape[1]
    ppb = MOBA_BLOCK // PS
    Tp, Ts = B * S, DB * L
    assert (n_pages * PS) % MOBA_BLOCK == 0, "paged past must end on a MoBA block boundary"
    assert CONV_W - 1 <= L <= SUBLANES and Ts <= POST_ROWS and POST_ROWS % TOK_TILE == 0
    assert Tp % PROJ_ROWS == 0 and S % GDN_ROWS == 0 and Tp % POST_ROWS == 0
    cur = n_pages // ppb
    n_sel = min(MOBA_TOPK, cur)
    assert n_sel > 0

    w_all = _prep_w_in(w_in)
    ln1 = ln1_w.reshape(1, D)
    qn = q_norm_w.reshape(1, dh)
    kn = k_norm_w.reshape(1, dh)
    alog_row = _pad_lanes(a_log)
    dtb_row = _pad_lanes(dt_bias)
    dnw = dn_norm_w.reshape(1, dh)

    u_p, z_p, ab_p, mq_p, mk_p, mv_p = _proj(xp.reshape(Tp, D), ln1, w_all, qn, kn, tm=PROJ_ROWS, seq=(B, S))
    od_p, ssm_p = _gdn(u_p.reshape(B, S, CONV_DIM), z_p.reshape(B, S, DN_WIDTH), ab_p.reshape(B, S, LANES),
                       jnp.zeros((B, SUBLANES, CONV_DIM), _F32), jnp.zeros((B, DN_HEADS, dh, dh), _F32),
                       w_conv, alog_row, dtb_row, dnw, lt=GDN_ROWS, valid_len=GDN_ROWS)
    om_p, psum = _moba_prompt(mq_p, mk_p, mv_p, cache_k.reshape(n_phys * H, PS, dh))
    conv_p = u_p.reshape(B, S, CONV_DIM)[:, S - (CONV_W - 1):]

    u_s, z_s, ab_s, mq_s, mk_s, mv_s = _proj(xs_.reshape(Ts, D), ln1, w_all, qn, kn, tm=Ts)
    padl = lambda a: jnp.pad(a.reshape(DB, L, -1), ((0, 0), (0, DN_CHUNK - L), (0, 0)))
    conv0 = jnp.pad(conv_s, ((0, 0), (SUBLANES - (CONV_W - 1), 0), (0, 0)))
    od_s, ssm_s_new = _gdn(padl(u_s), padl(z_s), padl(ab_s), conv0, ssm_s, w_conv, alog_row, dtb_row, dnw,
                           lt=DN_CHUNK, valid_len=L)
    od_s = od_s[:, :L].reshape(Ts, DN_WIDTH)
    conv_s_new = jnp.concatenate([conv_s, u_s.reshape(DB, L, CONV_DIM)], axis=1)[:, L:]

    heads = lambda a: a.reshape(DB, L, H, dh).transpose(0, 2, 1, 3)
    pad8 = lambda a: jnp.pad(a, ((0, 0), (0, 0), (0, SUBLANES - L), (0, 0)))
    q8, k8, v8 = pad8(heads(mq_s)), pad8(heads(mk_s)), pad8(heads(mv_s))
    psum = psum.reshape(n_phys, H * dh)
    sel = _moba_sel(page_table, psum, q8, ppb=ppb, n_sel=n_sel, rows_per_blk=MOBA_BLOCK)
    sel = sel[:, :, :L, :n_sel]
    logical = sel[..., None] * ppb + jnp.arange(ppb, dtype=jnp.int32)
    phys = page_table[jnp.arange(DB)[:, None, None, None, None], logical]
    om_s = _moba_sample(phys.reshape(-1).astype(jnp.int32), q8, k8, v8, cache_k, cache_v, n_new=L)
    om_s = om_s[:, :, :L].transpose(0, 2, 1, 3).reshape(Ts, H * dh).astype(_BF16)

    wo = w_o.astype(_BF16)
    wr = w_router.T
    wr_hi = wr.astype(_BF16)
    wr_lo = (wr - wr_hi.astype(_F32)).astype(_BF16)
    x1_tiles, xn_tiles, route_t, cnt = _post(
        xp.reshape(Tp, D), od_p.reshape(Tp, DN_WIDTH), om_p.reshape(Tp, ATT_WIDTH),
        _pad_rows(xs_.reshape(Ts, D), POST_ROWS), _pad_rows(od_s, POST_ROWS), _pad_rows(om_s, POST_ROWS),
        wo[:DN_WIDTH], wo[DN_WIDTH:], ln2_w.reshape(1, D), wr_hi, wr_lo, b_router.reshape(N_EXPERTS, 1),
        tm=POST_ROWS, n_tokens=Tp + Ts)
    y_p, y_s = _moe(xn_tiles, x1_tiles, route_t, cnt, Tp, Tp + Ts, w_gate, b_gate, w_up, b_up, w_down, b_down)
    return (y_p.reshape(B, S, D), y_s[:Ts].reshape(DB, L, D), mk_p, mv_p, conv_p, ssm_p,
            heads(mk_s), heads(mv_s), conv_s_new, ssm_s_new)


def kernel(x_prompt, x_sample, cache_k, cache_v, state_conv, state_ssm, page_table, ln1_w, w_in, w_conv,
           a_log, dt_bias, dn_norm_w, q_norm_w, k_norm_w, w_o, ln2_w, w_router, b_router, w_gate, b_gate,
           w_up, b_up, w_down, b_down):
    weights = (ln1_w, w_in, w_conv, a_log, dt_bias, dn_norm_w, q_norm_w, k_norm_w, w_o, ln2_w, w_router,
               b_router, w_gate, b_gate, w_up, b_up, w_down, b_down)
    depth = w_in.shape[0]
    yp, ys = x_prompt, x_sample
    outs = [[] for _ in range(8)]
    for l in range(depth):
        res = _layer(yp, ys, cache_k[l], cache_v[l], state_conv[l], state_ssm[l], page_table,
                     tuple(w[l] for w in weights))
        yp, ys = res[0], res[1]
        for acc, r in zip(outs, res[2:]):
            acc.append(r)
    return (yp, ys) + tuple(jnp.stack(o) for o in outs)
```
